```python
import jax, jax.numpy as jnp
from jax import lax
import numpy as np

D_MODEL = 1024
BATCH = 32
SEQ = 256
DEPTH = 2
DEC_BATCH = 8
DEC_SEQ = 2048
PAST_LEN = 256

GRID_W = 64
EPS = 1e-6
M_HEADS = 6
M_HD = 64
M_W = M_HEADS * M_HD
CHUNK = 64
R_HEADS = 6
R_HD = 64
R_W = R_HEADS * R_HD
R_LORA_W = 64
R_LORA_A = 64
RWKV_DECAY_SCALE = 0.6065306597126334
L_BLOCKS = 4
L_BD = 64
L_W = L_BLOCKS * L_BD
CONV_W = 4
LRU_C = 8.0
MIX_W = M_W + R_W + L_W
M_COLS = 5 * M_W + 4 * M_HEADS
R_SHIFT = 3 * R_W + 2 * R_LORA_W + 2 * R_LORA_A
R_COLS = R_SHIFT + R_W
L_COLS = 2 * L_W
IN_COLS = M_COLS + R_COLS + L_COLS

kernel_name = 'hymba_bidir_mlstm_rwkv7_rglru_dit_step'


def rmsnorm(x, g):
    xf = x.astype(jnp.float32)
    return xf * lax.rsqrt(jnp.mean(xf * xf, axis=-1, keepdims=True) + EPS) * g


def head_rmsnorm(h, g):
    B, T = h.shape[0], h.shape[1]
    hn = h * lax.rsqrt(jnp.mean(h * h, axis=-1, keepdims=True) + EPS)
    return hn.reshape(B, T, -1) * g


def seq_shift(u):
    left = jnp.pad(u[:, :-1], ((0, 0), (1, 0), (0, 0)))
    right = jnp.pad(u[:, 1:], ((0, 0), (0, 1), (0, 0)))
    return 0.5 * (left + right)


def grid_shift(u):
    B, T, C = u.shape
    rows = T // GRID_W
    g = u.reshape(B, rows, GRID_W, C)
    up = jnp.pad(g[:, :-1], ((0, 0), (1, 0), (0, 0), (0, 0)))
    down = jnp.pad(g[:, 1:], ((0, 0), (0, 1), (0, 0), (0, 0)))
    left = jnp.pad(g[:, :, :-1], ((0, 0), (0, 0), (1, 0), (0, 0)))
    right = jnp.pad(g[:, :, 1:], ((0, 0), (0, 0), (0, 1), (0, 0)))
    return (0.25 * (up + down + left + right)).reshape(B, T, C)


def mlstm_chunkwise(q, k, v, i_pre, logf, C0, n0, m0):
    B, H, T, Dh = q.shape
    nc = T // CHUNK
    to_c = lambda t: jnp.moveaxis(t.reshape(B, H, nc, CHUNK, *t.shape[3:]), 2, 0)
    mask = jnp.tril(jnp.ones((CHUNK, CHUNK), dtype=bool))

    def step(carry, inp):
        C, n, m = carry
        qc, kc, vc, ic, fc = inp
        b = jnp.cumsum(fc, axis=-1)
        dmat = jnp.where(mask, b[..., :, None] - b[..., None, :] + ic[..., None, :], -jnp.inf)
        inter = b + m[..., None]
        m_t = jnp.maximum(inter, jnp.max(dmat, axis=-1))
        s = jnp.einsum('bhtd,bhsd->bhts', qc, kc) * jnp.exp(dmat - m_t[..., None])
        sc = jnp.exp(inter - m_t)
        num = sc[..., None] * jnp.einsum('bhtd,bhde->bhte', qc, C) + jnp.einsum('bhts,bhse->bhte', s, vc)
        den = sc * jnp.einsum('bhtd,bhd->bht', qc, n) + jnp.sum(s, axis=-1)
        h = num / jnp.maximum(jnp.abs(den), jnp.exp(-m_t))[..., None]
        bL = b[..., -1]
        g = bL[..., None] - b + ic
        m_new = jnp.maximum(bL + m, jnp.max(g, axis=-1))
        dec = jnp.exp(bL + m - m_new)
        wk = jnp.exp(g - m_new[..., None])
        C_new = dec[..., None, None] * C + jnp.einsum('bhs,bhsd,bhse->bhde', wk, kc, vc)
        n_new = dec[..., None] * n + jnp.einsum('bhs,bhsd->bhd', wk, kc)
        return (C_new, n_new, m_new), h

    (C, n, m), hs = lax.scan(step, (C0, n0, m0), (to_c(q), to_c(k), to_c(v), to_c(i_pre), to_c(logf)))
    return jnp.moveaxis(hs, 0, 2).reshape(B, H, T, Dh), (C, n, m)


def mlstm_branch(u_m, b_i, b_f, norm_g, C0, n0, m0):
    B, T, _ = u_m.shape
    q, k, v, o, z, ig, fg = jnp.split(u_m, [M_W, 2 * M_W, 3 * M_W, 4 * M_W, 5 * M_W, 5 * M_W + 2 * M_HEADS], axis=-1)
    heads = lambda t: t.reshape(B, T, M_HEADS, M_HD).transpose(0, 2, 1, 3)
    q, k, v = heads(q), heads(k) * (M_HD ** -0.5), heads(v)
    ig = (ig.reshape(B, T, 2, M_HEADS) + b_i).transpose(0, 2, 3, 1)
    lf = jax.nn.log_sigmoid(fg.reshape(B, T, 2, M_HEADS) + b_f).transpose(0, 2, 3, 1)
    h_f, (Cf, nf, mf) = mlstm_chunkwise(q, k, v, ig[:, 0], lf[:, 0], C0[:, 0], n0[:, 0], m0[:, 0])
    flip = lambda t: jnp.flip(t, axis=2)
    h_b, (Cb, nb, mb) = mlstm_chunkwise(flip(q), flip(k), flip(v), jnp.flip(ig[:, 1], axis=-1),
                                        jnp.flip(lf[:, 1], axis=-1), C0[:, 1], n0[:, 1], m0[:, 1])
    h = (h_f + flip(h_b)).transpose(0, 2, 1, 3)
    y = head_rmsnorm(h, norm_g) * jax.nn.sigmoid(o) * jax.nn.silu(z)
    return y, (jnp.stack([Cf, Cb], 1), jnp.stack([nf, nb], 1), jnp.stack([mf, mb], 1))


def rwkv_step(S, inp):
    r, w, kh, b, kt, v = inp
    Sk = jnp.einsum('bhvk,bhk->bhv', S, kh)
    S = S * w[:, :, None, :] - Sk[..., None] * b[:, :, None, :] + v[..., None] * kt[:, :, None, :]
    return S, jnp.einsum('bhvk,bhk->bhv', S, r)


def rwkv_branch(u_rs, u_rz, lp, S0, shift_fn):
    B, T, _ = u_rs.shape
    blk = u_rs + lp['r_mu'] * (shift_fn(u_rs) - u_rs)
    r, k, v, wl, al = jnp.split(blk, [R_W, 2 * R_W, 3 * R_W, 3 * R_W + 2 * R_LORA_W], axis=-1)
    wl = wl.reshape(B, T, 2, R_LORA_W)
    al = al.reshape(B, T, 2, R_LORA_A)
    log_w = -RWKV_DECAY_SCALE * jax.nn.sigmoid(lp['r_w0'] + jnp.einsum('btdr,drc->btdc', jnp.tanh(wl), lp['r_w2']))
    a = jax.nn.sigmoid(lp['r_a0'] + jnp.einsum('btdr,drc->btdc', al, lp['r_a2']))
    kappa = (k * lp['r_kk']).reshape(B, T, R_HEADS, R_HD)
    kh = (kappa / jnp.maximum(jnp.sqrt(jnp.sum(kappa * kappa, axis=-1, keepdims=True)), 1e-12)).reshape(B, T, R_W)
    kt = k[:, :, None, :] * (1.0 + (a - 1.0) * lp['r_ka'])
    bvec = kh[:, :, None, :] * a
    tm = lambda t: t.reshape(B, T, R_HEADS, R_HD).transpose(1, 0, 2, 3)
    r_t, kh_t, v_t = tm(r), tm(kh), tm(v)
    ys, finals = [], []
    for d in range(2):
        Sd, yd = lax.scan(rwkv_step, S0[:, d],
                          (r_t, jnp.exp(tm(log_w[:, :, d])), kh_t, tm(bvec[:, :, d]), tm(kt[:, :, d]), v_t),
                          reverse=(d == 1))
        ys.append(yd)
        finals.append(Sd)
    y = (ys[0] + ys[1]).transpose(1, 0, 2, 3)
    bonus = jnp.sum((r * k * lp['r_rk']).reshape(B, T, R_HEADS, R_HD), axis=-1, keepdims=True) * v.reshape(B, T, R_HEADS, R_HD)
    out = head_rmsnorm(y, lp['r_norm']) + bonus.reshape(B, T, R_W)
    return out * jax.nn.silu(u_rz), jnp.stack(finals, 1)


def lru_combine(e1, e2):
    a1, b1 = e1
    a2, b2 = e2
    return a1 * a2, a2 * b1 + b2


def rglru_branch(u_l, lp, h0):
    B, T, _ = u_l.shape
    xl, z = jnp.split(u_l, [L_W], axis=-1)
    xp = jnp.pad(xl, ((0, 0), (CONV_W // 2, CONV_W - 1 - CONV_W // 2), (0, 0)))
    xc = sum(xp[:, j:j + T] * lp['l_conv'][j] for j in range(CONV_W)) + lp['l_conv_b']
    xb = xc.reshape(B, T, L_BLOCKS, L_BD)
    hs, finals = [], []
    for d in range(2):
        rg = jax.nn.sigmoid(jnp.einsum('btni,nio->btno', xb, lp['l_wa'][d]).reshape(B, T, L_W) + lp['l_ba'][d])
        ig = jax.nn.sigmoid(jnp.einsum('btni,nio->btno', xb, lp['l_wx'][d]).reshape(B, T, L_W) + lp['l_bx'][d])
        log_a = -LRU_C * rg * jax.nn.softplus(-lp['l_lambda'][d])
        a = jnp.exp(log_a)
        bt = jnp.sqrt(-jnp.expm1(2.0 * log_a)) * ig * xc
        pos = 0 if d == 0 else T - 1
        bt = bt.at[:, pos].add(a[:, pos] * h0[:, d])
        _, h = lax.associative_scan(lru_combine, (a, bt), reverse=(d == 1), axis=1)
        hs.append(h)
        finals.append(h[:, -1] if d == 0 else h[:, 0])
    return (hs[0] + hs[1]) * jax.nn.silu(z), jnp.stack(finals, 1)


def trunk_layer(x, shift, scale, gate, lp, states, shift_fn):
    h = rmsnorm(x, lp['g_pre']) * (1.0 + scale) + shift
    u = jnp.einsum('btd,dc->btc', h, lp['w_in']).astype(jnp.float32)
    u_m, u_rs, u_rz, u_l = jnp.split(u, [M_COLS, M_COLS + R_SHIFT, M_COLS + R_COLS], axis=-1)
    mC, mn, mm, rS, lh = states
    y_m, (mC2, mn2, mm2) = mlstm_branch(u_m, lp['m_bi'], lp['m_bf'], lp['m_norm'], mC, mn, mm)
    y_r, rS2 = rwkv_branch(u_rs, u_rz, lp, rS, shift_fn)
    y_l, lh2 = rglru_branch(u_l, lp, lh)
    mixed = jnp.concatenate([y_m, y_r, y_l], axis=-1)
    o = jnp.einsum('btc,cd->btd', mixed, lp['w_out'])
    return x + gate * rmsnorm(o, lp['g_post']), (mC2, mn2, mm2, rS2, lh2)


def setup_inputs(seed: int = 0) -> dict:
    key = jax.random.key(seed)
    ks = iter(jax.random.split(key, 48))
    nrm = lambda shape, s: jax.random.normal(next(ks), shape, jnp.float32) * s
    D = D_MODEL
    a_init = jax.random.uniform(next(ks), (DEPTH, 2, L_W), jnp.float32, 0.9, 0.999)
    s_init = a_init ** (1.0 / LRU_C)
    return {
        'x_prompt': nrm((BATCH, SEQ, D), 1.0),
        'x_sample': nrm((DEC_BATCH, DEC_SEQ, D), 1.0),
        'c': nrm((DEC_BATCH, D), 1.0),
        'state_mlstm_C': nrm((DEC_BATCH, DEPTH, 2, M_HEADS, M_HD, M_HD), 0.1),
        'state_mlstm_n': nrm((DEC_BATCH, DEPTH, 2, M_HEADS, M_HD), 0.1),
        'state_mlstm_m': nrm((DEC_BATCH, DEPTH, 2, M_HEADS), 0.5),
        'state_rwkv': nrm((DEC_BATCH, DEPTH, 2, R_HEADS, R_HD, R_HD), 0.1),
        'state_rglru': nrm((DEC_BATCH, DEPTH, 2, L_W), 0.5),
        'c_ctx': nrm((D,), 1.0),
        'g_pre': 1.0 + nrm((DEPTH, D), 0.02),
        'g_post': 1.0 + nrm((DEPTH, D), 0.02),
        'w_mod': nrm((DEPTH, D, 3 * D), 0.5 * D ** -0.5),
        'b_mod': nrm((DEPTH, 3 * D), 0.02),
        'w_in': nrm((DEPTH, D, IN_COLS), D ** -0.5),
        'w_out': nrm((DEPTH, MIX_W, D), MIX_W ** -0.5),
        'm_bi': nrm((DEPTH, 2, M_HEADS), 0.1),
        'm_bf': 3.0 + 3.0 * jax.random.uniform(next(ks), (DEPTH, 2, M_HEADS), jnp.float32),
        'm_norm': 1.0 + nrm((DEPTH, M_W), 0.02),
        'r_mu': jax.random.uniform(next(ks), (DEPTH, R_SHIFT), jnp.float32),
        'r_w0': nrm((DEPTH, 2, R_W), 0.5),
        'r_w2': nrm((DEPTH, 2, R_LORA_W, R_W), 0.5 * R_LORA_W ** -0.5),
        'r_a0': nrm((DEPTH, 2, R_W), 0.1),
        'r_a2': nrm((DEPTH, 2, R_LORA_A, R_W), 0.5 * R_LORA_A ** -0.5),
        'r_kk': 0.85 + nrm((DEPTH, R_W), 0.05),
        'r_ka': 1.0 + nrm((DEPTH, R_W), 0.05),
        'r_rk': nrm((DEPTH, R_W), 0.1),
        'r_norm': 1.0 + nrm((DEPTH, R_W), 0.02),
        'l_conv': nrm((DEPTH, CONV_W, L_W), CONV_W ** -0.5),
        'l_conv_b': nrm((DEPTH, L_W), 0.02),
        'l_wa': nrm((DEPTH, 2, L_BLOCKS, L_BD, L_BD), L_BD ** -0.5),
        'l_ba': nrm((DEPTH, 2, L_W), 0.02),
        'l_wx': nrm((DEPTH, 2, L_BLOCKS, L_BD, L_BD), L_BD ** -0.5),
        'l_bx': nrm((DEPTH, 2, L_W), 0.02),
        'l_lambda': jnp.log(s_init) - jnp.log1p(-s_init),
    }


def reference(x_prompt, x_sample, c, state_mlstm_C, state_mlstm_n, state_mlstm_m, state_rwkv, state_rglru,
              c_ctx, g_pre, g_post, w_mod, b_mod, w_in, w_out, m_bi, m_bf, m_norm, r_mu, r_w0, r_w2, r_a0,
              r_a2, r_kk, r_ka, r_rk, r_norm, l_conv, l_conv_b, l_wa, l_ba, l_wx, l_bx, l_lambda):
    f32 = jnp.float32
    Bp = x_prompt.shape[0]
    zero_states = (jnp.zeros((Bp, 2, M_HEADS, M_HD, M_HD), f32), jnp.zeros((Bp, 2, M_HEADS, M_HD), f32),
                   jnp.zeros((Bp, 2, M_HEADS), f32), jnp.zeros((Bp, 2, R_HEADS, R_HD, R_HD), f32),
                   jnp.zeros((Bp, 2, L_W), f32))
    xp, xs = x_prompt, x_sample
    mC_l, mn_l, mm_l, rS_l, lh_l = [], [], [], [], []
    for l in range(DEPTH):
        lp = dict(g_pre=g_pre[l], g_post=g_post[l], w_in=w_in[l], w_out=w_out[l], m_bi=m_bi[l], m_bf=m_bf[l],
                  m_norm=m_norm[l], r_mu=r_mu[l], r_w0=r_w0[l], r_w2=r_w2[l], r_a0=r_a0[l], r_a2=r_a2[l],
                  r_kk=r_kk[l], r_ka=r_ka[l], r_rk=r_rk[l], r_norm=r_norm[l], l_conv=l_conv[l],
                  l_conv_b=l_conv_b[l], l_wa=l_wa[l], l_ba=l_ba[l], l_wx=l_wx[l], l_bx=l_bx[l],
                  l_lambda=l_lambda[l])
        sh_c, sc_c, gt_c = jnp.split(jax.nn.silu(c_ctx) @ w_mod[l] + b_mod[l], 3, axis=-1)
        xp, st = trunk_layer(xp, sh_c, sc_c, gt_c, lp, zero_states, seq_shift)
        mC_l.append(st[0]); mn_l.append(st[1]); mm_l.append(st[2]); rS_l.append(st[3]); lh_l.append(st[4])
        sh_s, sc_s, gt_s = jnp.split(jax.nn.silu(c) @ w_mod[l] + b_mod[l], 3, axis=-1)
        cache = (state_mlstm_C[:, l].astype(f32), state_mlstm_n[:, l].astype(f32), state_mlstm_m[:, l].astype(f32),
                 state_rwkv[:, l].astype(f32), state_rglru[:, l].astype(f32))
        xs, _ = trunk_layer(xs, sh_s[:, None, :], sc_s[:, None, :], gt_s[:, None, :], lp, cache, grid_shift)
    y_prompt, y_sample = xp, xs
    new_mlstm_C = jnp.stack(mC_l, axis=1)
    new_mlstm_n = jnp.stack(mn_l, axis=1)
    new_mlstm_m = jnp.stack(mm_l, axis=1)
    new_rwkv = jnp.stack(rS_l, axis=1)
    new_rglru = jnp.stack(lh_l, axis=1)
    return (y_prompt, y_sample, new_mlstm_C, new_mlstm_n, new_mlstm_m, new_rwkv, new_rglru)
```

```python
import functools

import numpy as np
import jax
import jax.numpy as jnp
from jax import lax
from jax.experimental import pallas as pl
from jax.experimental.pallas import tpu as pltpu

F32 = jnp.float32
BF16 = jnp.bfloat16
HIGHEST = lax.Precision.HIGHEST

D_MODEL = 1024
EPS = 1e-6
HD = 64
N_HEADS = 6
N_PAIRS = N_HEADS // 2
HW = N_HEADS * HD
CHUNK = 64
GRID_W = 64
LORA = 64
R_SHIFT = 3 * HW + 4 * LORA
L_W = 256
L_BLOCKS = 4
CONV_W = 4
LRU_C = 8.0
RWKV_DECAY_SCALE = 0.6065306597126334
M_COLS = 5 * HW + 4 * N_HEADS
IN_COLS = M_COLS + R_SHIFT + HW + 2 * L_W

LANES = 128
SUBLANES = 8
LRU_BLOCK = 128
VMEM_LIMIT = 56 * 1024 * 1024

SEG_WIDTHS = (3 * HW, 2 * HW, LANES, LANES, R_SHIFT, HW, 2 * L_W)
SEG_OFFS = tuple(int(v) for v in np.cumsum((0,) + SEG_WIDTHS))
IN_COLS_PAD = SEG_OFFS[-1]
N_TILE = 512


def _gate_col(d, is_f, p):
    return 8 * d + 4 * is_f + p


def _bdot(a, b):
    return jnp.dot(a.astype(BF16), b.astype(BF16), preferred_element_type=F32)


def _bdot_nt(a, b):
    return lax.dot_general(a.astype(BF16), b.astype(BF16), (((1,), (1,)), ((), ())),
                           preferred_element_type=F32)


def _bdot_tn(a, b):
    return lax.dot_general(a.astype(BF16), b.astype(BF16), (((0,), (0,)), ((), ())),
                           preferred_element_type=F32)


def _fdot(a, b):
    return jnp.dot(a, b, precision=HIGHEST, preferred_element_type=F32)


def _sigmoid(x):
    return 1.0 / (1.0 + jnp.exp(-x))


def _silu(x):
    return x * _sigmoid(x)


def _softplus(x):
    return jnp.maximum(x, 0.0) + jnp.log1p(jnp.exp(-jnp.abs(x)))


def _log_sigmoid(x):
    return -_softplus(-x)


def _iota2(shape, dim):
    return lax.broadcasted_iota(jnp.int32, shape, dim)


def _head_ones(n):
    return (_iota2((n, n), 0) // HD == _iota2((n, n), 1) // HD).astype(F32)


def _bd_stack(x, lo):
    return jnp.concatenate([jnp.where(lo, x, 0.0), jnp.where(lo, 0.0, x)], axis=0)


def _pair_col(q, j, lo):
    return jnp.where(lo, q[0:CHUNK, j:j + 1], q[CHUNK:2 * CHUNK, j:j + 1])


def _mod_kernel(c_ref, w_ref, b_ref, o_ref):
    o_ref[...] = _fdot(_silu(c_ref[...]), w_ref[...]) + b_ref[...]


def _modulation(cc, w_mod, b_mod):
    depth, d, n = w_mod.shape
    rows = cc.shape[0]
    return pl.pallas_call(
        _mod_kernel,
        grid=(depth, n // N_TILE),
        in_specs=[pl.BlockSpec((rows, d), lambda l, j: (0, 0)),
                  pl.BlockSpec((None, d, N_TILE), lambda l, j: (l, 0, j)),
                  pl.BlockSpec((None, 1, N_TILE), lambda l, j: (l, 0, j))],
        out_specs=pl.BlockSpec((None, rows, N_TILE), lambda l, j: (l, 0, j)),
        out_shape=jax.ShapeDtypeStruct((depth, rows, n), F32),
        compiler_params=pltpu.CompilerParams(dimension_semantics=("arbitrary", "arbitrary")),
        name="modulation",
    )(cc, w_mod, b_mod.reshape(depth, 1, n))


def _inproj_kernel(x_ref, mod_ref, g_ref, w_ref, *out_refs):
    x = x_ref[...]
    d = x.shape[-1]
    mod = mod_ref[...]
    h = x * lax.rsqrt(jnp.mean(x * x, axis=-1, keepdims=True) + EPS) * g_ref[...]
    h = (h * (1.0 + mod[:, d:2 * d]) + mod[:, 0:d]).astype(BF16)
    for o_ref, a, b in zip(out_refs, SEG_OFFS[:-1], SEG_OFFS[1:]):
        for n0 in range(a, b, N_TILE):
            n1 = min(n0 + N_TILE, b)
            o_ref[:, n0 - a:n1 - a] = jnp.dot(h, w_ref[:, n0:n1], preferred_element_type=F32)


def _inproj(x, mod, g_pre, w_in_p, per_seq_mod):
    bsz, t, d = x.shape
    tm = min(t, 512)
    mod_idx = (lambda b, i: (b, 0, 0)) if per_seq_mod else (lambda b, i: (0, 0, 0))
    return pl.pallas_call(
        _inproj_kernel,
        grid=(bsz, t // tm),
        in_specs=[pl.BlockSpec((None, tm, d), lambda b, i: (b, i, 0)),
                  pl.BlockSpec((None, 1, 3 * d), mod_idx),
                  pl.BlockSpec((1, d), lambda b, i: (0, 0)),
                  pl.BlockSpec((d, IN_COLS_PAD), lambda b, i: (0, 0), pipeline_mode=pl.Buffered(1))],
        out_specs=[pl.BlockSpec((None, tm, w), lambda b, i: (b, i, 0)) for w in SEG_WIDTHS],
        out_shape=[jax.ShapeDtypeStruct((bsz, t, w), F32) for w in SEG_WIDTHS],
        compiler_params=pltpu.CompilerParams(dimension_semantics=("arbitrary", "arbitrary"),
                                             vmem_limit_bytes=VMEM_LIMIT),
        name="inproj",
    )(x, mod, g_pre, w_in_p)


def _outproj_kernel(ym_ref, yr_ref, yl_ref, x_ref, mod_ref, g_ref, w_ref, o_ref):
    d = x_ref.shape[-1]
    o = jnp.dot(ym_ref[...], w_ref[0:HW, :], preferred_element_type=F32)
    o = o + jnp.dot(yr_ref[...], w_ref[HW:2 * HW, :], preferred_element_type=F32)
    o = o + jnp.dot(yl_ref[...], w_ref[2 * HW:2 * HW + L_W, :], preferred_element_type=F32)
    on = o * lax.rsqrt(jnp.mean(o * o, axis=-1, keepdims=True) + EPS) * g_ref[...]
    o_ref[...] = x_ref[...] + mod_ref[:, 2 * d:3 * d] * on


def _outproj(ym, yr, yl, x, mod, g_post, w_out_b, per_seq_mod):
    bsz, t, d = x.shape
    tm = min(t, 512)
    mod_idx = (lambda b, i: (b, 0, 0)) if per_seq_mod else (lambda b, i: (0, 0, 0))
    tok = lambda w: pl.BlockSpec((None, tm, w), lambda b, i: (b, i, 0))
    return pl.pallas_call(
        _outproj_kernel,
        grid=(bsz, t // tm),
        in_specs=[tok(HW), tok(HW), tok(L_W), tok(d),
                  pl.BlockSpec((None, 1, 3 * d), mod_idx),
                  pl.BlockSpec((1, d), lambda b, i: (0, 0)),
                  pl.BlockSpec((2 * HW + L_W, d), lambda b, i: (0, 0))],
        out_specs=tok(d),
        out_shape=jax.ShapeDtypeStruct((bsz, t, d), F32),
        compiler_params=pltpu.CompilerParams(dimension_semantics=("arbitrary", "arbitrary"),
                                             vmem_limit_bytes=VMEM_LIMIT),
        name="outproj",
    )(ym, yr, yl, x, mod, g_post, w_out_b)


def _mlstm_kernel(qkv_ref, oz_ref, ge_ref, go_ref, gb_ref, ng_ref, cn0_ref, m0_ref,
                  y_ref, cn_out_ref, m_out_ref, hacc, cn_s, m_s):
    t = qkv_ref.shape[0]
    nc = t // CHUNK
    lane = _iota2((CHUNK, LANES), 1)
    row = _iota2((CHUNK, LANES), 0)
    lo = lane < HD
    s_idx = lane % HD
    causal = (s_idx <= row, s_idx >= row)
    r2 = _iota2((2 * CHUNK, 2 * CHUNK), 0)
    c2 = _iota2((2 * CHUNK, 2 * CHUNK), 1)
    same = (r2 // CHUNK) == (c2 // CHUNK)
    cum = ((same & (c2 <= r2)).astype(F32), (same & (c2 >= r2)).astype(F32))
    bd_ones = same.astype(F32)
    bd_mask2 = jnp.concatenate([same, same], axis=1)
    colid = _iota2((1, LANES), 1)
    is_f = (colid % 8) >= 4
    gb = gb_ref[...]

    hacc[...] = jnp.zeros_like(hacc)
    cn_s[...] = cn0_ref[...]
    m_s[...] = m0_ref[...]

    def gate_table(t0, d):
        ve = ge_ref[pl.ds(t0, CHUNK), :] + gb[0:1, :]
        vo = go_ref[pl.ds(t0, CHUNK), :] + gb[1:2, :]
        ve = jnp.where(is_f, _log_sigmoid(ve), ve)
        vo = jnp.where(is_f, _log_sigmoid(vo), vo)
        x = jnp.concatenate([ve, vo], axis=0)
        q = jnp.where(is_f, _fdot(cum[d], x), x)
        return q, q.T

    def process(c, d):
        t0 = pl.multiple_of(c * CHUNK, CHUNK)
        q_tab, q_tab_t = gate_table(t0, d)
        last = CHUNK - 1 if d == 0 else 0
        for p in range(N_PAIRS):
            sl = slice(p * LANES, (p + 1) * LANES)
            ji, jf = _gate_col(d, 0, p), _gate_col(d, 1, p)
            bcol = _pair_col(q_tab, jf, lo)
            icol = _pair_col(q_tab, ji, lo)
            brow = q_tab_t[jf:jf + 1, :]
            irow = q_tab_t[ji:ji + 1, :]
            qh = qkv_ref[pl.ds(t0, CHUNK), p * LANES:(p + 1) * LANES]
            kh = qkv_ref[pl.ds(t0, CHUNK), HW + p * LANES:HW + (p + 1) * LANES] * (HD ** -0.5)
            vh = qkv_ref[pl.ds(t0, CHUNK), 2 * HW + p * LANES:2 * HW + (p + 1) * LANES]
            mprev = m_s[d, p]
            cn = cn_s[d, p]
            dmat = jnp.where(causal[d], bcol - brow + irow, -jnp.inf)
            inter = bcol + mprev
            mx = jnp.where(lo,
                           jnp.max(jnp.where(lo, dmat, -jnp.inf), axis=1, keepdims=True),
                           jnp.max(jnp.where(lo, -jnp.inf, dmat), axis=1, keepdims=True))
            m_t = jnp.maximum(inter, mx)
            s = _bdot_nt(qh, _bd_stack(kh, lo)) * jnp.exp(dmat - m_t)
            sc = jnp.exp(inter - m_t)
            a1 = _bdot(qh, cn)
            a2 = _bdot(s, jnp.concatenate([_bd_stack(vh, lo), bd_ones], axis=1))
            num = sc * a1[:, 0:LANES] + a2[:, 0:LANES]
            den = sc * a1[:, LANES:] + a2[:, LANES:]
            h = num / jnp.maximum(jnp.abs(den), jnp.exp(-m_t))
            hacc[pl.ds(t0, CHUNK), sl] = hacc[pl.ds(t0, CHUNK), sl] + h
            b_last = bcol[last:last + 1, :]
            g = b_last - bcol + icol
            m_new = jnp.maximum(b_last + mprev, jnp.max(g, axis=0, keepdims=True))
            dec = jnp.exp(b_last + mprev - m_new)
            wk = jnp.exp(g - m_new)
            upd = _bdot_tn(wk * kh, jnp.concatenate([vh, jnp.ones_like(vh)], axis=1))
            cn_s[d, p] = jnp.concatenate([dec, dec], axis=1) * cn + jnp.where(bd_mask2, upd, 0.0)
            m_s[d, p] = m_new

    def body(j, carry):
        process(j, 0)
        process(nc - 1 - j, 1)
        return carry

    lax.fori_loop(0, nc, body, 0)
    cn_out_ref[...] = cn_s[...]
    m_out_ref[...] = m_s[...]

    mean_mat = _head_ones(HW) * (1.0 / HD)
    ng = ng_ref[...]
    rb = min(t, 256)

    def epilogue(i, carry):
        r0 = pl.multiple_of(i * rb, rb)
        h = hacc[pl.ds(r0, rb), :]
        hn = h * lax.rsqrt(_fdot(h * h, mean_mat) + EPS) * ng
        o = oz_ref[pl.ds(r0, rb), 0:HW]
        z = oz_ref[pl.ds(r0, rb), HW:2 * HW]
        y_ref[pl.ds(r0, rb), :] = (hn * _sigmoid(o) * _silu(z)).astype(y_ref.dtype)
        return carry

    lax.fori_loop(0, t // rb, epilogue, 0)


def _seq_spec(t, w, single_buffer):
    if single_buffer:
        return pl.BlockSpec((None, t, w), lambda b: (b, 0, 0), pipeline_mode=pl.Buffered(1))
    return pl.BlockSpec((None, t, w), lambda b: (b, 0, 0))


def _full_spec(shape):
    n = len(shape)
    return pl.BlockSpec(shape, lambda b: (0,) * n)


def _mlstm(qkv, oz, ge, go, gate_bias, norm_g, cn0, m0):
    bsz, t, _ = qkv.shape
    big = t > 512
    st = lambda shape: pl.BlockSpec((None,) + shape, lambda b: (b,) + (0,) * len(shape))
    return pl.pallas_call(
        _mlstm_kernel,
        grid=(bsz,),
        in_specs=[_seq_spec(t, 3 * HW, big), _seq_spec(t, 2 * HW, big), _seq_spec(t, LANES, big),
                  _seq_spec(t, LANES, big), _full_spec((2, LANES)), _full_spec((1, HW)),
                  st((2, N_PAIRS, 2 * CHUNK, 2 * LANES)), st((2, N_PAIRS, 1, LANES))],
        out_specs=[pl.BlockSpec((None, t, HW), lambda b: (b, 0, 0)),
                   st((2, N_PAIRS, 2 * CHUNK, 2 * LANES)), st((2, N_PAIRS, 1, LANES))],
        out_shape=[jax.ShapeDtypeStruct((bsz, t, HW), BF16),
                   jax.ShapeDtypeStruct((bsz, 2, N_PAIRS, 2 * CHUNK, 2 * LANES), F32),
                   jax.ShapeDtypeStruct((bsz, 2, N_PAIRS, 1, LANES), F32)],
        scratch_shapes=[pltpu.VMEM((t, HW), F32),
                        pltpu.VMEM((2, N_PAIRS, 2 * CHUNK, 2 * LANES), F32),
                        pltpu.VMEM((2, N_PAIRS, 1, LANES), F32)],
        compiler_params=pltpu.CompilerParams(dimension_semantics=("arbitrary",),
                                             vmem_limit_bytes=VMEM_LIMIT),
        name="mlstm",
    )(qkv, oz, ge, go, gate_bias, norm_g, cn0, m0)


def _rwkv_kernel(rs_ref, rz_ref, mu_ref, w0_ref, a0_ref, w2_ref, a2_ref, kk_ref, ka_ref, rk_ref,
                 ng_ref, s0_ref, y_ref, s_out_ref, yacc, bonus, s_s, *, grid_shift):
    t = rs_ref.shape[0]
    nc = t // CHUNK
    lane = _iota2((CHUNK, LANES), 1)
    row = _iota2((CHUNK, LANES), 0)
    lo = lane < HD
    s_idx = lane % HD
    strict = (s_idx < row, s_idx > row)
    incl = (s_idx <= row, s_idx >= row)
    eye_pk = (s_idx == row).astype(F32)
    r64 = _iota2((CHUNK, CHUNK), 0)
    c64 = _iota2((CHUNK, CHUNK), 1)
    cum = ((c64 <= r64).astype(F32), (c64 >= r64).astype(F32))
    r2 = _iota2((2 * CHUNK, 2 * CHUNK), 0)
    c2 = _iota2((2 * CHUNK, 2 * CHUNK), 1)
    bd_mask = (r2 // CHUNK) == (c2 // CHUNK)
    head_ones = _head_ones(HW)
    roww = _iota2((CHUNK, R_SHIFT), 0)
    mu = mu_ref[...]
    kk = kk_ref[...]
    ka = ka_ref[...]
    rk = rk_ref[...]

    yacc[...] = jnp.zeros_like(yacc)
    s_s[...] = s0_ref[...]

    def shifted(x, c, t0):
        left = jnp.where(roww == 0, 0.0, pltpu.roll(x, 1, 0))
        right = jnp.where(roww == CHUNK - 1, 0.0, pltpu.roll(x, CHUNK - 1, 0))
        has_prev = c > 0
        has_next = c < nc - 1
        if grid_shift:
            up = rs_ref[pl.ds(pl.multiple_of(jnp.maximum(t0 - GRID_W, 0), CHUNK), CHUNK), :]
            down = rs_ref[pl.ds(pl.multiple_of(jnp.minimum(t0 + GRID_W, t - CHUNK), CHUNK), CHUNK), :]
            up = jnp.where(has_prev, up, 0.0)
            down = jnp.where(has_next, down, 0.0)
            return 0.25 * (up + down + left + right)
        prev8 = rs_ref[pl.ds(pl.multiple_of(jnp.maximum(t0 - SUBLANES, 0), SUBLANES), SUBLANES), :]
        next8 = rs_ref[pl.ds(pl.multiple_of(jnp.minimum(t0 + CHUNK, t - SUBLANES), SUBLANES), SUBLANES), :]
        prev_row = jnp.where(has_prev, prev8[SUBLANES - 1:SUBLANES, :], 0.0)
        next_row = jnp.where(has_next, next8[0:1, :], 0.0)
        left = jnp.where(roww == 0, prev_row, left)
        right = jnp.where(roww == CHUNK - 1, next_row, right)
        return 0.5 * (left + right)

    def process(c, d):
        t0 = pl.multiple_of(c * CHUNK, CHUNK)
        x = rs_ref[pl.ds(t0, CHUNK), :]
        blk = x + mu * (shifted(x, c, t0) - x)
        r = blk[:, 0:HW]
        k = blk[:, HW:2 * HW]
        v = blk[:, 2 * HW:3 * HW]
        wl = blk[:, 3 * HW:3 * HW + 2 * LORA]
        al = blk[:, 3 * HW + 2 * LORA:3 * HW + 4 * LORA]
        dsl = slice(d * HW, (d + 1) * HW)
        logw = -RWKV_DECAY_SCALE * _sigmoid(w0_ref[:, dsl] + _bdot(jnp.tanh(wl), w2_ref[:, dsl]))
        a = _sigmoid(a0_ref[:, dsl] + _bdot(al, a2_ref[:, dsl]))
        kappa = k * kk
        knorm = jnp.sqrt(_fdot(kappa * kappa, head_ones))
        khat = kappa / jnp.maximum(knorm, 1e-12)
        kt = k * (1.0 + (a - 1.0) * ka)
        b = khat * a
        if d == 0:
            bonus[pl.ds(t0, CHUNK), :] = _fdot(r * k * rk, head_ones) * v
        last = CHUNK - 1 if d == 0 else 0
        logp = _fdot(cum[d], logw)
        logp_last = logp[last:last + 1, :]
        e_neg = jnp.exp(-logp)
        e_last = jnp.exp(logp_last - logp)
        alpha = jnp.exp(logp - logw) * khat
        beta = b * e_neg
        kap = kt * e_neg
        rho = r * jnp.exp(logp)
        beta_l = b * e_last
        kap_l = kt * e_last
        p_last = jnp.exp(logp_last)
        for p in range(N_PAIRS):
            sl = slice(p * LANES, (p + 1) * LANES)
            al_p, be_p, ka_p, rh_p, v_p = alpha[:, sl], beta[:, sl], kap[:, sl], rho[:, sl], v[:, sl]
            g = _bdot_nt(jnp.concatenate([al_p, rh_p], axis=0),
                         jnp.concatenate([_bd_stack(be_p, lo), _bd_stack(ka_p, lo)], axis=0))
            n_mat = jnp.where(strict[d], g[0:CHUNK, 0:LANES], 0.0)
            mk = jnp.where(strict[d], g[0:CHUNK, LANES:], 0.0)
            mrb = jnp.where(incl[d], g[CHUNK:, 0:LANES], 0.0)
            mrk = jnp.where(incl[d], g[CHUNK:, LANES:], 0.0)
            r_inv = eye_pk - n_mat
            pw = _bdot(n_mat, _bd_stack(n_mat, lo))
            for _ in range(4):
                rp = _bdot(jnp.concatenate([r_inv, pw], axis=0), _bd_stack(pw, lo))
                r_inv = r_inv + rp[0:CHUNK]
                pw = rp[CHUNK:]
            r_inv = r_inv + _bdot(r_inv, _bd_stack(pw, lo))
            mkv = _bdot(mk, _bd_stack(v_p, lo))
            wu = _bdot(r_inv, jnp.concatenate([_bd_stack(al_p, lo), _bd_stack(mkv, lo)], axis=1))
            s_prev = s_s[d, p]
            ws = _bdot_nt(jnp.concatenate([wu[:, 0:LANES], rh_p], axis=0), s_prev)
            u = ws[0:CHUNK] + wu[:, LANES:]
            y = ws[CHUNK:] + _bdot(jnp.concatenate([mrk, -mrb], axis=1),
                                   jnp.concatenate([_bd_stack(v_p, lo), _bd_stack(u, lo)], axis=0))
            yacc[pl.ds(t0, CHUNK), sl] = yacc[pl.ds(t0, CHUNK), sl] + y
            upd = _bdot_tn(jnp.concatenate([v_p, -u], axis=0),
                           jnp.concatenate([kap_l[:, sl], beta_l[:, sl]], axis=0))
            s_s[d, p] = s_prev * p_last[:, sl] + jnp.where(bd_mask, upd, 0.0)

    def body(j, carry):
        process(j, 0)
        process(nc - 1 - j, 1)
        return carry

    lax.fori_loop(0, nc, body, 0)
    s_out_ref[...] = s_s[...]

    mean_mat = head_ones * (1.0 / HD)
    ng = ng_ref[...]
    rb = min(t, 256)

    def epilogue(i, carry):
        r0 = pl.multiple_of(i * rb, rb)
        y = yacc[pl.ds(r0, rb), :]
        yn = y * lax.rsqrt(_fdot(y * y, mean_mat) + EPS) * ng + bonus[pl.ds(r0, rb), :]
        y_ref[pl.ds(r0, rb), :] = (yn * _silu(rz_ref[pl.ds(r0, rb), :])).astype(y_ref.dtype)
        return carry

    lax.fori_loop(0, t // rb, epilogue, 0)


def _rwkv(rs, rz, mu, w0, a0, w2bd, a2bd, kk, ka, rk, norm_g, s0, grid_shift):
    bsz, t, _ = rs.shape
    big = t > 512
    st = lambda shape: pl.BlockSpec((None,) + shape, lambda b: (b,) + (0,) * len(shape))
    return pl.pallas_call(
        functools.partial(_rwkv_kernel, grid_shift=grid_shift),
        grid=(bsz,),
        in_specs=[_seq_spec(t, R_SHIFT, big), _seq_spec(t, HW, big),
                  _full_spec((1, R_SHIFT)), _full_spec((1, 2 * HW)), _full_spec((1, 2 * HW)),
                  _full_spec((2 * LORA, 2 * HW)), _full_spec((2 * LORA, 2 * HW)),
                  _full_spec((1, HW)), _full_spec((1, HW)), _full_spec((1, HW)), _full_spec((1, HW)),
                  st((2, N_PAIRS, 2 * CHUNK, LANES))],
        out_specs=[pl.BlockSpec((None, t, HW), lambda b: (b, 0, 0)),
                   st((2, N_PAIRS, 2 * CHUNK, LANES))],
        out_shape=[jax.ShapeDtypeStruct((bsz, t, HW), BF16),
                   jax.ShapeDtypeStruct((bsz, 2, N_PAIRS, 2 * CHUNK, LANES), F32)],
        scratch_shapes=[pltpu.VMEM((t, HW), F32), pltpu.VMEM((t, HW), F32),
                        pltpu.VMEM((2, N_PAIRS, 2 * CHUNK, LANES), F32)],
        compiler_params=pltpu.CompilerParams(dimension_semantics=("arbitrary",),
                                             vmem_limit_bytes=VMEM_LIMIT),
        name="rwkv",
    )(rs, rz, mu, w0, a0, w2bd, a2bd, kk, ka, rk, norm_g, s0)


def _lru_kernel(xz_ref, conv_ref, cb_ref, w_ref, b_ref, lam_ref, h0_ref, y_ref, hfin_ref, acc, hc):
    t = xz_ref.shape[0]
    nb = t // LRU_BLOCK
    row = _iota2((LRU_BLOCK, L_W), 0)
    conv = conv_ref[...]
    cb = cb_ref[...]
    sp = _softplus(-lam_ref[...])

    acc[...] = jnp.zeros_like(acc)
    hc[...] = h0_ref[...]

    def process(j, d):
        t0 = pl.multiple_of(j * LRU_BLOCK, LRU_BLOCK)
        x = xz_ref[pl.ds(t0, LRU_BLOCK), 0:L_W]
        prev8 = xz_ref[pl.ds(pl.multiple_of(jnp.maximum(t0 - SUBLANES, 0), SUBLANES), SUBLANES), 0:L_W]
        next8 = xz_ref[pl.ds(pl.multiple_of(jnp.minimum(t0 + LRU_BLOCK, t - SUBLANES), SUBLANES), SUBLANES), 0:L_W]
        prev8 = jnp.where(j > 0, prev8, 0.0)
        next8 = jnp.where(j < nb - 1, next8, 0.0)
        xm1 = jnp.where(row == 0, prev8[7:8, :], pltpu.roll(x, 1, 0))
        xm2 = jnp.where(row == 0, prev8[6:7, :], jnp.where(row == 1, prev8[7:8, :], pltpu.roll(x, 2, 0)))
        xp1 = jnp.where(row == LRU_BLOCK - 1, next8[0:1, :], pltpu.roll(x, LRU_BLOCK - 1, 0))
        xc = conv[0:1, :] * xm2 + conv[1:2, :] * xm1 + conv[2:3, :] * x + conv[3:4, :] * xp1 + cb
        pre = _bdot(xc, w_ref[:, 2 * d * L_W:2 * (d + 1) * L_W]) + b_ref[:, 2 * d * L_W:2 * (d + 1) * L_W]
        rg = _sigmoid(pre[:, 0:L_W])
        ig = _sigmoid(pre[:, L_W:])
        log_a = -LRU_C * rg * sp[:, d * L_W:(d + 1) * L_W]
        a_cum = jnp.exp(log_a)
        b_cum = jnp.sqrt(jnp.tanh(-log_a) * (1.0 + a_cum * a_cum)) * ig * xc
        k = 1
        while k < LRU_BLOCK:
            if d == 0:
                keep = row >= k
                a_sh = jnp.where(keep, pltpu.roll(a_cum, k, 0), 1.0)
                b_sh = jnp.where(keep, pltpu.roll(b_cum, k, 0), 0.0)
            else:
                keep = row < LRU_BLOCK - k
                a_sh = jnp.where(keep, pltpu.roll(a_cum, LRU_BLOCK - k, 0), 1.0)
                b_sh = jnp.where(keep, pltpu.roll(b_cum, LRU_BLOCK - k, 0), 0.0)
            b_cum = a_cum * b_sh + b_cum
            a_cum = a_cum * a_sh
            k *= 2
        h = b_cum + a_cum * hc[d]
        last = LRU_BLOCK - 1 if d == 0 else 0
        hc[d] = h[last:last + 1, :]
        acc[pl.ds(t0, LRU_BLOCK), :] = acc[pl.ds(t0, LRU_BLOCK), :] + h

    def body(j, carry):
        process(j, 0)
        process(nb - 1 - j, 1)
        return carry

    lax.fori_loop(0, nb, body, 0)
    hfin_ref[...] = hc[...]

    def epilogue(i, carry):
        r0 = pl.multiple_of(i * LRU_BLOCK, LRU_BLOCK)
        z = xz_ref[pl.ds(r0, LRU_BLOCK), L_W:2 * L_W]
        y_ref[pl.ds(r0, LRU_BLOCK), :] = (acc[pl.ds(r0, LRU_BLOCK), :] * _silu(z)).astype(y_ref.dtype)
        return carry

    lax.fori_loop(0, nb, epilogue, 0)


def _lru(xz, conv, conv_b, wbd, bias, lam, h0):
    bsz, t, _ = xz.shape
    st = lambda shape: pl.BlockSpec((None,) + shape, lambda b: (b,) + (0,) * len(shape))
    return pl.pallas_call(
        _lru_kernel,
        grid=(bsz,),
        in_specs=[_seq_spec(t, 2 * L_W, False), _full_spec((CONV_W, L_W)), _full_spec((1, L_W)),
                  _full_spec((L_W, 4 * L_W)), _full_spec((1, 4 * L_W)), _full_spec((1, 2 * L_W)),
                  st((2, 1, L_W))],
        out_specs=[pl.BlockSpec((None, t, L_W), lambda b: (b, 0, 0)), st((2, 1, L_W))],
        out_shape=[jax.ShapeDtypeStruct((bsz, t, L_W), BF16),
                   jax.ShapeDtypeStruct((bsz, 2, 1, L_W), F32)],
        scratch_shapes=[pltpu.VMEM((t, L_W), F32), pltpu.VMEM((2, 1, L_W), F32)],
        compiler_params=pltpu.CompilerParams(dimension_semantics=("arbitrary",),
                                             vmem_limit_bytes=VMEM_LIMIT),
        name="lru",
    )(xz, conv, conv_b, wbd, bias, lam, h0)


def _permute_w_in(w_in_l):
    valid = jnp.asarray(_GATE_SRC >= 0)
    gates = w_in_l[:, 5 * HW:M_COLS]
    ge = jnp.where(valid, gates[:, np.maximum(_GATE_SRC, 0)], 0.0)
    go = jnp.where(valid, gates[:, np.maximum(_GATE_SRC, 0) + 1], 0.0)
    return jnp.concatenate([w_in_l[:, 0:5 * HW], ge, go, w_in_l[:, M_COLS:]], axis=1).astype(BF16)


def _gate_source_index():
    src = np.full((LANES,), -1, np.int32)
    for dd in range(2):
        for is_f in range(2):
            for p in range(N_PAIRS):
                src[_gate_col(dd, is_f, p)] = is_f * 2 * N_HEADS + dd * N_HEADS + 2 * p
    return src


_GATE_SRC = _gate_source_index()


def _gate_bias(m_bi_l, m_bf_l):
    flat = jnp.concatenate([m_bi_l.reshape(-1), m_bf_l.reshape(-1)])
    idx = np.maximum(_GATE_SRC, 0)
    valid = jnp.asarray(_GATE_SRC >= 0)
    return jnp.stack([jnp.where(valid, flat[idx], 0.0), jnp.where(valid, flat[idx + 1], 0.0)])


def _block_diag(blocks):
    n, a, b = blocks.shape
    eye = jnp.eye(n, dtype=blocks.dtype)
    return (eye[:, None, :, None] * blocks[:, :, None, :]).reshape(n * a, n * b)


def _pair_block_diag(x):
    lead = x.shape[:-3]
    a, b = x.shape[-2:]
    xp = x.reshape(lead + (N_PAIRS, 2, a, b))
    eye = jnp.eye(2, dtype=x.dtype)
    out = xp[..., :, :, :, None, :] * eye[:, None, :, None]
    return out.reshape(lead + (N_PAIRS, 2 * a, 2 * b))


def _pair_diag_blocks(x, a, b):
    lead = x.shape[:-3]
    xp = x.reshape(lead + (N_PAIRS, 2, a, 2, b))
    blocks = jnp.stack([xp[..., 0, :, 0, :], xp[..., 1, :, 1, :]], axis=-3)
    return blocks.reshape(lead + (N_HEADS, a, b))


def _pack_mlstm_state(c_st, n_st, m_st):
    cbd = _pair_block_diag(c_st)
    nbd = _pair_block_diag(jnp.broadcast_to(n_st[..., None], n_st.shape + (HD,)))
    mrow = jnp.repeat(m_st.reshape(m_st.shape[:2] + (N_PAIRS, 1, 2)), HD, axis=-1)
    return jnp.concatenate([cbd, nbd], axis=-1), mrow


def _unpack_mlstm_state(cn, mrow):
    c_st = _pair_diag_blocks(cn[..., 0:LANES], HD, HD)
    n_st = _pair_diag_blocks(cn[..., LANES:], HD, HD)[..., 0]
    m_st = mrow[..., 0, ::HD].reshape(mrow.shape[:2] + (N_HEADS,))
    return c_st, n_st, m_st


def _layer_params(l, g_pre, g_post, w_in, w_out, m_bi, m_bf, m_norm, r_mu, r_w0, r_w2, r_a0, r_a2,
                  r_kk, r_ka, r_rk, r_norm, l_conv, l_conv_b, l_wa, l_ba, l_wx, l_bx, l_lambda):
    row = lambda v: v.reshape(1, -1)
    return dict(
        g_pre=row(g_pre[l]), g_post=row(g_post[l]),
        w_in=_permute_w_in(w_in[l]), w_out=w_out[l].astype(BF16),
        gate_bias=_gate_bias(m_bi[l], m_bf[l]), m_norm=row(m_norm[l]),
        r_mu=row(r_mu[l]), r_w0=row(r_w0[l]), r_a0=row(r_a0[l]),
        r_w2=_block_diag(r_w2[l]).astype(BF16), r_a2=_block_diag(r_a2[l]).astype(BF16),
        r_kk=row(r_kk[l]), r_ka=row(r_ka[l]), r_rk=row(r_rk[l]), r_norm=row(r_norm[l]),
        l_conv=l_conv[l], l_conv_b=row(l_conv_b[l]),
        l_w=jnp.concatenate([_block_diag(l_wa[l][0]), _block_diag(l_wx[l][0]),
                             _block_diag(l_wa[l][1]), _block_diag(l_wx[l][1])], axis=1).astype(BF16),
        l_b=jnp.concatenate([l_ba[l][0], l_bx[l][0], l_ba[l][1], l_bx[l][1]]).reshape(1, -1),
        l_lambda=row(l_lambda[l]),
    )


def _trunk_layer(x, mod, lp, states, per_seq_mod, grid_shift):
    m_c, m_n, m_m, r_s, l_h = states
    qkv, oz, ge, go, rs, rz, lxz = _inproj(x, mod, lp['g_pre'], lp['w_in'], per_seq_mod)
    cn0, m0 = _pack_mlstm_state(m_c, m_n, m_m)
    y_m, cn1, m1 = _mlstm(qkv, oz, ge, go, lp['gate_bias'], lp['m_norm'], cn0, m0)
    y_r, s1 = _rwkv(rs, rz, lp['r_mu'], lp['r_w0'], lp['r_a0'], lp['r_w2'], lp['r_a2'], lp['r_kk'],
                    lp['r_ka'], lp['r_rk'], lp['r_norm'], _pair_block_diag(r_s), grid_shift)
    y_l, h1 = _lru(lxz, lp['l_conv'], lp['l_conv_b'], lp['l_w'], lp['l_b'], lp['l_lambda'],
                   l_h[:, :, None, :])
    y = _outproj(y_m, y_r, y_l, x, mod, lp['g_post'], lp['w_out'], per_seq_mod)
    new_states = _unpack_mlstm_state(cn1, m1) + (_pair_diag_blocks(s1, HD, HD), h1[:, :, 0, :])
    return y, new_states


def kernel(x_prompt, x_sample, c, state_mlstm_C, state_mlstm_n, state_mlstm_m, state_rwkv, state_rglru, c_ctx, g_pre, g_post, w_mod, b_mod, w_in, w_out, m_bi, m_bf, m_norm, r_mu, r_w0, r_w2, r_a0, r_a2, r_kk, r_ka, r_rk, r_norm, l_conv, l_conv_b, l_wa, l_ba, l_wx, l_bx, l_lambda):
    depth = w_in.shape[0]
    bp = x_prompt.shape[0]
    bs = x_sample.shape[0]
    d = x_prompt.shape[-1]
    rows = -(-(1 + bs) // SUBLANES) * SUBLANES
    cc = jnp.zeros((rows, d), F32).at[0].set(c_ctx).at[1:1 + bs].set(c)
    mod = _modulation(cc, w_mod, b_mod)
    zero_states = (jnp.zeros((bp, 2, N_HEADS, HD, HD), F32), jnp.zeros((bp, 2, N_HEADS, HD), F32),
                   jnp.zeros((bp, 2, N_HEADS), F32), jnp.zeros((bp, 2, N_HEADS, HD, HD), F32),
                   jnp.zeros((bp, 2, L_W), F32))
    xp, xs = x_prompt, x_sample
    new_states = []
    for l in range(depth):
        lp = _layer_params(l, g_pre, g_post, w_in, w_out, m_bi, m_bf, m_norm, r_mu, r_w0, r_w2, r_a0,
                           r_a2, r_kk, r_ka, r_rk, r_norm, l_conv, l_conv_b, l_wa, l_ba, l_wx, l_bx,
                           l_lambda)
        xp, st = _trunk_layer(xp, mod[l, 0:1][:, None, :], lp, zero_states, False, False)
        new_states.append(st)
        cache = (state_mlstm_C[:, l].astype(F32), state_mlstm_n[:, l].astype(F32),
                 state_mlstm_m[:, l].astype(F32), state_rwkv[:, l].astype(F32),
                 state_rglru[:, l].astype(F32))
        xs, _ = _trunk_layer(xs, mod[l, 1:1 + bs][:, None, :], lp, cache, True, True)
    stacked = tuple(jnp.stack([st[i] for st in new_states], axis=1) for i in range(5))
    return (xp, xs) + stacked
```

```python
import functools

import numpy as np
import jax
import jax.numpy as jnp
from jax import lax
from jax.experimental import pallas as pl
from jax.experimental.pallas import tpu as pltpu

F32 = jnp.float32
BF16 = jnp.bfloat16
HIGHEST = lax.Precision.HIGHEST

D_MODEL = 1024
EPS = 1e-6
HD = 64
N_HEADS = 6
N_PAIRS = N_HEADS // 2
HW = N_HEADS * HD
CHUNK = 64
GRID_W = 64
LORA = 64
R_SHIFT = 3 * HW + 4 * LORA
L_W = 256
L_BLOCKS = 4
CONV_W = 4
LRU_C = 8.0
RWKV_DECAY_SCALE = 0.6065306597126334
M_COLS = 5 * HW + 4 * N_HEADS
IN_COLS = M_COLS + R_SHIFT + HW + 2 * L_W

LANES = 128
SUBLANES = 8
LRU_BLOCK = 128
VMEM_LIMIT = 56 * 1024 * 1024

SEG_WIDTHS = (3 * HW, 2 * HW, LANES, LANES, R_SHIFT, HW, 2 * L_W)
SEG_OFFS = tuple(int(v) for v in np.cumsum((0,) + SEG_WIDTHS))
IN_COLS_PAD = SEG_OFFS[-1]
N_TILE = 512


def _gate_col(d, is_f, p):
    return 8 * d + 4 * is_f + p


def _bdot(a, b):
    return jnp.dot(a.astype(BF16), b.astype(BF16), preferred_element_type=F32)


def _bdot_nt(a, b):
    return lax.dot_general(a.astype(BF16), b.astype(BF16), (((1,), (1,)), ((), ())),
                           preferred_element_type=F32)


def _bdot_tn(a, b):
    return lax.dot_general(a.astype(BF16), b.astype(BF16), (((0,), (0,)), ((), ())),
                           preferred_element_type=F32)


def _fdot(a, b):
    return jnp.dot(a, b, precision=HIGHEST, preferred_element_type=F32)


def _split_bf16(x, terms):
    out = []
    for _ in range(terms - 1):
        hi = x.astype(BF16)
        out.append(hi)
        x = x - hi.astype(F32)
    out.append(x.astype(BF16))
    return out


def _cum_dot(cum_rep, x, terms=3):
    return jnp.dot(cum_rep, jnp.concatenate(_split_bf16(x, terms), axis=0), preferred_element_type=F32)


def _head_sum(x, ones_rep, terms=2):
    outs = []
    for p in range(x.shape[1] // LANES):
        parts = _split_bf16(x[:, p * LANES:(p + 1) * LANES], terms)
        outs.append(jnp.dot(jnp.concatenate(parts, axis=1), ones_rep, preferred_element_type=F32))
    return outs[0] if len(outs) == 1 else jnp.concatenate(outs, axis=1)


def _sigmoid(x):
    return 1.0 / (1.0 + jnp.exp(-x))


def _silu(x):
    return x * _sigmoid(x)


def _softplus(x):
    return jnp.maximum(x, 0.0) + jnp.log1p(jnp.exp(-jnp.abs(x)))


def _log_sigmoid(x):
    return -_softplus(-x)


def _iota2(shape, dim):
    return lax.broadcasted_iota(jnp.int32, shape, dim)


def _head_ones(n):
    return (_iota2((n, n), 0) // HD == _iota2((n, n), 1) // HD).astype(F32)


def _bd_stack(x, lo):
    return jnp.concatenate([jnp.where(lo, x, 0.0), jnp.where(lo, 0.0, x)], axis=0)


def _pair_col(q, j, lo):
    return jnp.where(lo, q[0:CHUNK, j:j + 1], q[CHUNK:2 * CHUNK, j:j + 1])


def _mod_kernel(c_ref, w_ref, b_ref, o_ref):
    o_ref[...] = _fdot(_silu(c_ref[...]), w_ref[...]) + b_ref[...]


def _modulation(cc, w_mod, b_mod):
    depth, d, n = w_mod.shape
    rows = cc.shape[0]
    return pl.pallas_call(
        _mod_kernel,
        grid=(depth, n // N_TILE),
        in_specs=[pl.BlockSpec((rows, d), lambda l, j: (0, 0)),
                  pl.BlockSpec((None, d, N_TILE), lambda l, j: (l, 0, j)),
                  pl.BlockSpec((None, 1, N_TILE), lambda l, j: (l, 0, j))],
        out_specs=pl.BlockSpec((None, rows, N_TILE), lambda l, j: (l, 0, j)),
        out_shape=jax.ShapeDtypeStruct((depth, rows, n), F32),
        compiler_params=pltpu.CompilerParams(dimension_semantics=("arbitrary", "arbitrary")),
        name="modulation",
    )(cc, w_mod, b_mod.reshape(depth, 1, n))


def _inproj_kernel(x_ref, mod_ref, g_ref, w_ref, *out_refs):
    x = x_ref[...]
    d = x.shape[-1]
    mod = mod_ref[...]
    h = x * lax.rsqrt(jnp.mean(x * x, axis=-1, keepdims=True) + EPS) * g_ref[...]
    h = (h * (1.0 + mod[:, d:2 * d]) + mod[:, 0:d]).astype(BF16)
    for o_ref, a, b in zip(out_refs, SEG_OFFS[:-1], SEG_OFFS[1:]):
        for n0 in range(a, b, N_TILE):
            n1 = min(n0 + N_TILE, b)
            o_ref[:, n0 - a:n1 - a] = jnp.dot(h, w_ref[:, n0:n1], preferred_element_type=F32)


def _inproj(x, mod, g_pre, w_in_p, per_seq_mod):
    bsz, t, d = x.shape
    tm = min(t, 512)
    mod_idx = (lambda b, i: (b, 0, 0)) if per_seq_mod else (lambda b, i: (0, 0, 0))
    return pl.pallas_call(
        _inproj_kernel,
        grid=(bsz, t // tm),
        in_specs=[pl.BlockSpec((None, tm, d), lambda b, i: (b, i, 0)),
                  pl.BlockSpec((None, 1, 3 * d), mod_idx),
                  pl.BlockSpec((1, d), lambda b, i: (0, 0)),
                  pl.BlockSpec((d, IN_COLS_PAD), lambda b, i: (0, 0), pipeline_mode=pl.Buffered(1))],
        out_specs=[pl.BlockSpec((None, tm, w), lambda b, i: (b, i, 0)) for w in SEG_WIDTHS],
        out_shape=[jax.ShapeDtypeStruct((bsz, t, w), F32) for w in SEG_WIDTHS],
        compiler_params=pltpu.CompilerParams(dimension_semantics=("arbitrary", "arbitrary"),
                                             vmem_limit_bytes=VMEM_LIMIT),
        name="inproj",
    )(x, mod, g_pre, w_in_p)


def _outproj_kernel(ym_ref, yr_ref, yl_ref, x_ref, mod_ref, g_ref, w_ref, o_ref):
    d = x_ref.shape[-1]
    o = jnp.dot(ym_ref[...], w_ref[0:HW, :], preferred_element_type=F32)
    o = o + jnp.dot(yr_ref[...], w_ref[HW:2 * HW, :], preferred_element_type=F32)
    o = o + jnp.dot(yl_ref[...], w_ref[2 * HW:2 * HW + L_W, :], preferred_element_type=F32)
    on = o * lax.rsqrt(jnp.mean(o * o, axis=-1, keepdims=True) + EPS) * g_ref[...]
    o_ref[...] = x_ref[...] + mod_ref[:, 2 * d:3 * d] * on


def _outproj(ym, yr, yl, x, mod, g_post, w_out_b, per_seq_mod):
    bsz, t, d = x.shape
    tm = min(t, 512)
    mod_idx = (lambda b, i: (b, 0, 0)) if per_seq_mod else (lambda b, i: (0, 0, 0))
    tok = lambda w: pl.BlockSpec((None, tm, w), lambda b, i: (b, i, 0))
    return pl.pallas_call(
        _outproj_kernel,
        grid=(bsz, t // tm),
        in_specs=[tok(HW), tok(HW), tok(L_W), tok(d),
                  pl.BlockSpec((None, 1, 3 * d), mod_idx),
                  pl.BlockSpec((1, d), lambda b, i: (0, 0)),
                  pl.BlockSpec((2 * HW + L_W, d), lambda b, i: (0, 0))],
        out_specs=tok(d),
        out_shape=jax.ShapeDtypeStruct((bsz, t, d), F32),
        compiler_params=pltpu.CompilerParams(dimension_semantics=("arbitrary", "arbitrary"),
                                             vmem_limit_bytes=VMEM_LIMIT),
        name="outproj",
    )(ym, yr, yl, x, mod, g_post, w_out_b)


def _mlstm_kernel(qkv_ref, oz_ref, ge_ref, go_ref, gb_ref, ng_ref, cn0_ref, m0_ref,
                  y_ref, cn_out_ref, m_out_ref, hacc, cn_s, m_s):
    t = qkv_ref.shape[0]
    nc = t // CHUNK
    lane = _iota2((CHUNK, LANES), 1)
    row = _iota2((CHUNK, LANES), 0)
    lo = lane < HD
    s_idx = lane % HD
    causal = (s_idx <= row, s_idx >= row)
    r2 = _iota2((2 * CHUNK, 2 * CHUNK), 0)
    c2 = _iota2((2 * CHUNK, 2 * CHUNK), 1)
    same = (r2 // CHUNK) == (c2 // CHUNK)
    cum = ((same & (c2 <= r2)).astype(BF16), (same & (c2 >= r2)).astype(BF16))
    cum_rep = tuple(jnp.concatenate([m, m, m], axis=1) for m in cum)
    bd_ones = same.astype(F32)
    ones_rep = jnp.concatenate([same.astype(BF16), same.astype(BF16)], axis=0)
    bd_mask2 = jnp.concatenate([same, same], axis=1)
    colid = _iota2((1, LANES), 1)
    is_f = (colid % 8) >= 4
    gb = gb_ref[...]

    hacc[...] = jnp.zeros_like(hacc)
    cn_s[...] = cn0_ref[...]
    m_s[...] = m0_ref[...]

    def gate_table(t0, d):
        ve = ge_ref[pl.ds(t0, CHUNK), :] + gb[0:1, :]
        vo = go_ref[pl.ds(t0, CHUNK), :] + gb[1:2, :]
        ve = jnp.where(is_f, _log_sigmoid(ve), ve)
        vo = jnp.where(is_f, _log_sigmoid(vo), vo)
        x = jnp.concatenate([ve, vo], axis=0)
        q = jnp.where(is_f, _cum_dot(cum_rep[d], x), x)
        return q, q.T

    chains = [(d, p) for d in range(2) for p in range(N_PAIRS)]

    def body(j, carry):
        t0s = (pl.multiple_of(j * CHUNK, CHUNK), pl.multiple_of((nc - 1 - j) * CHUNK, CHUNK))
        tabs = [gate_table(t0s[d], d) for d in range(2)]
        st = []
        for d, p in chains:
            t0 = t0s[d]
            q_tab, q_tab_t = tabs[d]
            ji, jf = _gate_col(d, 0, p), _gate_col(d, 1, p)
            c = dict(t0=t0, sl=slice(p * LANES, (p + 1) * LANES))
            c['bcol'] = _pair_col(q_tab, jf, lo)
            c['icol'] = _pair_col(q_tab, ji, lo)
            brow = q_tab_t[jf:jf + 1, :]
            irow = q_tab_t[ji:ji + 1, :]
            c['q'] = qkv_ref[pl.ds(t0, CHUNK), p * LANES:(p + 1) * LANES]
            c['k'] = qkv_ref[pl.ds(t0, CHUNK), HW + p * LANES:HW + (p + 1) * LANES] * (HD ** -0.5)
            c['v'] = qkv_ref[pl.ds(t0, CHUNK), 2 * HW + p * LANES:2 * HW + (p + 1) * LANES]
            c['mprev'] = m_s[d, p]
            c['cn'] = cn_s[d, p]
            dmat = jnp.where(causal[d], c['bcol'] - brow + irow, -jnp.inf)
            inter = c['bcol'] + c['mprev']
            mx = jnp.where(lo,
                           jnp.max(jnp.where(lo, dmat, -jnp.inf), axis=1, keepdims=True),
                           jnp.max(jnp.where(lo, -jnp.inf, dmat), axis=1, keepdims=True))
            c['m_t'] = jnp.maximum(inter, mx)
            c['pexp'] = jnp.exp(dmat - c['m_t'])
            c['sc'] = jnp.exp(inter - c['m_t'])
            last = CHUNK - 1 if d == 0 else 0
            b_last = c['bcol'][last:last + 1, :]
            g = b_last - c['bcol'] + c['icol']
            c['m_new'] = jnp.maximum(b_last + c['mprev'], jnp.max(g, axis=0, keepdims=True))
            c['dec'] = jnp.exp(b_last + c['mprev'] - c['m_new'])
            c['wk'] = jnp.exp(g - c['m_new'])
            st.append(c)
        for c in st:
            c['s'] = _bdot_nt(c['q'], _bd_stack(c['k'], lo)) * c['pexp']
        for c in st:
            c['a1'] = _bdot(c['q'], c['cn'])
        for c in st:
            c['upd'] = _bdot_tn(c['wk'] * c['k'], jnp.concatenate([c['v'], jnp.ones_like(c['v'])], axis=1))
        for c in st:
            c['a2'] = _bdot(c['s'], jnp.concatenate([_bd_stack(c['v'], lo), bd_ones], axis=1))
        for (d, p), c in zip(chains, st):
            num = c['sc'] * c['a1'][:, 0:LANES] + c['a2'][:, 0:LANES]
            den = c['sc'] * c['a1'][:, LANES:] + c['a2'][:, LANES:]
            h = num / jnp.maximum(jnp.abs(den), jnp.exp(-c['m_t']))
            hacc[pl.ds(c['t0'], CHUNK), c['sl']] = hacc[pl.ds(c['t0'], CHUNK), c['sl']] + h
            cn_s[d, p] = (jnp.concatenate([c['dec'], c['dec']], axis=1) * c['cn']
                          + jnp.where(bd_mask2, c['upd'], 0.0))
            m_s[d, p] = c['m_new']
        return carry

    lax.fori_loop(0, nc, body, 0)
    cn_out_ref[...] = cn_s[...]
    m_out_ref[...] = m_s[...]

    ng = ng_ref[...]
    rb = min(t, 256)

    def epilogue(i, carry):
        r0 = pl.multiple_of(i * rb, rb)
        h = hacc[pl.ds(r0, rb), :]
        hn = h * lax.rsqrt(_head_sum(h * h, ones_rep) * (1.0 / HD) + EPS) * ng
        o = oz_ref[pl.ds(r0, rb), 0:HW]
        z = oz_ref[pl.ds(r0, rb), HW:2 * HW]
        y_ref[pl.ds(r0, rb), :] = (hn * _sigmoid(o) * _silu(z)).astype(y_ref.dtype)
        return carry

    lax.fori_loop(0, t // rb, epilogue, 0)


def _seq_spec(t, w, single_buffer):
    if single_buffer:
        return pl.BlockSpec((None, t, w), lambda b: (b, 0, 0), pipeline_mode=pl.Buffered(1))
    return pl.BlockSpec((None, t, w), lambda b: (b, 0, 0))


def _full_spec(shape):
    n = len(shape)
    return pl.BlockSpec(shape, lambda b: (0,) * n)


def _mlstm(qkv, oz, ge, go, gate_bias, norm_g, cn0, m0):
    bsz, t, _ = qkv.shape
    big = t > 512
    st = lambda shape: pl.BlockSpec((None,) + shape, lambda b: (b,) + (0,) * len(shape))
    return pl.pallas_call(
        _mlstm_kernel,
        grid=(bsz,),
        in_specs=[_seq_spec(t, 3 * HW, big), _seq_spec(t, 2 * HW, big), _seq_spec(t, LANES, big),
                  _seq_spec(t, LANES, big), _full_spec((2, LANES)), _full_spec((1, HW)),
                  st((2, N_PAIRS, 2 * CHUNK, 2 * LANES)), st((2, N_PAIRS, 1, LANES))],
        out_specs=[pl.BlockSpec((None, t, HW), lambda b: (b, 0, 0)),
                   st((2, N_PAIRS, 2 * CHUNK, 2 * LANES)), st((2, N_PAIRS, 1, LANES))],
        out_shape=[jax.ShapeDtypeStruct((bsz, t, HW), BF16),
                   jax.ShapeDtypeStruct((bsz, 2, N_PAIRS, 2 * CHUNK, 2 * LANES), F32),
                   jax.ShapeDtypeStruct((bsz, 2, N_PAIRS, 1, LANES), F32)],
        scratch_shapes=[pltpu.VMEM((t, HW), F32),
                        pltpu.VMEM((2, N_PAIRS, 2 * CHUNK, 2 * LANES), F32),
                        pltpu.VMEM((2, N_PAIRS, 1, LANES), F32)],
        compiler_params=pltpu.CompilerParams(dimension_semantics=("arbitrary",),
                                             vmem_limit_bytes=VMEM_LIMIT),
        name="mlstm",
    )(qkv, oz, ge, go, gate_bias, norm_g, cn0, m0)


def _rwkv_kernel(rs_ref, rz_ref, mu_ref, w0_ref, a0_ref, w2_ref, a2_ref, kk_ref, ka_ref, rk_ref,
                 ng_ref, s0_ref, y_ref, s_out_ref, yacc, bonus, s_s, *, grid_shift):
    t = rs_ref.shape[0]
    nc = t // CHUNK
    lane = _iota2((CHUNK, LANES), 1)
    row = _iota2((CHUNK, LANES), 0)
    lo = lane < HD
    s_idx = lane % HD
    strict = (s_idx < row, s_idx > row)
    incl = (s_idx <= row, s_idx >= row)
    eye_pk = (s_idx == row).astype(F32)
    r64 = _iota2((CHUNK, CHUNK), 0)
    c64 = _iota2((CHUNK, CHUNK), 1)
    cum = ((c64 <= r64).astype(BF16), (c64 >= r64).astype(BF16))
    cum_rep = tuple(jnp.concatenate([m, m, m], axis=1) for m in cum)
    r2 = _iota2((2 * CHUNK, 2 * CHUNK), 0)
    c2 = _iota2((2 * CHUNK, 2 * CHUNK), 1)
    bd_mask = (r2 // CHUNK) == (c2 // CHUNK)
    ones_rep = jnp.concatenate([bd_mask.astype(BF16), bd_mask.astype(BF16)], axis=0)
    roww = _iota2((CHUNK, R_SHIFT), 0)
    mu = mu_ref[...]
    kk = kk_ref[...]
    ka = ka_ref[...]
    rk = rk_ref[...]

    yacc[...] = jnp.zeros_like(yacc)
    s_s[...] = s0_ref[...]

    def shifted(x, c, t0):
        left = jnp.where(roww == 0, 0.0, pltpu.roll(x, 1, 0))
        right = jnp.where(roww == CHUNK - 1, 0.0, pltpu.roll(x, CHUNK - 1, 0))
        has_prev = c > 0
        has_next = c < nc - 1
        if grid_shift:
            up = rs_ref[pl.ds(pl.multiple_of(jnp.maximum(t0 - GRID_W, 0), CHUNK), CHUNK), :]
            down = rs_ref[pl.ds(pl.multiple_of(jnp.minimum(t0 + GRID_W, t - CHUNK), CHUNK), CHUNK), :]
            up = jnp.where(has_prev, up, 0.0)
            down = jnp.where(has_next, down, 0.0)
            return 0.25 * (up + down + left + right)
        prev8 = rs_ref[pl.ds(pl.multiple_of(jnp.maximum(t0 - SUBLANES, 0), SUBLANES), SUBLANES), :]
        next8 = rs_ref[pl.ds(pl.multiple_of(jnp.minimum(t0 + CHUNK, t - SUBLANES), SUBLANES), SUBLANES), :]
        prev_row = jnp.where(has_prev, prev8[SUBLANES - 1:SUBLANES, :], 0.0)
        next_row = jnp.where(has_next, next8[0:1, :], 0.0)
        left = jnp.where(roww == 0, prev_row, left)
        right = jnp.where(roww == CHUNK - 1, next_row, right)
        return 0.5 * (left + right)

    def prep(c, d):
        t0 = pl.multiple_of(c * CHUNK, CHUNK)
        x = rs_ref[pl.ds(t0, CHUNK), :]
        blk = x + mu * (shifted(x, c, t0) - x)
        r = blk[:, 0:HW]
        k = blk[:, HW:2 * HW]
        v = blk[:, 2 * HW:3 * HW]
        wl = blk[:, 3 * HW:3 * HW + 2 * LORA]
        al = blk[:, 3 * HW + 2 * LORA:3 * HW + 4 * LORA]
        dsl = slice(d * HW, (d + 1) * HW)
        logw = -RWKV_DECAY_SCALE * _sigmoid(w0_ref[:, dsl] + _bdot(jnp.tanh(wl), w2_ref[:, dsl]))
        a = _sigmoid(a0_ref[:, dsl] + _bdot(al, a2_ref[:, dsl]))
        kappa = k * kk
        khat = kappa / jnp.maximum(jnp.sqrt(_head_sum(kappa * kappa, ones_rep)), 1e-12)
        kt = k * (1.0 + (a - 1.0) * ka)
        b = khat * a
        if d == 0:
            bonus[pl.ds(t0, CHUNK), :] = _head_sum(r * k * rk, ones_rep) * v
        last = CHUNK - 1 if d == 0 else 0
        logp = _cum_dot(cum_rep[d], logw)
        logp_last = logp[last:last + 1, :]
        e_neg = jnp.exp(-logp)
        e_last = jnp.exp(logp_last - logp)
        return dict(t0=t0, v=v, alpha=jnp.exp(logp - logw) * khat, beta=b * e_neg, kap=kt * e_neg,
                    rho=r * jnp.exp(logp), beta_l=b * e_last, kap_l=kt * e_last,
                    p_last=jnp.exp(logp_last))

    chains = [(d, p) for d in range(2) for p in range(N_PAIRS)]

    def body(j, carry):
        pre = (prep(j, 0), prep(nc - 1 - j, 1))
        st = []
        for d, p in chains:
            sl = slice(p * LANES, (p + 1) * LANES)
            c = {key: (val if key == 't0' else val[:, sl]) for key, val in pre[d].items()}
            c['sl'] = sl
            st.append(c)
        for c in st:
            c['g'] = _bdot_nt(jnp.concatenate([c['alpha'], c['rho']], axis=0),
                              jnp.concatenate([_bd_stack(c['beta'], lo), _bd_stack(c['kap'], lo)], axis=0))
        for (d, p), c in zip(chains, st):
            g = c.pop('g')
            c['n'] = jnp.where(strict[d], g[0:CHUNK, 0:LANES], 0.0)
            c['mk'] = jnp.where(strict[d], g[0:CHUNK, LANES:], 0.0)
            c['mrb'] = jnp.where(incl[d], g[CHUNK:, 0:LANES], 0.0)
            c['mrk'] = jnp.where(incl[d], g[CHUNK:, LANES:], 0.0)
            c['inv'] = eye_pk - c['n']
        for c in st:
            c['pw'] = _bdot(c['n'], _bd_stack(c['n'], lo))
        for c in st:
            c['mkv'] = _bdot(c['mk'], _bd_stack(c['v'], lo))
        for _ in range(4):
            for c in st:
                rp = _bdot(jnp.concatenate([c['inv'], c['pw']], axis=0), _bd_stack(c['pw'], lo))
                c['inv'] = c['inv'] + rp[0:CHUNK]
                c['pw'] = rp[CHUNK:]
        for c in st:
            c['inv'] = c['inv'] + _bdot(c['inv'], _bd_stack(c['pw'], lo))
        for c in st:
            c['wu'] = _bdot(c['inv'], jnp.concatenate([_bd_stack(c['alpha'], lo), _bd_stack(c['mkv'], lo)], axis=1))
        for (d, p), c in zip(chains, st):
            c['s_prev'] = s_s[d, p]
            ws = _bdot_nt(jnp.concatenate([c['wu'][:, 0:LANES], c['rho']], axis=0), c['s_prev'])
            c['u'] = ws[0:CHUNK] + c['wu'][:, LANES:]
            c['ys'] = ws[CHUNK:]
        for c in st:
            c['upd'] = _bdot_tn(jnp.concatenate([c['v'], -c['u']], axis=0),
                                jnp.concatenate([c['kap_l'], c['beta_l']], axis=0))
        for c in st:
            c['y'] = c['ys'] + _bdot(jnp.concatenate([c['mrk'], -c['mrb']], axis=1),
                                     jnp.concatenate([_bd_stack(c['v'], lo), _bd_stack(c['u'], lo)], axis=0))
        for (d, p), c in zip(chains, st):
            s_s[d, p] = c['s_prev'] * c['p_last'] + jnp.where(bd_mask, c['upd'], 0.0)
            yacc[pl.ds(c['t0'], CHUNK), c['sl']] = yacc[pl.ds(c['t0'], CHUNK), c['sl']] + c['y']
        return carry

    lax.fori_loop(0, nc, body, 0)
    s_out_ref[...] = s_s[...]

    ng = ng_ref[...]
    rb = min(t, 256)

    def epilogue(i, carry):
        r0 = pl.multiple_of(i * rb, rb)
        y = yacc[pl.ds(r0, rb), :]
        yn = y * lax.rsqrt(_head_sum(y * y, ones_rep) * (1.0 / HD) + EPS) * ng + bonus[pl.ds(r0, rb), :]
        y_ref[pl.ds(r0, rb), :] = (yn * _silu(rz_ref[pl.ds(r0, rb), :])).astype(y_ref.dtype)
        return carry

    lax.fori_loop(0, t // rb, epilogue, 0)


def _rwkv(rs, rz, mu, w0, a0, w2bd, a2bd, kk, ka, rk, norm_g, s0, grid_shift):
    bsz, t, _ = rs.shape
    big = t > 512
    st = lambda shape: pl.BlockSpec((None,) + shape, lambda b: (b,) + (0,) * len(shape))
    return pl.pallas_call(
        functools.partial(_rwkv_kernel, grid_shift=grid_shift),
        grid=(bsz,),
        in_specs=[_seq_spec(t, R_SHIFT, big), _seq_spec(t, HW, big),
                  _full_spec((1, R_SHIFT)), _full_spec((1, 2 * HW)), _full_spec((1, 2 * HW)),
                  _full_spec((2 * LORA, 2 * HW)), _full_spec((2 * LORA, 2 * HW)),
                  _full_spec((1, HW)), _full_spec((1, HW)), _full_spec((1, HW)), _full_spec((1, HW)),
                  st((2, N_PAIRS, 2 * CHUNK, LANES))],
        out_specs=[pl.BlockSpec((None, t, HW), lambda b: (b, 0, 0)),
                   st((2, N_PAIRS, 2 * CHUNK, LANES))],
        out_shape=[jax.ShapeDtypeStruct((bsz, t, HW), BF16),
                   jax.ShapeDtypeStruct((bsz, 2, N_PAIRS, 2 * CHUNK, LANES), F32)],
        scratch_shapes=[pltpu.VMEM((t, HW), F32), pltpu.VMEM((t, HW), F32),
                        pltpu.VMEM((2, N_PAIRS, 2 * CHUNK, LANES), F32)],
        compiler_params=pltpu.CompilerParams(dimension_semantics=("arbitrary",),
                                             vmem_limit_bytes=VMEM_LIMIT),
        name="rwkv",
    )(rs, rz, mu, w0, a0, w2bd, a2bd, kk, ka, rk, norm_g, s0)


def _lru_kernel(xz_ref, conv_ref, cb_ref, w_ref, b_ref, lam_ref, h0_ref, y_ref, hfin_ref, acc, hc):
    t = xz_ref.shape[0]
    nb = t // LRU_BLOCK
    row = _iota2((LRU_BLOCK, L_W), 0)
    conv = conv_ref[...]
    cb = cb_ref[...]
    sp = _softplus(-lam_ref[...])

    acc[...] = jnp.zeros_like(acc)
    hc[...] = h0_ref[...]

    def process(j, d):
        t0 = pl.multiple_of(j * LRU_BLOCK, LRU_BLOCK)
        x = xz_ref[pl.ds(t0, LRU_BLOCK), 0:L_W]
        prev8 = xz_ref[pl.ds(pl.multiple_of(jnp.maximum(t0 - SUBLANES, 0), SUBLANES), SUBLANES), 0:L_W]
        next8 = xz_ref[pl.ds(pl.multiple_of(jnp.minimum(t0 + LRU_BLOCK, t - SUBLANES), SUBLANES), SUBLANES), 0:L_W]
        prev8 = jnp.where(j > 0, prev8, 0.0)
        next8 = jnp.where(j < nb - 1, next8, 0.0)
        xm1 = jnp.where(row == 0, prev8[7:8, :], pltpu.roll(x, 1, 0))
        xm2 = jnp.where(row == 0, prev8[6:7, :], jnp.where(row == 1, prev8[7:8, :], pltpu.roll(x, 2, 0)))
        xp1 = jnp.where(row == LRU_BLOCK - 1, next8[0:1, :], pltpu.roll(x, LRU_BLOCK - 1, 0))
        xc = conv[0:1, :] * xm2 + conv[1:2, :] * xm1 + conv[2:3, :] * x + conv[3:4, :] * xp1 + cb
        pre = _bdot(xc, w_ref[:, 2 * d * L_W:2 * (d + 1) * L_W]) + b_ref[:, 2 * d * L_W:2 * (d + 1) * L_W]
        rg = _sigmoid(pre[:, 0:L_W])
        ig = _sigmoid(pre[:, L_W:])
        log_a = -LRU_C * rg * sp[:, d * L_W:(d + 1) * L_W]
        a_cum = jnp.exp(log_a)
        b_cum = jnp.sqrt(jnp.tanh(-log_a) * (1.0 + a_cum * a_cum)) * ig * xc
        k = 1
        while k < LRU_BLOCK:
            if d == 0:
                keep = row >= k
                a_sh = jnp.where(keep, pltpu.roll(a_cum, k, 0), 1.0)
                b_sh = jnp.where(keep, pltpu.roll(b_cum, k, 0), 0.0)
            else:
                keep = row < LRU_BLOCK - k
                a_sh = jnp.where(keep, pltpu.roll(a_cum, LRU_BLOCK - k, 0), 1.0)
                b_sh = jnp.where(keep, pltpu.roll(b_cum, LRU_BLOCK - k, 0), 0.0)
            b_cum = a_cum * b_sh + b_cum
            a_cum = a_cum * a_sh
            k *= 2
        h = b_cum + a_cum * hc[d]
        last = LRU_BLOCK - 1 if d == 0 else 0
        hc[d] = h[last:last + 1, :]
        acc[pl.ds(t0, LRU_BLOCK), :] = acc[pl.ds(t0, LRU_BLOCK), :] + h

    def body(j, carry):
        process(j, 0)
        process(nb - 1 - j, 1)
        return carry

    lax.fori_loop(0, nb, body, 0)
    hfin_ref[...] = hc[...]

    def epilogue(i, carry):
        r0 = pl.multiple_of(i * LRU_BLOCK, LRU_BLOCK)
        z = xz_ref[pl.ds(r0, LRU_BLOCK), L_W:2 * L_W]
        y_ref[pl.ds(r0, LRU_BLOCK), :] = (acc[pl.ds(r0, LRU_BLOCK), :] * _silu(z)).astype(y_ref.dtype)
        return carry

    lax.fori_loop(0, nb, epilogue, 0)


def _lru(xz, conv, conv_b, wbd, bias, lam, h0):
    bsz, t, _ = xz.shape
    st = lambda shape: pl.BlockSpec((None,) + shape, lambda b: (b,) + (0,) * len(shape))
    return pl.pallas_call(
        _lru_kernel,
        grid=(bsz,),
        in_specs=[_seq_spec(t, 2 * L_W, False), _full_spec((CONV_W, L_W)), _full_spec((1, L_W)),
                  _full_spec((L_W, 4 * L_W)), _full_spec((1, 4 * L_W)), _full_spec((1, 2 * L_W)),
                  st((2, 1, L_W))],
        out_specs=[pl.BlockSpec((None, t, L_W), lambda b: (b, 0, 0)), st((2, 1, L_W))],
        out_shape=[jax.ShapeDtypeStruct((bsz, t, L_W), BF16),
                   jax.ShapeDtypeStruct((bsz, 2, 1, L_W), F32)],
        scratch_shapes=[pltpu.VMEM((t, L_W), F32), pltpu.VMEM((2, 1, L_W), F32)],
        compiler_params=pltpu.CompilerParams(dimension_semantics=("arbitrary",),
                                             vmem_limit_bytes=VMEM_LIMIT),
        name="lru",
    )(xz, conv, conv_b, wbd, bias, lam, h0)


def _permute_w_in(w_in_l):
    valid = jnp.asarray(_GATE_SRC >= 0)
    gates = w_in_l[:, 5 * HW:M_COLS]
    ge = jnp.where(valid, gates[:, np.maximum(_GATE_SRC, 0)], 0.0)
    go = jnp.where(valid, gates[:, np.maximum(_GATE_SRC, 0) + 1], 0.0)
    return jnp.concatenate([w_in_l[:, 0:5 * HW], ge, go, w_in_l[:, M_COLS:]], axis=1).astype(BF16)


def _gate_source_index():
    src = np.full((LANES,), -1, np.int32)
    for dd in range(2):
        for is_f in range(2):
            for p in range(N_PAIRS):
                src[_gate_col(dd, is_f, p)] = is_f * 2 * N_HEADS + dd * N_HEADS + 2 * p
    return src


_GATE_SRC = _gate_source_index()


def _gate_bias(m_bi_l, m_bf_l):
    flat = jnp.concatenate([m_bi_l.reshape(-1), m_bf_l.reshape(-1)])
    idx = np.maximum(_GATE_SRC, 0)
    valid = jnp.asarray(_GATE_SRC >= 0)
    return jnp.stack([jnp.where(valid, flat[idx], 0.0), jnp.where(valid, flat[idx + 1], 0.0)])


def _block_diag(blocks):
    n, a, b = blocks.shape
    eye = jnp.eye(n, dtype=blocks.dtype)
    return (eye[:, None, :, None] * blocks[:, :, None, :]).reshape(n * a, n * b)


def _pair_block_diag(x):
    lead = x.shape[:-3]
    a, b = x.shape[-2:]
    xp = x.reshape(lead + (N_PAIRS, 2, a, b))
    eye = jnp.eye(2, dtype=x.dtype)
    out = xp[..., :, :, :, None, :] * eye[:, None, :, None]
    return out.reshape(lead + (N_PAIRS, 2 * a, 2 * b))


def _pair_diag_blocks(x, a, b):
    lead = x.shape[:-3]
    xp = x.reshape(lead + (N_PAIRS, 2, a, 2, b))
    blocks = jnp.stack([xp[..., 0, :, 0, :], xp[..., 1, :, 1, :]], axis=-3)
    return blocks.reshape(lead + (N_HEADS, a, b))


def _pack_mlstm_state(c_st, n_st, m_st):
    cbd = _pair_block_diag(c_st)
    nbd = _pair_block_diag(jnp.broadcast_to(n_st[..., None], n_st.shape + (HD,)))
    mrow = jnp.repeat(m_st.reshape(m_st.shape[:2] + (N_PAIRS, 1, 2)), HD, axis=-1)
    return jnp.concatenate([cbd, nbd], axis=-1), mrow


def _unpack_mlstm_state(cn, mrow):
    c_st = _pair_diag_blocks(cn[..., 0:LANES], HD, HD)
    n_st = _pair_diag_blocks(cn[..., LANES:], HD, HD)[..., 0]
    m_st = mrow[..., 0, ::HD].reshape(mrow.shape[:2] + (N_HEADS,))
    return c_st, n_st, m_st


def _layer_params(l, g_pre, g_post, w_in, w_out, m_bi, m_bf, m_norm, r_mu, r_w0, r_w2, r_a0, r_a2,
                  r_kk, r_ka, r_rk, r_norm, l_conv, l_conv_b, l_wa, l_ba, l_wx, l_bx, l_lambda):
    row = lambda v: v.reshape(1, -1)
    return dict(
        g_pre=row(g_pre[l]), g_post=row(g_post[l]),
        w_in=_permute_w_in(w_in[l]), w_out=w_out[l].astype(BF16),
        gate_bias=_gate_bias(m_bi[l], m_bf[l]), m_norm=row(m_norm[l]),
        r_mu=row(r_mu[l]), r_w0=row(r_w0[l]), r_a0=row(r_a0[l]),
        r_w2=_block_diag(r_w2[l]).astype(BF16), r_a2=_block_diag(r_a2[l]).astype(BF16),
        r_kk=row(r_kk[l]), r_ka=row(r_ka[l]), r_rk=row(r_rk[l]), r_norm=row(r_norm[l]),
        l_conv=l_conv[l], l_conv_b=row(l_conv_b[l]),
        l_w=jnp.concatenate([_block_diag(l_wa[l][0]), _block_diag(l_wx[l][0]),
                             _block_diag(l_wa[l][1]), _block_diag(l_wx[l][1])], axis=1).astype(BF16),
        l_b=jnp.concatenate([l_ba[l][0], l_bx[l][0], l_ba[l][1], l_bx[l][1]]).reshape(1, -1),
        l_lambda=row(l_lambda[l]),
    )


def _trunk_layer(x, mod, lp, states, per_seq_mod, grid_shift):
    m_c, m_n, m_m, r_s, l_h = states
    qkv, oz, ge, go, rs, rz, lxz = _inproj(x, mod, lp['g_pre'], lp['w_in'], per_seq_mod)
    cn0, m0 = _pack_mlstm_state(m_c, m_n, m_m)
    y_m, cn1, m1 = _mlstm(qkv, oz, ge, go, lp['gate_bias'], lp['m_norm'], cn0, m0)
    y_r, s1 = _rwkv(rs, rz, lp['r_mu'], lp['r_w0'], lp['r_a0'], lp['r_w2'], lp['r_a2'], lp['r_kk'],
                    lp['r_ka'], lp['r_rk'], lp['r_norm'], _pair_block_diag(r_s), grid_shift)
    y_l, h1 = _lru(lxz, lp['l_conv'], lp['l_conv_b'], lp['l_w'], lp['l_b'], lp['l_lambda'],
                   l_h[:, :, None, :])
    y = _outproj(y_m, y_r, y_l, x, mod, lp['g_post'], lp['w_out'], per_seq_mod)
    new_states = _unpack_mlstm_state(cn1, m1) + (_pair_diag_blocks(s1, HD, HD), h1[:, :, 0, :])
    return y, new_states


def kernel(x_prompt, x_sample, c, state_mlstm_C, state_mlstm_n, state_mlstm_m, state_rwkv, state_rglru, c_ctx, g_pre, g_post, w_mod, b_mod, w_in, w_out, m_bi, m_bf, m_norm, r_mu, r_w0, r_w2, r_a0, r_a2, r_kk, r_ka, r_rk, r_norm, l_conv, l_conv_b, l_wa, l_ba, l_wx, l_bx, l_lambda):
    depth = w_in.shape[0]
    bp = x_prompt.shape[0]
    bs = x_sample.shape[0]
    d = x_prompt.shape[-1]
    rows = -(-(1 + bs) // SUBLANES) * SUBLANES
    cc = jnp.zeros((rows, d), F32).at[0].set(c_ctx).at[1:1 + bs].set(c)
    mod = _modulation(cc, w_mod, b_mod)
    zero_states = (jnp.zeros((bp, 2, N_HEADS, HD, HD), F32), jnp.zeros((bp, 2, N_HEADS, HD), F32),
                   jnp.zeros((bp, 2, N_HEADS), F32), jnp.zeros((bp, 2, N_HEADS, HD, HD), F32),
                   jnp.zeros((bp, 2, L_W), F32))
    xp, xs = x_prompt, x_sample
    new_states = []
    for l in range(depth):
        lp = _layer_params(l, g_pre, g_post, w_in, w_out, m_bi, m_bf, m_norm, r_mu, r_w0, r_w2, r_a0,
                           r_a2, r_kk, r_ka, r_rk, r_norm, l_conv, l_conv_b, l_wa, l_ba, l_wx, l_bx,
                           l_lambda)
        xp, st = _trunk_layer(xp, mod[l, 0:1][:, None, :], lp, zero_states, False, False)
        new_states.append(st)
        cache = (state_mlstm_C[:, l].astype(F32), state_mlstm_n[:, l].astype(F32),
                 state_mlstm_m[:, l].astype(F32), state_rwkv[:, l].astype(F32),
                 state_rglru[:, l].astype(F32))
        xs, _ = _trunk_layer(xs, mod[l, 1:1 + bs][:, None, :], lp, cache, True, True)
    stacked = tuple(jnp.stack([st[i] for st in new_states], axis=1) for i in range(5))
    return (xp, xs) + stacked
```

```python
import functools

import numpy as np
import jax
import jax.numpy as jnp
from jax import lax
from jax.experimental import pallas as pl
from jax.experimental.pallas import tpu as pltpu

F32 = jnp.float32
BF16 = jnp.bfloat16
HIGHEST = lax.Precision.HIGHEST

D_MODEL = 1024
EPS = 1e-6
HD = 64
N_HEADS = 6
N_PAIRS = N_HEADS // 2
HW = N_HEADS * HD
CHUNK = 64
GRID_W = 64
LORA = 64
R_SHIFT = 3 * HW + 4 * LORA
L_W = 256
L_BLOCKS = 4
CONV_W = 4
LRU_C = 8.0
RWKV_DECAY_SCALE = 0.6065306597126334
M_COLS = 5 * HW + 4 * N_HEADS
IN_COLS = M_COLS + R_SHIFT + HW + 2 * L_W

LANES = 128
SUBLANES = 8
LRU_BLOCK = 128
VMEM_LIMIT = 56 * 1024 * 1024

SEG_WIDTHS = (3 * HW, 2 * HW, LANES, LANES, R_SHIFT, HW, 2 * L_W)
SEG_OFFS = tuple(int(v) for v in np.cumsum((0,) + SEG_WIDTHS))
IN_COLS_PAD = SEG_OFFS[-1]
N_TILE = 512


def _gate_col(d, is_f, p):
    return 8 * d + 4 * is_f + p


def _bdot(a, b):
    return jnp.dot(a.astype(BF16), b.astype(BF16), preferred_element_type=F32)


def _bdot_nt(a, b):
    return lax.dot_general(a.astype(BF16), b.astype(BF16), (((1,), (1,)), ((), ())),
                           preferred_element_type=F32)


def _bdot_tn(a, b):
    return lax.dot_general(a.astype(BF16), b.astype(BF16), (((0,), (0,)), ((), ())),
                           preferred_element_type=F32)


def _fdot(a, b):
    return jnp.dot(a, b, precision=HIGHEST, preferred_element_type=F32)


def _split_bf16(x, terms):
    out = []
    for _ in range(terms - 1):
        hi = x.astype(BF16)
        out.append(hi)
        x = x - hi.astype(F32)
    out.append(x.astype(BF16))
    return out


def _cum_dot(cum_rep, x, terms=3):
    return jnp.dot(cum_rep, jnp.concatenate(_split_bf16(x, terms), axis=0), preferred_element_type=F32)


def _head_sum(x, ones_rep, terms=2):
    outs = []
    for p in range(x.shape[1] // LANES):
        parts = _split_bf16(x[:, p * LANES:(p + 1) * LANES], terms)
        outs.append(jnp.dot(jnp.concatenate(parts, axis=1), ones_rep, preferred_element_type=F32))
    return outs[0] if len(outs) == 1 else jnp.concatenate(outs, axis=1)


def _sigmoid(x):
    return 1.0 / (1.0 + jnp.exp(-x))


def _silu(x):
    return x * _sigmoid(x)


def _softplus(x):
    return jnp.maximum(x, 0.0) + jnp.log1p(jnp.exp(-jnp.abs(x)))


def _log_sigmoid(x):
    return -_softplus(-x)


def _iota2(shape, dim):
    return lax.broadcasted_iota(jnp.int32, shape, dim)


def _head_ones(n):
    return (_iota2((n, n), 0) // HD == _iota2((n, n), 1) // HD).astype(F32)


def _bd_stack(x, lo):
    return jnp.concatenate([jnp.where(lo, x, 0.0), jnp.where(lo, 0.0, x)], axis=0)


def _pair_col(q, j, lo):
    return jnp.where(lo, q[0:CHUNK, j:j + 1], q[CHUNK:2 * CHUNK, j:j + 1])


def _mod_kernel(c_ref, w_ref, b_ref, o_ref):
    o_ref[...] = _fdot(_silu(c_ref[...]), w_ref[...]) + b_ref[...]


def _modulation(cc, w_mod, b_mod):
    depth, d, n = w_mod.shape
    rows = cc.shape[0]
    return pl.pallas_call(
        _mod_kernel,
        grid=(depth, n // N_TILE),
        in_specs=[pl.BlockSpec((rows, d), lambda l, j: (0, 0)),
                  pl.BlockSpec((None, d, N_TILE), lambda l, j: (l, 0, j)),
                  pl.BlockSpec((None, 1, N_TILE), lambda l, j: (l, 0, j))],
        out_specs=pl.BlockSpec((None, rows, N_TILE), lambda l, j: (l, 0, j)),
        out_shape=jax.ShapeDtypeStruct((depth, rows, n), F32),
        compiler_params=pltpu.CompilerParams(dimension_semantics=("arbitrary", "arbitrary")),
        name="modulation",
    )(cc, w_mod, b_mod.reshape(depth, 1, n))


def _inproj_kernel(x_ref, mod_ref, g_ref, w_ref, *out_refs):
    x = x_ref[...]
    d = x.shape[-1]
    mod = mod_ref[...]
    h = x * lax.rsqrt(jnp.mean(x * x, axis=-1, keepdims=True) + EPS) * g_ref[...]
    h = (h * (1.0 + mod[:, d:2 * d]) + mod[:, 0:d]).astype(BF16)
    for o_ref, a, b in zip(out_refs, SEG_OFFS[:-1], SEG_OFFS[1:]):
        for n0 in range(a, b, N_TILE):
            n1 = min(n0 + N_TILE, b)
            o_ref[:, n0 - a:n1 - a] = jnp.dot(h, w_ref[:, n0:n1], preferred_element_type=F32)


def _inproj(x, mod, g_pre, w_in_p, per_seq_mod):
    bsz, t, d = x.shape
    tm = min(t, 512)
    mod_idx = (lambda b, i: (b, 0, 0)) if per_seq_mod else (lambda b, i: (0, 0, 0))
    return pl.pallas_call(
        _inproj_kernel,
        grid=(bsz, t // tm),
        in_specs=[pl.BlockSpec((None, tm, d), lambda b, i: (b, i, 0)),
                  pl.BlockSpec((None, 1, 3 * d), mod_idx),
                  pl.BlockSpec((1, d), lambda b, i: (0, 0)),
                  pl.BlockSpec((d, IN_COLS_PAD), lambda b, i: (0, 0), pipeline_mode=pl.Buffered(1))],
        out_specs=[pl.BlockSpec((None, tm, w), lambda b, i: (b, i, 0)) for w in SEG_WIDTHS],
        out_shape=[jax.ShapeDtypeStruct((bsz, t, w), F32) for w in SEG_WIDTHS],
        compiler_params=pltpu.CompilerParams(dimension_semantics=("arbitrary", "arbitrary"),
                                             vmem_limit_bytes=VMEM_LIMIT),
        name="inproj",
    )(x, mod, g_pre, w_in_p)


def _outproj_kernel(ym_ref, yr_ref, yl_ref, x_ref, mod_ref, g_ref, w_ref, o_ref):
    d = x_ref.shape[-1]
    o = jnp.dot(ym_ref[...], w_ref[0:HW, :], preferred_element_type=F32)
    o = o + jnp.dot(yr_ref[...], w_ref[HW:2 * HW, :], preferred_element_type=F32)
    o = o + jnp.dot(yl_ref[...], w_ref[2 * HW:2 * HW + L_W, :], preferred_element_type=F32)
    on = o * lax.rsqrt(jnp.mean(o * o, axis=-1, keepdims=True) + EPS) * g_ref[...]
    o_ref[...] = x_ref[...] + mod_ref[:, 2 * d:3 * d] * on


def _outproj(ym, yr, yl, x, mod, g_post, w_out_b, per_seq_mod):
    bsz, t, d = x.shape
    tm = min(t, 512)
    mod_idx = (lambda b, i: (b, 0, 0)) if per_seq_mod else (lambda b, i: (0, 0, 0))
    tok = lambda w: pl.BlockSpec((None, tm, w), lambda b, i: (b, i, 0))
    return pl.pallas_call(
        _outproj_kernel,
        grid=(bsz, t // tm),
        in_specs=[tok(HW), tok(HW), tok(L_W), tok(d),
                  pl.BlockSpec((None, 1, 3 * d), mod_idx),
                  pl.BlockSpec((1, d), lambda b, i: (0, 0)),
                  pl.BlockSpec((2 * HW + L_W, d), lambda b, i: (0, 0))],
        out_specs=tok(d),
        out_shape=jax.ShapeDtypeStruct((bsz, t, d), F32),
        compiler_params=pltpu.CompilerParams(dimension_semantics=("arbitrary", "arbitrary"),
                                             vmem_limit_bytes=VMEM_LIMIT),
        name="outproj",
    )(ym, yr, yl, x, mod, g_post, w_out_b)


def _mlstm_kernel(*refs, has_state_in, has_state_out):
    qkv_ref, oz_ref, ge_ref, go_ref, gb_ref, ng_ref = refs[0:6]
    refs = refs[6:]
    if has_state_in:
        c0_ref, n0_ref, m0_ref = refs[0:3]
        refs = refs[3:]
    y_ref = refs[0]
    refs = refs[1:]
    if has_state_out:
        c_out_ref, n_out_ref, m_out_ref = refs[0:3]
        refs = refs[3:]
    hacc, cn_s, m_s = refs
    t = qkv_ref.shape[0]
    nc = t // CHUNK
    lane = _iota2((CHUNK, LANES), 1)
    row = _iota2((CHUNK, LANES), 0)
    lo = lane < HD
    s_idx = lane % HD
    causal = (s_idx <= row, s_idx >= row)
    r2 = _iota2((2 * CHUNK, 2 * CHUNK), 0)
    c2 = _iota2((2 * CHUNK, 2 * CHUNK), 1)
    same = (r2 // CHUNK) == (c2 // CHUNK)
    cum = ((same & (c2 <= r2)).astype(BF16), (same & (c2 >= r2)).astype(BF16))
    cum_rep = tuple(jnp.concatenate([m, m, m], axis=1) for m in cum)
    bd_ones = same.astype(F32)
    ones_rep = jnp.concatenate([same.astype(BF16), same.astype(BF16)], axis=0)
    bd_mask2 = jnp.concatenate([same, same], axis=1)
    colid = _iota2((1, LANES), 1)
    is_f = (colid % 8) >= 4
    gb = gb_ref[...]

    hacc[...] = jnp.zeros_like(hacc)
    cn_s[...] = jnp.zeros_like(cn_s)
    m_s[...] = jnp.zeros_like(m_s)
    if has_state_in:
        m_s[...] = m0_ref[...]
        for d in range(2):
            for p in range(N_PAIRS):
                for h in range(2):
                    cn_s[d, p, h * HD:(h + 1) * HD, h * HD:(h + 1) * HD] = c0_ref[d, 2 * p + h]
                n_rows = jnp.where(same, jnp.broadcast_to(n0_ref[d, p], (2 * CHUNK, LANES)), 0.0)
                cn_s[d, p, :, LANES:2 * LANES] = n_rows.T

    def gate_table(t0, d):
        ve = ge_ref[pl.ds(t0, CHUNK), :] + gb[0:1, :]
        vo = go_ref[pl.ds(t0, CHUNK), :] + gb[1:2, :]
        ve = jnp.where(is_f, _log_sigmoid(ve), ve)
        vo = jnp.where(is_f, _log_sigmoid(vo), vo)
        x = jnp.concatenate([ve, vo], axis=0)
        q = jnp.where(is_f, _cum_dot(cum_rep[d], x), x)
        return q, q.T

    chains = [(d, p) for d in range(2) for p in range(N_PAIRS)]

    def body(j, carry):
        t0s = (pl.multiple_of(j * CHUNK, CHUNK), pl.multiple_of((nc - 1 - j) * CHUNK, CHUNK))
        tabs = [gate_table(t0s[d], d) for d in range(2)]
        st = []
        for d, p in chains:
            t0 = t0s[d]
            q_tab, q_tab_t = tabs[d]
            ji, jf = _gate_col(d, 0, p), _gate_col(d, 1, p)
            c = dict(t0=t0, sl=slice(p * LANES, (p + 1) * LANES))
            c['bcol'] = _pair_col(q_tab, jf, lo)
            c['icol'] = _pair_col(q_tab, ji, lo)
            brow = q_tab_t[jf:jf + 1, :]
            irow = q_tab_t[ji:ji + 1, :]
            c['q'] = qkv_ref[pl.ds(t0, CHUNK), p * LANES:(p + 1) * LANES]
            c['k'] = qkv_ref[pl.ds(t0, CHUNK), HW + p * LANES:HW + (p + 1) * LANES] * (HD ** -0.5)
            c['v'] = qkv_ref[pl.ds(t0, CHUNK), 2 * HW + p * LANES:2 * HW + (p + 1) * LANES]
            c['mprev'] = m_s[d, p]
            c['cn'] = cn_s[d, p]
            dmat = jnp.where(causal[d], c['bcol'] - brow + irow, -jnp.inf)
            inter = c['bcol'] + c['mprev']
            mx = jnp.where(lo,
                           jnp.max(jnp.where(lo, dmat, -jnp.inf), axis=1, keepdims=True),
                           jnp.max(jnp.where(lo, -jnp.inf, dmat), axis=1, keepdims=True))
            c['m_t'] = jnp.maximum(inter, mx)
            c['pexp'] = jnp.exp(dmat - c['m_t'])
            c['sc'] = jnp.exp(inter - c['m_t'])
            last = CHUNK - 1 if d == 0 else 0
            b_last = c['bcol'][last:last + 1, :]
            g = b_last - c['bcol'] + c['icol']
            c['m_new'] = jnp.maximum(b_last + c['mprev'], jnp.max(g, axis=0, keepdims=True))
            c['dec'] = jnp.exp(b_last + c['mprev'] - c['m_new'])
            c['wk'] = jnp.exp(g - c['m_new'])
            st.append(c)
        for c in st:
            c['s'] = _bdot_nt(c['q'], _bd_stack(c['k'], lo)) * c['pexp']
        for c in st:
            c['a1'] = _bdot(c['q'], c['cn'])
        for c in st:
            c['upd'] = _bdot_tn(c['wk'] * c['k'], jnp.concatenate([c['v'], jnp.ones_like(c['v'])], axis=1))
        for c in st:
            c['a2'] = _bdot(c['s'], jnp.concatenate([_bd_stack(c['v'], lo), bd_ones], axis=1))
        for (d, p), c in zip(chains, st):
            num = c['sc'] * c['a1'][:, 0:LANES] + c['a2'][:, 0:LANES]
            den = c['sc'] * c['a1'][:, LANES:] + c['a2'][:, LANES:]
            h = num / jnp.maximum(jnp.abs(den), jnp.exp(-c['m_t']))
            hacc[pl.ds(c['t0'], CHUNK), c['sl']] = hacc[pl.ds(c['t0'], CHUNK), c['sl']] + h
            cn_s[d, p] = (jnp.concatenate([c['dec'], c['dec']], axis=1) * c['cn']
                          + jnp.where(bd_mask2, c['upd'], 0.0))
            m_s[d, p] = c['m_new']
        return carry

    lax.fori_loop(0, nc, body, 0)
    if has_state_out:
        m_out_ref[...] = m_s[...]
        lo2 = _iota2((1, LANES), 1) < HD
        for d in range(2):
            for p in range(N_PAIRS):
                for h in range(2):
                    c_out_ref[d, 2 * p + h] = cn_s[d, p, h * HD:(h + 1) * HD, h * HD:(h + 1) * HD]
                n_t = cn_s[d, p, :, LANES:2 * LANES].T
                n_out_ref[d, p] = jnp.where(lo2, n_t[0:1, :], n_t[HD:HD + 1, :])

    ng = ng_ref[...]
    rb = min(t, 256)

    def epilogue(i, carry):
        r0 = pl.multiple_of(i * rb, rb)
        h = hacc[pl.ds(r0, rb), :]
        hn = h * lax.rsqrt(_head_sum(h * h, ones_rep) * (1.0 / HD) + EPS) * ng
        o = oz_ref[pl.ds(r0, rb), 0:HW]
        z = oz_ref[pl.ds(r0, rb), HW:2 * HW]
        y_ref[pl.ds(r0, rb), :] = (hn * _sigmoid(o) * _silu(z)).astype(y_ref.dtype)
        return carry

    lax.fori_loop(0, t // rb, epilogue, 0)


def _seq_spec(t, w, single_buffer):
    if single_buffer:
        return pl.BlockSpec((None, t, w), lambda b: (b, 0, 0), pipeline_mode=pl.Buffered(1))
    return pl.BlockSpec((None, t, w), lambda b: (b, 0, 0))


def _full_spec(shape):
    n = len(shape)
    return pl.BlockSpec(shape, lambda b: (0,) * n)


def _state_spec(shape):
    return pl.BlockSpec((None,) + shape, lambda b: (b,) + (0,) * len(shape))


def _pair_rows(x):
    return jnp.repeat(x.reshape(x.shape[:2] + (N_PAIRS, 1, 2)), HD, axis=-1)


def _mlstm(qkv, oz, ge, go, gate_bias, norm_g, state_in, want_state):
    bsz, t, _ = qkv.shape
    big = t > 512
    c_shape, row_shape = (2, N_HEADS, HD, HD), (2, N_PAIRS, 1, LANES)
    in_specs = [_seq_spec(t, 3 * HW, big), _seq_spec(t, 2 * HW, big), _seq_spec(t, LANES, big),
                _seq_spec(t, LANES, big), _full_spec((2, LANES)), _full_spec((1, HW))]
    args = [qkv, oz, ge, go, gate_bias, norm_g]
    if state_in is not None:
        c0, n0, m0 = state_in
        in_specs += [_state_spec(c_shape), _state_spec(row_shape), _state_spec(row_shape)]
        args += [c0, n0.reshape((bsz,) + row_shape), _pair_rows(m0)]
    out_specs = [pl.BlockSpec((None, t, HW), lambda b: (b, 0, 0))]
    out_shape = [jax.ShapeDtypeStruct((bsz, t, HW), BF16)]
    if want_state:
        out_specs += [_state_spec(c_shape), _state_spec(row_shape), _state_spec(row_shape)]
        out_shape += [jax.ShapeDtypeStruct((bsz,) + c_shape, F32),
                      jax.ShapeDtypeStruct((bsz,) + row_shape, F32),
                      jax.ShapeDtypeStruct((bsz,) + row_shape, F32)]
    outs = pl.pallas_call(
        functools.partial(_mlstm_kernel, has_state_in=state_in is not None, has_state_out=want_state),
        grid=(bsz,),
        in_specs=in_specs,
        out_specs=out_specs,
        out_shape=out_shape,
        scratch_shapes=[pltpu.VMEM((t, HW), F32),
                        pltpu.VMEM((2, N_PAIRS, 2 * CHUNK, 2 * LANES), F32),
                        pltpu.VMEM(row_shape, F32)],
        compiler_params=pltpu.CompilerParams(dimension_semantics=("arbitrary",),
                                             vmem_limit_bytes=VMEM_LIMIT),
        name="mlstm",
    )(*args)
    if not want_state:
        return outs[0], None
    y, c1, n1, m1 = outs
    return y, (c1, n1.reshape(bsz, 2, N_HEADS, HD), m1[:, :, :, 0, ::HD].reshape(bsz, 2, N_HEADS))


def _rwkv_kernel(*refs, grid_shift, has_state_in, has_state_out):
    (rs_ref, rz_ref, mu_ref, w0_ref, a0_ref, w2_ref, a2_ref, kk_ref, ka_ref, rk_ref, ng_ref) = refs[0:11]
    refs = refs[11:]
    if has_state_in:
        s0_ref = refs[0]
        refs = refs[1:]
    y_ref = refs[0]
    refs = refs[1:]
    if has_state_out:
        s_out_ref = refs[0]
        refs = refs[1:]
    yacc, bonus, s_s = refs
    t = rs_ref.shape[0]
    nc = t // CHUNK
    lane = _iota2((CHUNK, LANES), 1)
    row = _iota2((CHUNK, LANES), 0)
    lo = lane < HD
    s_idx = lane % HD
    strict = (s_idx < row, s_idx > row)
    incl = (s_idx <= row, s_idx >= row)
    eye_pk = (s_idx == row).astype(F32)
    r64 = _iota2((CHUNK, CHUNK), 0)
    c64 = _iota2((CHUNK, CHUNK), 1)
    cum = ((c64 <= r64).astype(BF16), (c64 >= r64).astype(BF16))
    cum_rep = tuple(jnp.concatenate([m, m, m], axis=1) for m in cum)
    r2 = _iota2((2 * CHUNK, 2 * CHUNK), 0)
    c2 = _iota2((2 * CHUNK, 2 * CHUNK), 1)
    bd_mask = (r2 // CHUNK) == (c2 // CHUNK)
    ones_rep = jnp.concatenate([bd_mask.astype(BF16), bd_mask.astype(BF16)], axis=0)
    roww = _iota2((CHUNK, R_SHIFT), 0)
    mu = mu_ref[...]
    kk = kk_ref[...]
    ka = ka_ref[...]
    rk = rk_ref[...]

    yacc[...] = jnp.zeros_like(yacc)
    s_s[...] = jnp.zeros_like(s_s)
    if has_state_in:
        for d in range(2):
            for p in range(N_PAIRS):
                for h in range(2):
                    s_s[d, p, h * HD:(h + 1) * HD, h * HD:(h + 1) * HD] = s0_ref[d, 2 * p + h]

    def shifted(x, c, t0):
        left = jnp.where(roww == 0, 0.0, pltpu.roll(x, 1, 0))
        right = jnp.where(roww == CHUNK - 1, 0.0, pltpu.roll(x, CHUNK - 1, 0))
        has_prev = c > 0
        has_next = c < nc - 1
        if grid_shift:
            up = rs_ref[pl.ds(pl.multiple_of(jnp.maximum(t0 - GRID_W, 0), CHUNK), CHUNK), :]
            down = rs_ref[pl.ds(pl.multiple_of(jnp.minimum(t0 + GRID_W, t - CHUNK), CHUNK), CHUNK), :]
            up = jnp.where(has_prev, up, 0.0)
            down = jnp.where(has_next, down, 0.0)
            return 0.25 * (up + down + left + right)
        prev8 = rs_ref[pl.ds(pl.multiple_of(jnp.maximum(t0 - SUBLANES, 0), SUBLANES), SUBLANES), :]
        next8 = rs_ref[pl.ds(pl.multiple_of(jnp.minimum(t0 + CHUNK, t - SUBLANES), SUBLANES), SUBLANES), :]
        prev_row = jnp.where(has_prev, prev8[SUBLANES - 1:SUBLANES, :], 0.0)
        next_row = jnp.where(has_next, next8[0:1, :], 0.0)
        left = jnp.where(roww == 0, prev_row, left)
        right = jnp.where(roww == CHUNK - 1, next_row, right)
        return 0.5 * (left + right)

    def prep(c, d):
        t0 = pl.multiple_of(c * CHUNK, CHUNK)
        x = rs_ref[pl.ds(t0, CHUNK), :]
        blk = x + mu * (shifted(x, c, t0) - x)
        r = blk[:, 0:HW]
        k = blk[:, HW:2 * HW]
        v = blk[:, 2 * HW:3 * HW]
        wl = blk[:, 3 * HW:3 * HW + 2 * LORA]
        al = blk[:, 3 * HW + 2 * LORA:3 * HW + 4 * LORA]
        dsl = slice(d * HW, (d + 1) * HW)
        logw = -RWKV_DECAY_SCALE * _sigmoid(w0_ref[:, dsl] + _bdot(jnp.tanh(wl), w2_ref[:, dsl]))
        a = _sigmoid(a0_ref[:, dsl] + _bdot(al, a2_ref[:, dsl]))
        kappa = k * kk
        khat = kappa / jnp.maximum(jnp.sqrt(_head_sum(kappa * kappa, ones_rep)), 1e-12)
        kt = k * (1.0 + (a - 1.0) * ka)
        b = khat * a
        if d == 0:
            bonus[pl.ds(t0, CHUNK), :] = _head_sum(r * k * rk, ones_rep) * v
        last = CHUNK - 1 if d == 0 else 0
        logp = _cum_dot(cum_rep[d], logw)
        logp_last = logp[last:last + 1, :]
        e_neg = jnp.exp(-logp)
        e_last = jnp.exp(logp_last - logp)
        return dict(t0=t0, v=v, alpha=jnp.exp(logp - logw) * khat, beta=b * e_neg, kap=kt * e_neg,
                    rho=r * jnp.exp(logp), beta_l=b * e_last, kap_l=kt * e_last,
                    p_last=jnp.exp(logp_last))

    chains = [(d, p) for d in range(2) for p in range(N_PAIRS)]

    def body(j, carry):
        pre = (prep(j, 0), prep(nc - 1 - j, 1))
        st = []
        for d, p in chains:
            sl = slice(p * LANES, (p + 1) * LANES)
            c = {key: (val if key == 't0' else val[:, sl]) for key, val in pre[d].items()}
            c['sl'] = sl
            st.append(c)
        for c in st:
            c['g'] = _bdot_nt(jnp.concatenate([c['alpha'], c['rho']], axis=0),
                              jnp.concatenate([_bd_stack(c['beta'], lo), _bd_stack(c['kap'], lo)], axis=0))
        for (d, p), c in zip(chains, st):
            g = c.pop('g')
            c['n'] = jnp.where(strict[d], g[0:CHUNK, 0:LANES], 0.0)
            c['mk'] = jnp.where(strict[d], g[0:CHUNK, LANES:], 0.0)
            c['mrb'] = jnp.where(incl[d], g[CHUNK:, 0:LANES], 0.0)
            c['mrk'] = jnp.where(incl[d], g[CHUNK:, LANES:], 0.0)
            c['inv'] = eye_pk - c['n']
        for c in st:
            c['pw'] = _bdot(c['n'], _bd_stack(c['n'], lo))
        for c in st:
            c['mkv'] = _bdot(c['mk'], _bd_stack(c['v'], lo))
        for _ in range(4):
            for c in st:
                rp = _bdot(jnp.concatenate([c['inv'], c['pw']], axis=0), _bd_stack(c['pw'], lo))
                c['inv'] = c['inv'] + rp[0:CHUNK]
                c['pw'] = rp[CHUNK:]
        for c in st:
            c['inv'] = c['inv'] + _bdot(c['inv'], _bd_stack(c['pw'], lo))
        for c in st:
            c['wu'] = _bdot(c['inv'], jnp.concatenate([_bd_stack(c['alpha'], lo), _bd_stack(c['mkv'], lo)], axis=1))
        for (d, p), c in zip(chains, st):
            c['s_prev'] = s_s[d, p]
            ws = _bdot_nt(jnp.concatenate([c['wu'][:, 0:LANES], c['rho']], axis=0), c['s_prev'])
            c['u'] = ws[0:CHUNK] + c['wu'][:, LANES:]
            c['ys'] = ws[CHUNK:]
        for c in st:
            c['upd'] = _bdot_tn(jnp.concatenate([c['v'], -c['u']], axis=0),
                                jnp.concatenate([c['kap_l'], c['beta_l']], axis=0))
        for c in st:
            c['y'] = c['ys'] + _bdot(jnp.concatenate([c['mrk'], -c['mrb']], axis=1),
                                     jnp.concatenate([_bd_stack(c['v'], lo), _bd_stack(c['u'], lo)], axis=0))
        for (d, p), c in zip(chains, st):
            s_s[d, p] = c['s_prev'] * c['p_last'] + jnp.where(bd_mask, c['upd'], 0.0)
            yacc[pl.ds(c['t0'], CHUNK), c['sl']] = yacc[pl.ds(c['t0'], CHUNK), c['sl']] + c['y']
        return carry

    lax.fori_loop(0, nc, body, 0)
    if has_state_out:
        for d in range(2):
            for p in range(N_PAIRS):
                for h in range(2):
                    s_out_ref[d, 2 * p + h] = s_s[d, p, h * HD:(h + 1) * HD, h * HD:(h + 1) * HD]

    ng = ng_ref[...]
    rb = min(t, 256)

    def epilogue(i, carry):
        r0 = pl.multiple_of(i * rb, rb)
        y = yacc[pl.ds(r0, rb), :]
        yn = y * lax.rsqrt(_head_sum(y * y, ones_rep) * (1.0 / HD) + EPS) * ng + bonus[pl.ds(r0, rb), :]
        y_ref[pl.ds(r0, rb), :] = (yn * _silu(rz_ref[pl.ds(r0, rb), :])).astype(y_ref.dtype)
        return carry

    lax.fori_loop(0, t // rb, epilogue, 0)


def _rwkv(rs, rz, mu, w0, a0, w2bd, a2bd, kk, ka, rk, norm_g, state_in, want_state, grid_shift):
    bsz, t, _ = rs.shape
    big = t > 512
    s_shape = (2, N_HEADS, HD, HD)
    in_specs = [_seq_spec(t, R_SHIFT, big), _seq_spec(t, HW, big),
                _full_spec((1, R_SHIFT)), _full_spec((1, 2 * HW)), _full_spec((1, 2 * HW)),
                _full_spec((2 * LORA, 2 * HW)), _full_spec((2 * LORA, 2 * HW)),
                _full_spec((1, HW)), _full_spec((1, HW)), _full_spec((1, HW)), _full_spec((1, HW))]
    args = [rs, rz, mu, w0, a0, w2bd, a2bd, kk, ka, rk, norm_g]
    if state_in is not None:
        in_specs.append(_state_spec(s_shape))
        args.append(state_in)
    out_specs = [pl.BlockSpec((None, t, HW), lambda b: (b, 0, 0))]
    out_shape = [jax.ShapeDtypeStruct((bsz, t, HW), BF16)]
    if want_state:
        out_specs.append(_state_spec(s_shape))
        out_shape.append(jax.ShapeDtypeStruct((bsz,) + s_shape, F32))
    outs = pl.pallas_call(
        functools.partial(_rwkv_kernel, grid_shift=grid_shift, has_state_in=state_in is not None,
                          has_state_out=want_state),
        grid=(bsz,),
        in_specs=in_specs,
        out_specs=out_specs,
        out_shape=out_shape,
        scratch_shapes=[pltpu.VMEM((t, HW), F32), pltpu.VMEM((t, HW), F32),
                        pltpu.VMEM((2, N_PAIRS, 2 * CHUNK, LANES), F32)],
        compiler_params=pltpu.CompilerParams(dimension_semantics=("arbitrary",),
                                             vmem_limit_bytes=VMEM_LIMIT),
        name="rwkv",
    )(*args)
    return (outs[0], outs[1]) if want_state else (outs[0], None)


def _lru_kernel(xz_ref, conv_ref, cb_ref, w_ref, b_ref, lam_ref, h0_ref, y_ref, hfin_ref, acc, hc):
    t = xz_ref.shape[0]
    nb = t // LRU_BLOCK
    row = _iota2((LRU_BLOCK, L_W), 0)
    conv = conv_ref[...]
    cb = cb_ref[...]
    sp = _softplus(-lam_ref[...])

    acc[...] = jnp.zeros_like(acc)
    hc[...] = h0_ref[...]

    def process(j, d):
        t0 = pl.multiple_of(j * LRU_BLOCK, LRU_BLOCK)
        x = xz_ref[pl.ds(t0, LRU_BLOCK), 0:L_W]
        prev8 = xz_ref[pl.ds(pl.multiple_of(jnp.maximum(t0 - SUBLANES, 0), SUBLANES), SUBLANES), 0:L_W]
        next8 = xz_ref[pl.ds(pl.multiple_of(jnp.minimum(t0 + LRU_BLOCK, t - SUBLANES), SUBLANES), SUBLANES), 0:L_W]
        prev8 = jnp.where(j > 0, prev8, 0.0)
        next8 = jnp.where(j < nb - 1, next8, 0.0)
        xm1 = jnp.where(row == 0, prev8[7:8, :], pltpu.roll(x, 1, 0))
        xm2 = jnp.where(row == 0, prev8[6:7, :], jnp.where(row == 1, prev8[7:8, :], pltpu.roll(x, 2, 0)))
        xp1 = jnp.where(row == LRU_BLOCK - 1, next8[0:1, :], pltpu.roll(x, LRU_BLOCK - 1, 0))
        xc = conv[0:1, :] * xm2 + conv[1:2, :] * xm1 + conv[2:3, :] * x + conv[3:4, :] * xp1 + cb
        pre = _bdot(xc, w_ref[:, 2 * d * L_W:2 * (d + 1) * L_W]) + b_ref[:, 2 * d * L_W:2 * (d + 1) * L_W]
        rg = _sigmoid(pre[:, 0:L_W])
        ig = _sigmoid(pre[:, L_W:])
        log_a = -LRU_C * rg * sp[:, d * L_W:(d + 1) * L_W]
        a_cum = jnp.exp(log_a)
        b_cum = jnp.sqrt(jnp.tanh(-log_a) * (1.0 + a_cum * a_cum)) * ig * xc
        k = 1
        while k < LRU_BLOCK:
            if d == 0:
                keep = row >= k
                a_sh = jnp.where(keep, pltpu.roll(a_cum, k, 0), 1.0)
                b_sh = jnp.where(keep, pltpu.roll(b_cum, k, 0), 0.0)
            else:
                keep = row < LRU_BLOCK - k
                a_sh = jnp.where(keep, pltpu.roll(a_cum, LRU_BLOCK - k, 0), 1.0)
                b_sh = jnp.where(keep, pltpu.roll(b_cum, LRU_BLOCK - k, 0), 0.0)
            b_cum = a_cum * b_sh + b_cum
            a_cum = a_cum * a_sh
            k *= 2
        h = b_cum + a_cum * hc[d]
        last = LRU_BLOCK - 1 if d == 0 else 0
        hc[d] = h[last:last + 1, :]
        acc[pl.ds(t0, LRU_BLOCK), :] = acc[pl.ds(t0, LRU_BLOCK), :] + h

    def body(j, carry):
        process(j, 0)
        process(nb - 1 - j, 1)
        return carry

    lax.fori_loop(0, nb, body, 0)
    hfin_ref[...] = hc[...]

    def epilogue(i, carry):
        r0 = pl.multiple_of(i * LRU_BLOCK, LRU_BLOCK)
        z = xz_ref[pl.ds(r0, LRU_BLOCK), L_W:2 * L_W]
        y_ref[pl.ds(r0, LRU_BLOCK), :] = (acc[pl.ds(r0, LRU_BLOCK), :] * _silu(z)).astype(y_ref.dtype)
        return carry

    lax.fori_loop(0, nb, epilogue, 0)


def _lru(xz, conv, conv_b, wbd, bias, lam, h0):
    bsz, t, _ = xz.shape
    st = lambda shape: pl.BlockSpec((None,) + shape, lambda b: (b,) + (0,) * len(shape))
    return pl.pallas_call(
        _lru_kernel,
        grid=(bsz,),
        in_specs=[_seq_spec(t, 2 * L_W, False), _full_spec((CONV_W, L_W)), _full_spec((1, L_W)),
                  _full_spec((L_W, 4 * L_W)), _full_spec((1, 4 * L_W)), _full_spec((1, 2 * L_W)),
                  st((2, 1, L_W))],
        out_specs=[pl.BlockSpec((None, t, L_W), lambda b: (b, 0, 0)), st((2, 1, L_W))],
        out_shape=[jax.ShapeDtypeStruct((bsz, t, L_W), BF16),
                   jax.ShapeDtypeStruct((bsz, 2, 1, L_W), F32)],
        scratch_shapes=[pltpu.VMEM((t, L_W), F32), pltpu.VMEM((2, 1, L_W), F32)],
        compiler_params=pltpu.CompilerParams(dimension_semantics=("arbitrary",),
                                             vmem_limit_bytes=VMEM_LIMIT),
        name="lru",
    )(xz, conv, conv_b, wbd, bias, lam, h0)


def _permute_w_in(w_in_l):
    valid = jnp.asarray(_GATE_SRC >= 0)
    gates = w_in_l[:, 5 * HW:M_COLS]
    ge = jnp.where(valid, gates[:, np.maximum(_GATE_SRC, 0)], 0.0)
    go = jnp.where(valid, gates[:, np.maximum(_GATE_SRC, 0) + 1], 0.0)
    return jnp.concatenate([w_in_l[:, 0:5 * HW], ge, go, w_in_l[:, M_COLS:]], axis=1).astype(BF16)


def _gate_source_index():
    src = np.full((LANES,), -1, np.int32)
    for dd in range(2):
        for is_f in range(2):
            for p in range(N_PAIRS):
                src[_gate_col(dd, is_f, p)] = is_f * 2 * N_HEADS + dd * N_HEADS + 2 * p
    return src


_GATE_SRC = _gate_source_index()


def _gate_bias(m_bi_l, m_bf_l):
    flat = jnp.concatenate([m_bi_l.reshape(-1), m_bf_l.reshape(-1)])
    idx = np.maximum(_GATE_SRC, 0)
    valid = jnp.asarray(_GATE_SRC >= 0)
    return jnp.stack([jnp.where(valid, flat[idx], 0.0), jnp.where(valid, flat[idx + 1], 0.0)])


def _block_diag(blocks):
    n, a, b = blocks.shape
    eye = jnp.eye(n, dtype=blocks.dtype)
    return (eye[:, None, :, None] * blocks[:, :, None, :]).reshape(n * a, n * b)


def _layer_params(l, g_pre, g_post, w_in, w_out, m_bi, m_bf, m_norm, r_mu, r_w0, r_w2, r_a0, r_a2,
                  r_kk, r_ka, r_rk, r_norm, l_conv, l_conv_b, l_wa, l_ba, l_wx, l_bx, l_lambda):
    row = lambda v: v.reshape(1, -1)
    return dict(
        g_pre=row(g_pre[l]), g_post=row(g_post[l]),
        w_in=_permute_w_in(w_in[l]), w_out=w_out[l].astype(BF16),
        gate_bias=_gate_bias(m_bi[l], m_bf[l]), m_norm=row(m_norm[l]),
        r_mu=row(r_mu[l]), r_w0=row(r_w0[l]), r_a0=row(r_a0[l]),
        r_w2=_block_diag(r_w2[l]).astype(BF16), r_a2=_block_diag(r_a2[l]).astype(BF16),
        r_kk=row(r_kk[l]), r_ka=row(r_ka[l]), r_rk=row(r_rk[l]), r_norm=row(r_norm[l]),
        l_conv=l_conv[l], l_conv_b=row(l_conv_b[l]),
        l_w=jnp.concatenate([_block_diag(l_wa[l][0]), _block_diag(l_wx[l][0]),
                             _block_diag(l_wa[l][1]), _block_diag(l_wx[l][1])], axis=1).astype(BF16),
        l_b=jnp.concatenate([l_ba[l][0], l_bx[l][0], l_ba[l][1], l_bx[l][1]]).reshape(1, -1),
        l_lambda=row(l_lambda[l]),
    )


def _trunk_layer(x, mod, lp, states, want_state, per_seq_mod, grid_shift):
    bsz = x.shape[0]
    qkv, oz, ge, go, rs, rz, lxz = _inproj(x, mod, lp['g_pre'], lp['w_in'], per_seq_mod)
    m_in = None if states is None else states[0:3]
    r_in = None if states is None else states[3]
    l_in = jnp.zeros((bsz, 2, 1, L_W), F32) if states is None else states[4][:, :, None, :]
    y_m, m_out = _mlstm(qkv, oz, ge, go, lp['gate_bias'], lp['m_norm'], m_in, want_state)
    y_r, r_out = _rwkv(rs, rz, lp['r_mu'], lp['r_w0'], lp['r_a0'], lp['r_w2'], lp['r_a2'], lp['r_kk'],
                       lp['r_ka'], lp['r_rk'], lp['r_norm'], r_in, want_state, grid_shift)
    y_l, l_out = _lru(lxz, lp['l_conv'], lp['l_conv_b'], lp['l_w'], lp['l_b'], lp['l_lambda'], l_in)
    y = _outproj(y_m, y_r, y_l, x, mod, lp['g_post'], lp['w_out'], per_seq_mod)
    new_states = m_out + (r_out, l_out[:, :, 0, :]) if want_state else None
    return y, new_states


def kernel(x_prompt, x_sample, c, state_mlstm_C, state_mlstm_n, state_mlstm_m, state_rwkv, state_rglru, c_ctx, g_pre, g_post, w_mod, b_mod, w_in, w_out, m_bi, m_bf, m_norm, r_mu, r_w0, r_w2, r_a0, r_a2, r_kk, r_ka, r_rk, r_norm, l_conv, l_conv_b, l_wa, l_ba, l_wx, l_bx, l_lambda):
    depth = w_in.shape[0]
    bp = x_prompt.shape[0]
    bs = x_sample.shape[0]
    d = x_prompt.shape[-1]
    rows = -(-(1 + bs) // SUBLANES) * SUBLANES
    cc = jnp.zeros((rows, d), F32).at[0].set(c_ctx).at[1:1 + bs].set(c)
    mod = _modulation(cc, w_mod, b_mod)
    xp, xs = x_prompt, x_sample
    new_states = []
    for l in range(depth):
        lp = _layer_params(l, g_pre, g_post, w_in, w_out, m_bi, m_bf, m_norm, r_mu, r_w0, r_w2, r_a0,
                           r_a2, r_kk, r_ka, r_rk, r_norm, l_conv, l_conv_b, l_wa, l_ba, l_wx, l_bx,
                           l_lambda)
        xp, st = _trunk_layer(xp, mod[l, 0:1][:, None, :], lp, None, True, False, False)
        new_states.append(st)
        cache = (state_mlstm_C[:, l].astype(F32), state_mlstm_n[:, l].astype(F32),
                 state_mlstm_m[:, l].astype(F32), state_rwkv[:, l].astype(F32),
                 state_rglru[:, l].astype(F32))
        xs, _ = _trunk_layer(xs, mod[l, 1:1 + bs][:, None, :], lp, cache, False, True, True)
    stacked = tuple(jnp.stack([st[i] for st in new_states], axis=1) for i in range(5))
    return (xp, xs) + stacked
```

```python
import functools

import numpy as np
import jax
import jax.numpy as jnp
from jax import lax
from jax.experimental import pallas as pl
from jax.experimental.pallas import tpu as pltpu

F32 = jnp.float32
BF16 = jnp.bfloat16
HIGHEST = lax.Precision.HIGHEST

D_MODEL = 1024
EPS = 1e-6
HD = 64
N_HEADS = 6
N_PAIRS = N_HEADS // 2
HW = N_HEADS * HD
CHUNK = 64
GRID_W = 64
LORA = 64
R_SHIFT = 3 * HW + 4 * LORA
L_W = 256
L_BLOCKS = 4
CONV_W = 4
LRU_C = 8.0
RWKV_DECAY_SCALE = 0.6065306597126334
M_COLS = 5 * HW + 4 * N_HEADS
IN_COLS = M_COLS + R_SHIFT + HW + 2 * L_W

LANES = 128
SUBLANES = 8
HALO = 16
LRU_BLOCK = 128
VMEM_LIMIT = 56 * 1024 * 1024
MIXER_VMEM_BUDGET = 36 * 1024 * 1024

SEG_WIDTHS = (3 * HW, 2 * HW, LANES, LANES, R_SHIFT, HW, 2 * L_W)
SEG_DTYPES = (BF16, BF16, F32, F32, BF16, BF16, BF16)
SEG_OFFS = tuple(int(v) for v in np.cumsum((0,) + SEG_WIDTHS))
IN_COLS_PAD = SEG_OFFS[-1]
N_TILE = 512


def _gate_col(d, is_f, p):
    return 8 * d + 4 * is_f + p


def _bdot(a, b):
    return jnp.dot(a.astype(BF16), b.astype(BF16), preferred_element_type=F32)


def _bdot_nt(a, b):
    return lax.dot_general(a.astype(BF16), b.astype(BF16), (((1,), (1,)), ((), ())),
                           preferred_element_type=F32)


def _bdot_tn(a, b):
    return lax.dot_general(a.astype(BF16), b.astype(BF16), (((0,), (0,)), ((), ())),
                           preferred_element_type=F32)


def _fdot(a, b):
    return jnp.dot(a, b, precision=HIGHEST, preferred_element_type=F32)


def _split_bf16(x, terms):
    out = []
    for _ in range(terms - 1):
        hi = x.astype(BF16)
        out.append(hi)
        x = x - hi.astype(F32)
    out.append(x.astype(BF16))
    return out


def _cum_dot(cum_rep, x, terms=3):
    return jnp.dot(cum_rep, jnp.concatenate(_split_bf16(x, terms), axis=0), preferred_element_type=F32)


def _head_sum(x, ones_rep, terms=2):
    outs = []
    for p in range(x.shape[1] // LANES):
        parts = _split_bf16(x[:, p * LANES:(p + 1) * LANES], terms)
        outs.append(jnp.dot(jnp.concatenate(parts, axis=1), ones_rep, preferred_element_type=F32))
    return outs[0] if len(outs) == 1 else jnp.concatenate(outs, axis=1)


def _sigmoid(x):
    return 1.0 / (1.0 + jnp.exp(-x))


def _silu(x):
    return x * _sigmoid(x)


def _softplus(x):
    return jnp.maximum(x, 0.0) + jnp.log1p(jnp.exp(-jnp.abs(x)))


def _log_sigmoid(x):
    return -_softplus(-x)


def _iota2(shape, dim):
    return lax.broadcasted_iota(jnp.int32, shape, dim)


def _head_ones(n):
    return (_iota2((n, n), 0) // HD == _iota2((n, n), 1) // HD).astype(F32)


def _bd_stack(x, lo):
    return jnp.concatenate([jnp.where(lo, x, 0.0), jnp.where(lo, 0.0, x)], axis=0)


def _pair_col(q, j, lo):
    return jnp.where(lo, q[0:CHUNK, j:j + 1], q[CHUNK:2 * CHUNK, j:j + 1])


def _mod_kernel(c_ref, w_ref, b_ref, o_ref):
    o_ref[...] = _fdot(_silu(c_ref[...]), w_ref[...]) + b_ref[...]


def _modulation(cc, w_mod, b_mod):
    depth, d, n = w_mod.shape
    rows = cc.shape[0]
    return pl.pallas_call(
        _mod_kernel,
        grid=(depth, n // N_TILE),
        in_specs=[pl.BlockSpec((rows, d), lambda l, j: (0, 0)),
                  pl.BlockSpec((None, d, N_TILE), lambda l, j: (l, 0, j)),
                  pl.BlockSpec((None, 1, N_TILE), lambda l, j: (l, 0, j))],
        out_specs=pl.BlockSpec((None, rows, N_TILE), lambda l, j: (l, 0, j)),
        out_shape=jax.ShapeDtypeStruct((depth, rows, n), F32),
        compiler_params=pltpu.CompilerParams(dimension_semantics=("arbitrary", "arbitrary")),
        name="modulation",
    )(cc, w_mod, b_mod.reshape(depth, 1, n))


def _inproj_kernel(x_ref, mod_ref, g_ref, w_ref, *out_refs):
    x = x_ref[...]
    d = x.shape[-1]
    mod = mod_ref[...]
    h = x * lax.rsqrt(jnp.mean(x * x, axis=-1, keepdims=True) + EPS) * g_ref[...]
    h = (h * (1.0 + mod[:, d:2 * d]) + mod[:, 0:d]).astype(BF16)
    for o_ref, a, b in zip(out_refs, SEG_OFFS[:-1], SEG_OFFS[1:]):
        for n0 in range(a, b, N_TILE):
            n1 = min(n0 + N_TILE, b)
            o_ref[:, n0 - a:n1 - a] = jnp.dot(h, w_ref[:, n0:n1],
                                              preferred_element_type=F32).astype(o_ref.dtype)


def _inproj(x, mod, g_pre, w_in_p, per_seq_mod):
    bsz, t, d = x.shape
    tm = min(t, 512)
    mod_idx = (lambda b, i: (b, 0, 0)) if per_seq_mod else (lambda b, i: (0, 0, 0))
    return pl.pallas_call(
        _inproj_kernel,
        grid=(bsz, t // tm),
        in_specs=[pl.BlockSpec((None, tm, d), lambda b, i: (b, i, 0)),
                  pl.BlockSpec((None, 1, 3 * d), mod_idx),
                  pl.BlockSpec((1, d), lambda b, i: (0, 0)),
                  pl.BlockSpec((d, IN_COLS_PAD), lambda b, i: (0, 0), pipeline_mode=pl.Buffered(1))],
        out_specs=[pl.BlockSpec((None, tm, w), lambda b, i: (b, i, 0)) for w in SEG_WIDTHS],
        out_shape=[jax.ShapeDtypeStruct((bsz, t, w), dt) for w, dt in zip(SEG_WIDTHS, SEG_DTYPES)],
        compiler_params=pltpu.CompilerParams(dimension_semantics=("arbitrary", "arbitrary"),
                                             vmem_limit_bytes=VMEM_LIMIT),
        name="inproj",
    )(x, mod, g_pre, w_in_p)


def _outproj_kernel(ym_ref, yr_ref, yl_ref, x_ref, mod_ref, g_ref, w_ref, o_ref):
    d = x_ref.shape[-1]
    o = jnp.dot(ym_ref[...], w_ref[0:HW, :], preferred_element_type=F32)
    o = o + jnp.dot(yr_ref[...], w_ref[HW:2 * HW, :], preferred_element_type=F32)
    o = o + jnp.dot(yl_ref[...], w_ref[2 * HW:2 * HW + L_W, :], preferred_element_type=F32)
    on = o * lax.rsqrt(jnp.mean(o * o, axis=-1, keepdims=True) + EPS) * g_ref[...]
    o_ref[...] = x_ref[...] + mod_ref[:, 2 * d:3 * d] * on


def _outproj(ym, yr, yl, x, mod, g_post, w_out_b, per_seq_mod):
    bsz, t, d = x.shape
    tm = min(t, 512)
    mod_idx = (lambda b, i: (b, 0, 0)) if per_seq_mod else (lambda b, i: (0, 0, 0))
    tok = lambda w: pl.BlockSpec((None, tm, w), lambda b, i: (b, i, 0))
    return pl.pallas_call(
        _outproj_kernel,
        grid=(bsz, t // tm),
        in_specs=[tok(HW), tok(HW), tok(L_W), tok(d),
                  pl.BlockSpec((None, 1, 3 * d), mod_idx),
                  pl.BlockSpec((1, d), lambda b, i: (0, 0)),
                  pl.BlockSpec((2 * HW + L_W, d), lambda b, i: (0, 0))],
        out_specs=tok(d),
        out_shape=jax.ShapeDtypeStruct((bsz, t, d), F32),
        compiler_params=pltpu.CompilerParams(dimension_semantics=("arbitrary", "arbitrary"),
                                             vmem_limit_bytes=VMEM_LIMIT),
        name="outproj",
    )(ym, yr, yl, x, mod, g_post, w_out_b)


def _mlstm_kernel(*refs, has_state_in, has_state_out):
    qkv_ref, oz_ref, ge_ref, go_ref, gb_ref, ng_ref = refs[0:6]
    refs = refs[6:]
    if has_state_in:
        c0_ref, n0_ref, m0_ref = refs[0:3]
        refs = refs[3:]
    y_ref = refs[0]
    refs = refs[1:]
    if has_state_out:
        c_out_ref, n_out_ref, m_out_ref = refs[0:3]
        refs = refs[3:]
    hacc, cn_s, m_s = refs
    ns, t = qkv_ref.shape[0], qkv_ref.shape[1]
    nc = t // CHUNK
    lane = _iota2((CHUNK, LANES), 1)
    row = _iota2((CHUNK, LANES), 0)
    lo = lane < HD
    s_idx = lane % HD
    causal = (s_idx <= row, s_idx >= row)
    r2 = _iota2((2 * CHUNK, 2 * CHUNK), 0)
    c2 = _iota2((2 * CHUNK, 2 * CHUNK), 1)
    same = (r2 // CHUNK) == (c2 // CHUNK)
    cum = ((same & (c2 <= r2)).astype(BF16), (same & (c2 >= r2)).astype(BF16))
    cum_rep = tuple(jnp.concatenate([m, m, m], axis=1) for m in cum)
    bd_ones = same.astype(F32)
    ones_rep = jnp.concatenate([same.astype(BF16), same.astype(BF16)], axis=0)
    bd_mask2 = jnp.concatenate([same, same], axis=1)
    colid = _iota2((1, LANES), 1)
    is_f = (colid % 8) >= 4
    gb = gb_ref[...]

    hacc[...] = jnp.zeros_like(hacc)
    cn_s[...] = jnp.zeros_like(cn_s)
    m_s[...] = jnp.zeros_like(m_s)
    if has_state_in:
        m_s[...] = m0_ref[...]
        for s in range(ns):
            for d in range(2):
                for p in range(N_PAIRS):
                    for h in range(2):
                        cn_s[s, d, p, h * HD:(h + 1) * HD, h * HD:(h + 1) * HD] = c0_ref[s, d, 2 * p + h]
                    n_rows = jnp.where(same, jnp.broadcast_to(n0_ref[s, d, p], (2 * CHUNK, LANES)), 0.0)
                    cn_s[s, d, p, :, LANES:2 * LANES] = n_rows.T

    def gate_table(s, t0, d):
        ve = ge_ref[s, pl.ds(t0, CHUNK), :] + gb[0:1, :]
        vo = go_ref[s, pl.ds(t0, CHUNK), :] + gb[1:2, :]
        ve = jnp.where(is_f, _log_sigmoid(ve), ve)
        vo = jnp.where(is_f, _log_sigmoid(vo), vo)
        x = jnp.concatenate([ve, vo], axis=0)
        q = jnp.where(is_f, _cum_dot(cum_rep[d], x), x)
        return q, q.T

    chains = [(s, d, p) for s in range(ns) for d in range(2) for p in range(N_PAIRS)]

    def body(j, carry):
        t0s = (pl.multiple_of(j * CHUNK, CHUNK), pl.multiple_of((nc - 1 - j) * CHUNK, CHUNK))
        tabs = {(s, d): gate_table(s, t0s[d], d) for s in range(ns) for d in range(2)}
        st = []
        for s, d, p in chains:
            t0 = t0s[d]
            q_tab, q_tab_t = tabs[(s, d)]
            ji, jf = _gate_col(d, 0, p), _gate_col(d, 1, p)
            c = dict(t0=t0, sl=slice(p * LANES, (p + 1) * LANES))
            c['bcol'] = _pair_col(q_tab, jf, lo)
            c['icol'] = _pair_col(q_tab, ji, lo)
            brow = q_tab_t[jf:jf + 1, :]
            irow = q_tab_t[ji:ji + 1, :]
            c['q'] = qkv_ref[s, pl.ds(t0, CHUNK), p * LANES:(p + 1) * LANES].astype(F32)
            c['k'] = qkv_ref[s, pl.ds(t0, CHUNK), HW + p * LANES:HW + (p + 1) * LANES].astype(F32) * (HD ** -0.5)
            c['v'] = qkv_ref[s, pl.ds(t0, CHUNK), 2 * HW + p * LANES:2 * HW + (p + 1) * LANES].astype(F32)
            c['mprev'] = m_s[s, d, p]
            c['cn'] = cn_s[s, d, p]
            dmat = jnp.where(causal[d], c['bcol'] - brow + irow, -jnp.inf)
            inter = c['bcol'] + c['mprev']
            mx = jnp.where(lo,
                           jnp.max(jnp.where(lo, dmat, -jnp.inf), axis=1, keepdims=True),
                           jnp.max(jnp.where(lo, -jnp.inf, dmat), axis=1, keepdims=True))
            c['m_t'] = jnp.maximum(inter, mx)
            c['pexp'] = jnp.exp(dmat - c['m_t'])
            c['sc'] = jnp.exp(inter - c['m_t'])
            last = CHUNK - 1 if d == 0 else 0
            b_last = c['bcol'][last:last + 1, :]
            g = b_last - c['bcol'] + c['icol']
            c['m_new'] = jnp.maximum(b_last + c['mprev'], jnp.max(g, axis=0, keepdims=True))
            c['dec'] = jnp.exp(b_last + c['mprev'] - c['m_new'])
            c['wk'] = jnp.exp(g - c['m_new'])
            st.append(c)
        for c in st:
            c['s'] = _bdot_nt(c['q'], _bd_stack(c['k'], lo)) * c['pexp']
        for c in st:
            c['a1'] = _bdot(c['q'], c['cn'])
        for c in st:
            c['upd'] = _bdot_tn(c['wk'] * c['k'], jnp.concatenate([c['v'], jnp.ones_like(c['v'])], axis=1))
        for c in st:
            c['a2'] = _bdot(c['s'], jnp.concatenate([_bd_stack(c['v'], lo), bd_ones], axis=1))
        for (s, d, p), c in zip(chains, st):
            num = c['sc'] * c['a1'][:, 0:LANES] + c['a2'][:, 0:LANES]
            den = c['sc'] * c['a1'][:, LANES:] + c['a2'][:, LANES:]
            h = num / jnp.maximum(jnp.abs(den), jnp.exp(-c['m_t']))
            hacc[s, pl.ds(c['t0'], CHUNK), c['sl']] = hacc[s, pl.ds(c['t0'], CHUNK), c['sl']] + h
            cn_s[s, d, p] = (jnp.concatenate([c['dec'], c['dec']], axis=1) * c['cn']
                             + jnp.where(bd_mask2, c['upd'], 0.0))
            m_s[s, d, p] = c['m_new']
        return carry

    lax.fori_loop(0, nc, body, 0)
    if has_state_out:
        m_out_ref[...] = m_s[...]
        lo2 = _iota2((1, LANES), 1) < HD
        for s in range(ns):
            for d in range(2):
                for p in range(N_PAIRS):
                    for h in range(2):
                        c_out_ref[s, d, 2 * p + h] = cn_s[s, d, p, h * HD:(h + 1) * HD, h * HD:(h + 1) * HD]
                    n_t = cn_s[s, d, p, :, LANES:2 * LANES].T
                    n_out_ref[s, d, p] = jnp.where(lo2, n_t[0:1, :], n_t[HD:HD + 1, :])

    ng = ng_ref[...]
    rb = min(t, 256)

    def epilogue(i, carry):
        r0 = pl.multiple_of(i * rb, rb)
        for s in range(ns):
            h = hacc[s, pl.ds(r0, rb), :]
            hn = h * lax.rsqrt(_head_sum(h * h, ones_rep) * (1.0 / HD) + EPS) * ng
            o = oz_ref[s, pl.ds(r0, rb), 0:HW].astype(F32)
            z = oz_ref[s, pl.ds(r0, rb), HW:2 * HW].astype(F32)
            y_ref[s, pl.ds(r0, rb), :] = (hn * _sigmoid(o) * _silu(z)).astype(y_ref.dtype)
        return carry

    lax.fori_loop(0, t // rb, epilogue, 0)


def _seq_spec(ns, t, w, single_buffer):
    if single_buffer:
        return pl.BlockSpec((ns, t, w), lambda b: (b, 0, 0), pipeline_mode=pl.Buffered(1))
    return pl.BlockSpec((ns, t, w), lambda b: (b, 0, 0))


def _full_spec(shape):
    n = len(shape)
    return pl.BlockSpec(shape, lambda b: (0,) * n)


def _state_spec(ns, shape):
    return pl.BlockSpec((ns,) + shape, lambda b: (b,) + (0,) * len(shape))


def _seqs_per_step(bsz, bytes_per_seq):
    for ns in (2, 1):
        if bsz % ns == 0 and ns * bytes_per_seq <= MIXER_VMEM_BUDGET:
            return ns
    return 1


def _pair_rows(x):
    return jnp.repeat(x.reshape(x.shape[:2] + (N_PAIRS, 1, 2)), HD, axis=-1)


def _mlstm(qkv, oz, ge, go, gate_bias, norm_g, state_in, want_state):
    bsz, t, _ = qkv.shape
    big = t > 512
    in_bufs = 1 if big else 2
    ns = _seqs_per_step(bsz, t * ((5 * HW * qkv.dtype.itemsize + 2 * LANES * 4) * in_bufs
                                  + HW * 2 * 2 + HW * 4))
    c_shape, row_shape = (2, N_HEADS, HD, HD), (2, N_PAIRS, 1, LANES)
    in_specs = [_seq_spec(ns, t, 3 * HW, big), _seq_spec(ns, t, 2 * HW, big),
                _seq_spec(ns, t, LANES, big), _seq_spec(ns, t, LANES, big),
                _full_spec((2, LANES)), _full_spec((1, HW))]
    args = [qkv, oz, ge, go, gate_bias, norm_g]
    if state_in is not None:
        c0, n0, m0 = state_in
        in_specs += [_state_spec(ns, c_shape), _state_spec(ns, row_shape), _state_spec(ns, row_shape)]
        args += [c0, n0.reshape((bsz,) + row_shape), _pair_rows(m0)]
    out_specs = [_seq_spec(ns, t, HW, False)]
    out_shape = [jax.ShapeDtypeStruct((bsz, t, HW), BF16)]
    if want_state:
        out_specs += [_state_spec(ns, c_shape), _state_spec(ns, row_shape), _state_spec(ns, row_shape)]
        out_shape += [jax.ShapeDtypeStruct((bsz,) + c_shape, F32),
                      jax.ShapeDtypeStruct((bsz,) + row_shape, F32),
                      jax.ShapeDtypeStruct((bsz,) + row_shape, F32)]
    outs = pl.pallas_call(
        functools.partial(_mlstm_kernel, has_state_in=state_in is not None, has_state_out=want_state),
        grid=(bsz // ns,),
        in_specs=in_specs,
        out_specs=out_specs,
        out_shape=out_shape,
        scratch_shapes=[pltpu.VMEM((ns, t, HW), F32),
                        pltpu.VMEM((ns, 2, N_PAIRS, 2 * CHUNK, 2 * LANES), F32),
                        pltpu.VMEM((ns,) + row_shape, F32)],
        compiler_params=pltpu.CompilerParams(dimension_semantics=("arbitrary",),
                                             vmem_limit_bytes=VMEM_LIMIT),
        name="mlstm",
    )(*args)
    if not want_state:
        return outs[0], None
    y, c1, n1, m1 = outs
    return y, (c1, n1.reshape(bsz, 2, N_HEADS, HD), m1[:, :, :, 0, ::HD].reshape(bsz, 2, N_HEADS))


def _rwkv_kernel(*refs, grid_shift, has_state_in, has_state_out):
    (rs_ref, rz_ref, mu_ref, w0_ref, a0_ref, w2_ref, a2_ref, kk_ref, ka_ref, rk_ref, ng_ref) = refs[0:11]
    refs = refs[11:]
    if has_state_in:
        s0_ref = refs[0]
        refs = refs[1:]
    y_ref = refs[0]
    refs = refs[1:]
    if has_state_out:
        s_out_ref = refs[0]
        refs = refs[1:]
    yacc, bonus, s_s = refs
    ns, t = rs_ref.shape[0], rs_ref.shape[1]
    nc = t // CHUNK
    lane = _iota2((CHUNK, LANES), 1)
    row = _iota2((CHUNK, LANES), 0)
    lo = lane < HD
    s_idx = lane % HD
    strict = (s_idx < row, s_idx > row)
    incl = (s_idx <= row, s_idx >= row)
    eye_pk = (s_idx == row).astype(F32)
    r64 = _iota2((CHUNK, CHUNK), 0)
    c64 = _iota2((CHUNK, CHUNK), 1)
    cum = ((c64 <= r64).astype(BF16), (c64 >= r64).astype(BF16))
    cum_rep = tuple(jnp.concatenate([m, m, m], axis=1) for m in cum)
    r2 = _iota2((2 * CHUNK, 2 * CHUNK), 0)
    c2 = _iota2((2 * CHUNK, 2 * CHUNK), 1)
    bd_mask = (r2 // CHUNK) == (c2 // CHUNK)
    ones_rep = jnp.concatenate([bd_mask.astype(BF16), bd_mask.astype(BF16)], axis=0)
    roww = _iota2((CHUNK, R_SHIFT), 0)
    mu = mu_ref[...]
    kk = kk_ref[...]
    ka = ka_ref[...]
    rk = rk_ref[...]

    yacc[...] = jnp.zeros_like(yacc)
    s_s[...] = jnp.zeros_like(s_s)
    if has_state_in:
        for s in range(ns):
            for d in range(2):
                for p in range(N_PAIRS):
                    for h in range(2):
                        s_s[s, d, p, h * HD:(h + 1) * HD, h * HD:(h + 1) * HD] = s0_ref[s, d, 2 * p + h]

    def shifted(x, s, c, t0):
        left = jnp.where(roww == 0, 0.0, pltpu.roll(x, 1, 0))
        right = jnp.where(roww == CHUNK - 1, 0.0, pltpu.roll(x, CHUNK - 1, 0))
        has_prev = c > 0
        has_next = c < nc - 1
        if grid_shift:
            up = rs_ref[s, pl.ds(pl.multiple_of(jnp.maximum(t0 - GRID_W, 0), CHUNK), CHUNK), :]
            down = rs_ref[s, pl.ds(pl.multiple_of(jnp.minimum(t0 + GRID_W, t - CHUNK), CHUNK), CHUNK), :]
            up = jnp.where(has_prev, up.astype(F32), 0.0)
            down = jnp.where(has_next, down.astype(F32), 0.0)
            return 0.25 * (up + down + left + right)
        prev_rows = rs_ref[s, pl.ds(pl.multiple_of(jnp.maximum(t0 - HALO, 0), HALO), HALO), :]
        next_rows = rs_ref[s, pl.ds(pl.multiple_of(jnp.minimum(t0 + CHUNK, t - HALO), HALO), HALO), :]
        prev_row = jnp.where(has_prev, prev_rows[HALO - 1:HALO, :].astype(F32), 0.0)
        next_row = jnp.where(has_next, next_rows[0:1, :].astype(F32), 0.0)
        left = jnp.where(roww == 0, prev_row, left)
        right = jnp.where(roww == CHUNK - 1, next_row, right)
        return 0.5 * (left + right)

    def prep(s, c, d):
        t0 = pl.multiple_of(c * CHUNK, CHUNK)
        x = rs_ref[s, pl.ds(t0, CHUNK), :].astype(F32)
        blk = x + mu * (shifted(x, s, c, t0) - x)
        r = blk[:, 0:HW]
        k = blk[:, HW:2 * HW]
        v = blk[:, 2 * HW:3 * HW]
        wl = blk[:, 3 * HW:3 * HW + 2 * LORA]
        al = blk[:, 3 * HW + 2 * LORA:3 * HW + 4 * LORA]
        dsl = slice(d * HW, (d + 1) * HW)
        logw = -RWKV_DECAY_SCALE * _sigmoid(w0_ref[:, dsl] + _bdot(jnp.tanh(wl), w2_ref[:, dsl]))
        a = _sigmoid(a0_ref[:, dsl] + _bdot(al, a2_ref[:, dsl]))
        kappa = k * kk
        khat = kappa / jnp.maximum(jnp.sqrt(_head_sum(kappa * kappa, ones_rep)), 1e-12)
        kt = k * (1.0 + (a - 1.0) * ka)
        b = khat * a
        if d == 0:
            bonus[s, pl.ds(t0, CHUNK), :] = _head_sum(r * k * rk, ones_rep) * v
        last = CHUNK - 1 if d == 0 else 0
        logp = _cum_dot(cum_rep[d], logw)
        logp_last = logp[last:last + 1, :]
        e_neg = jnp.exp(-logp)
        e_last = jnp.exp(logp_last - logp)
        return dict(t0=t0, v=v, alpha=jnp.exp(logp - logw) * khat, beta=b * e_neg, kap=kt * e_neg,
                    rho=r * jnp.exp(logp), beta_l=b * e_last, kap_l=kt * e_last,
                    p_last=jnp.exp(logp_last))

    chains = [(s, d, p) for s in range(ns) for d in range(2) for p in range(N_PAIRS)]

    def body(j, carry):
        pre = {(s, d): prep(s, j if d == 0 else nc - 1 - j, d) for s in range(ns) for d in range(2)}
        st = []
        for s, d, p in chains:
            sl = slice(p * LANES, (p + 1) * LANES)
            c = {key: (val if key == 't0' else val[:, sl]) for key, val in pre[(s, d)].items()}
            c['sl'] = sl
            st.append(c)
        for c in st:
            c['g'] = _bdot_nt(jnp.concatenate([c['alpha'], c['rho']], axis=0),
                              jnp.concatenate([_bd_stack(c['beta'], lo), _bd_stack(c['kap'], lo)], axis=0))
        for (s, d, p), c in zip(chains, st):
            g = c.pop('g')
            c['n'] = jnp.where(strict[d], g[0:CHUNK, 0:LANES], 0.0)
            c['mk'] = jnp.where(strict[d], g[0:CHUNK, LANES:], 0.0)
            c['mrb'] = jnp.where(incl[d], g[CHUNK:, 0:LANES], 0.0)
            c['mrk'] = jnp.where(incl[d], g[CHUNK:, LANES:], 0.0)
            c['inv'] = eye_pk - c['n']
        for c in st:
            c['pw'] = _bdot(c['n'], _bd_stack(c['n'], lo))
        for c in st:
            c['mkv'] = _bdot(c['mk'], _bd_stack(c['v'], lo))
        for _ in range(4):
            for c in st:
                rp = _bdot(jnp.concatenate([c['inv'], c['pw']], axis=0), _bd_stack(c['pw'], lo))
                c['inv'] = c['inv'] + rp[0:CHUNK]
                c['pw'] = rp[CHUNK:]
        for c in st:
            c['inv'] = c['inv'] + _bdot(c['inv'], _bd_stack(c['pw'], lo))
        for c in st:
            c['wu'] = _bdot(c['inv'], jnp.concatenate([_bd_stack(c['alpha'], lo), _bd_stack(c['mkv'], lo)], axis=1))
        for (s, d, p), c in zip(chains, st):
            c['s_prev'] = s_s[s, d, p]
            ws = _bdot_nt(jnp.concatenate([c['wu'][:, 0:LANES], c['rho']], axis=0), c['s_prev'])
            c['u'] = ws[0:CHUNK] + c['wu'][:, LANES:]
            c['ys'] = ws[CHUNK:]
        for c in st:
            c['upd'] = _bdot_tn(jnp.concatenate([c['v'], -c['u']], axis=0),
                                jnp.concatenate([c['kap_l'], c['beta_l']], axis=0))
        for c in st:
            c['y'] = c['ys'] + _bdot(jnp.concatenate([c['mrk'], -c['mrb']], axis=1),
                                     jnp.concatenate([_bd_stack(c['v'], lo), _bd_stack(c['u'], lo)], axis=0))
        for (s, d, p), c in zip(chains, st):
            s_s[s, d, p] = c['s_prev'] * c['p_last'] + jnp.where(bd_mask, c['upd'], 0.0)
            yacc[s, pl.ds(c['t0'], CHUNK), c['sl']] = yacc[s, pl.ds(c['t0'], CHUNK), c['sl']] + c['y']
        return carry

    lax.fori_loop(0, nc, body, 0)
    if has_state_out:
        for s in range(ns):
            for d in range(2):
                for p in range(N_PAIRS):
                    for h in range(2):
                        s_out_ref[s, d, 2 * p + h] = s_s[s, d, p, h * HD:(h + 1) * HD, h * HD:(h + 1) * HD]

    ng = ng_ref[...]
    rb = min(t, 256)

    def epilogue(i, carry):
        r0 = pl.multiple_of(i * rb, rb)
        for s in range(ns):
            y = yacc[s, pl.ds(r0, rb), :]
            yn = (y * lax.rsqrt(_head_sum(y * y, ones_rep) * (1.0 / HD) + EPS) * ng
                  + bonus[s, pl.ds(r0, rb), :])
            y_ref[s, pl.ds(r0, rb), :] = (yn * _silu(rz_ref[s, pl.ds(r0, rb), :].astype(F32))).astype(y_ref.dtype)
        return carry

    lax.fori_loop(0, t // rb, epilogue, 0)


def _rwkv(rs, rz, mu, w0, a0, w2bd, a2bd, kk, ka, rk, norm_g, state_in, want_state, grid_shift):
    bsz, t, _ = rs.shape
    big = t > 512
    in_bufs = 1 if big else 2
    itemsize = rs.dtype.itemsize
    ns = _seqs_per_step(bsz, t * ((R_SHIFT + HW) * itemsize * in_bufs + HW * 2 * 2 + 2 * HW * 4))
    s_shape = (2, N_HEADS, HD, HD)
    in_specs = [_seq_spec(ns, t, R_SHIFT, big), _seq_spec(ns, t, HW, big),
                _full_spec((1, R_SHIFT)), _full_spec((1, 2 * HW)), _full_spec((1, 2 * HW)),
                _full_spec((2 * LORA, 2 * HW)), _full_spec((2 * LORA, 2 * HW)),
                _full_spec((1, HW)), _full_spec((1, HW)), _full_spec((1, HW)), _full_spec((1, HW))]
    args = [rs, rz, mu, w0, a0, w2bd, a2bd, kk, ka, rk, norm_g]
    if state_in is not None:
        in_specs.append(_state_spec(ns, s_shape))
        args.append(state_in)
    out_specs = [_seq_spec(ns, t, HW, False)]
    out_shape = [jax.ShapeDtypeStruct((bsz, t, HW), BF16)]
    if want_state:
        out_specs.append(_state_spec(ns, s_shape))
        out_shape.append(jax.ShapeDtypeStruct((bsz,) + s_shape, F32))
    outs = pl.pallas_call(
        functools.partial(_rwkv_kernel, grid_shift=grid_shift, has_state_in=state_in is not None,
                          has_state_out=want_state),
        grid=(bsz // ns,),
        in_specs=in_specs,
        out_specs=out_specs,
        out_shape=out_shape,
        scratch_shapes=[pltpu.VMEM((ns, t, HW), F32), pltpu.VMEM((ns, t, HW), F32),
                        pltpu.VMEM((ns, 2, N_PAIRS, 2 * CHUNK, LANES), F32)],
        compiler_params=pltpu.CompilerParams(dimension_semantics=("arbitrary",),
                                             vmem_limit_bytes=VMEM_LIMIT),
        name="rwkv",
    )(*args)
    return (outs[0], outs[1]) if want_state else (outs[0], None)


def _lru_kernel(xz_ref, conv_ref, cb_ref, w_ref, b_ref, lam_ref, h0_ref, y_ref, hfin_ref, acc, hc):
    t = xz_ref.shape[0]
    nb = t // LRU_BLOCK
    row = _iota2((LRU_BLOCK, L_W), 0)
    conv = conv_ref[...]
    cb = cb_ref[...]
    sp = _softplus(-lam_ref[...])

    acc[...] = jnp.zeros_like(acc)
    hc[...] = h0_ref[...]

    def process(j, d):
        t0 = pl.multiple_of(j * LRU_BLOCK, LRU_BLOCK)
        x = xz_ref[pl.ds(t0, LRU_BLOCK), 0:L_W].astype(F32)
        prev_rows = xz_ref[pl.ds(pl.multiple_of(jnp.maximum(t0 - HALO, 0), HALO), HALO), 0:L_W]
        next_rows = xz_ref[pl.ds(pl.multiple_of(jnp.minimum(t0 + LRU_BLOCK, t - HALO), HALO), HALO), 0:L_W]
        prev_rows = jnp.where(j > 0, prev_rows.astype(F32), 0.0)
        next_rows = jnp.where(j < nb - 1, next_rows.astype(F32), 0.0)
        p1, p2 = prev_rows[HALO - 1:HALO, :], prev_rows[HALO - 2:HALO - 1, :]
        xm1 = jnp.where(row == 0, p1, pltpu.roll(x, 1, 0))
        xm2 = jnp.where(row == 0, p2, jnp.where(row == 1, p1, pltpu.roll(x, 2, 0)))
        xp1 = jnp.where(row == LRU_BLOCK - 1, next_rows[0:1, :], pltpu.roll(x, LRU_BLOCK - 1, 0))
        xc = conv[0:1, :] * xm2 + conv[1:2, :] * xm1 + conv[2:3, :] * x + conv[3:4, :] * xp1 + cb
        pre = _bdot(xc, w_ref[:, 2 * d * L_W:2 * (d + 1) * L_W]) + b_ref[:, 2 * d * L_W:2 * (d + 1) * L_W]
        rg = _sigmoid(pre[:, 0:L_W])
        ig = _sigmoid(pre[:, L_W:])
        log_a = -LRU_C * rg * sp[:, d * L_W:(d + 1) * L_W]
        a_cum = jnp.exp(log_a)
        b_cum = jnp.sqrt(jnp.tanh(-log_a) * (1.0 + a_cum * a_cum)) * ig * xc
        k = 1
        while k < LRU_BLOCK:
            if d == 0:
                keep = row >= k
                a_sh = jnp.where(keep, pltpu.roll(a_cum, k, 0), 1.0)
                b_sh = jnp.where(keep, pltpu.roll(b_cum, k, 0), 0.0)
            else:
                keep = row < LRU_BLOCK - k
                a_sh = jnp.where(keep, pltpu.roll(a_cum, LRU_BLOCK - k, 0), 1.0)
                b_sh = jnp.where(keep, pltpu.roll(b_cum, LRU_BLOCK - k, 0), 0.0)
            b_cum = a_cum * b_sh + b_cum
            a_cum = a_cum * a_sh
            k *= 2
        h = b_cum + a_cum * hc[d]
        last = LRU_BLOCK - 1 if d == 0 else 0
        hc[d] = h[last:last + 1, :]
        acc[pl.ds(t0, LRU_BLOCK), :] = acc[pl.ds(t0, LRU_BLOCK), :] + h

    def body(j, carry):
        process(j, 0)
        process(nb - 1 - j, 1)
        return carry

    lax.fori_loop(0, nb, body, 0)
    hfin_ref[...] = hc[...]

    def epilogue(i, carry):
        r0 = pl.multiple_of(i * LRU_BLOCK, LRU_BLOCK)
        z = xz_ref[pl.ds(r0, LRU_BLOCK), L_W:2 * L_W].astype(F32)
        y_ref[pl.ds(r0, LRU_BLOCK), :] = (acc[pl.ds(r0, LRU_BLOCK), :] * _silu(z)).astype(y_ref.dtype)
        return carry

    lax.fori_loop(0, nb, epilogue, 0)


def _lru(xz, conv, conv_b, wbd, bias, lam, h0):
    bsz, t, _ = xz.shape
    st = lambda shape: pl.BlockSpec((None,) + shape, lambda b: (b,) + (0,) * len(shape))
    return pl.pallas_call(
        _lru_kernel,
        grid=(bsz,),
        in_specs=[_seq_spec(None, t, 2 * L_W, False), _full_spec((CONV_W, L_W)), _full_spec((1, L_W)),
                  _full_spec((L_W, 4 * L_W)), _full_spec((1, 4 * L_W)), _full_spec((1, 2 * L_W)),
                  st((2, 1, L_W))],
        out_specs=[pl.BlockSpec((None, t, L_W), lambda b: (b, 0, 0)), st((2, 1, L_W))],
        out_shape=[jax.ShapeDtypeStruct((bsz, t, L_W), BF16),
                   jax.ShapeDtypeStruct((bsz, 2, 1, L_W), F32)],
        scratch_shapes=[pltpu.VMEM((t, L_W), F32), pltpu.VMEM((2, 1, L_W), F32)],
        compiler_params=pltpu.CompilerParams(dimension_semantics=("arbitrary",),
                                             vmem_limit_bytes=VMEM_LIMIT),
        name="lru",
    )(xz, conv, conv_b, wbd, bias, lam, h0)


def _permute_w_in(w_in_l):
    valid = jnp.asarray(_GATE_SRC >= 0)
    gates = w_in_l[:, 5 * HW:M_COLS]
    ge = jnp.where(valid, gates[:, np.maximum(_GATE_SRC, 0)], 0.0)
    go = jnp.where(valid, gates[:, np.maximum(_GATE_SRC, 0) + 1], 0.0)
    return jnp.concatenate([w_in_l[:, 0:5 * HW], ge, go, w_in_l[:, M_COLS:]], axis=1).astype(BF16)


def _gate_source_index():
    src = np.full((LANES,), -1, np.int32)
    for dd in range(2):
        for is_f in range(2):
            for p in range(N_PAIRS):
                src[_gate_col(dd, is_f, p)] = is_f * 2 * N_HEADS + dd * N_HEADS + 2 * p
    return src


_GATE_SRC = _gate_source_index()


def _gate_bias(m_bi_l, m_bf_l):
    flat = jnp.concatenate([m_bi_l.reshape(-1), m_bf_l.reshape(-1)])
    idx = np.maximum(_GATE_SRC, 0)
    valid = jnp.asarray(_GATE_SRC >= 0)
    return jnp.stack([jnp.where(valid, flat[idx], 0.0), jnp.where(valid, flat[idx + 1], 0.0)])


def _block_diag(blocks):
    n, a, b = blocks.shape
    eye = jnp.eye(n, dtype=blocks.dtype)
    return (eye[:, None, :, None] * blocks[:, :, None, :]).reshape(n * a, n * b)


def _layer_params(l, g_pre, g_post, w_in, w_out, m_bi, m_bf, m_norm, r_mu, r_w0, r_w2, r_a0, r_a2,
                  r_kk, r_ka, r_rk, r_norm, l_conv, l_conv_b, l_wa, l_ba, l_wx, l_bx, l_lambda):
    row = lambda v: v.reshape(1, -1)
    return dict(
        g_pre=row(g_pre[l]), g_post=row(g_post[l]),
        w_in=_permute_w_in(w_in[l]), w_out=w_out[l].astype(BF16),
        gate_bias=_gate_bias(m_bi[l], m_bf[l]), m_norm=row(m_norm[l]),
        r_mu=row(r_mu[l]), r_w0=row(r_w0[l]), r_a0=row(r_a0[l]),
        r_w2=_block_diag(r_w2[l]).astype(BF16), r_a2=_block_diag(r_a2[l]).astype(BF16),
        r_kk=row(r_kk[l]), r_ka=row(r_ka[l]), r_rk=row(r_rk[l]), r_norm=row(r_norm[l]),
        l_conv=l_conv[l], l_conv_b=row(l_conv_b[l]),
        l_w=jnp.concatenate([_block_diag(l_wa[l][0]), _block_diag(l_wx[l][0]),
                             _block_diag(l_wa[l][1]), _block_diag(l_wx[l][1])], axis=1).astype(BF16),
        l_b=jnp.concatenate([l_ba[l][0], l_bx[l][0], l_ba[l][1], l_bx[l][1]]).reshape(1, -1),
        l_lambda=row(l_lambda[l]),
    )


def _trunk_layer(x, mod, lp, states, want_state, per_seq_mod, grid_shift):
    bsz = x.shape[0]
    qkv, oz, ge, go, rs, rz, lxz = _inproj(x, mod, lp['g_pre'], lp['w_in'], per_seq_mod)
    m_in = None if states is None else states[0:3]
    r_in = None if states is None else states[3]
    l_in = jnp.zeros((bsz, 2, 1, L_W), F32) if states is None else states[4][:, :, None, :]
    y_m, m_out = _mlstm(qkv, oz, ge, go, lp['gate_bias'], lp['m_norm'], m_in, want_state)
    y_r, r_out = _rwkv(rs, rz, lp['r_mu'], lp['r_w0'], lp['r_a0'], lp['r_w2'], lp['r_a2'], lp['r_kk'],
                       lp['r_ka'], lp['r_rk'], lp['r_norm'], r_in, want_state, grid_shift)
    y_l, l_out = _lru(lxz, lp['l_conv'], lp['l_conv_b'], lp['l_w'], lp['l_b'], lp['l_lambda'], l_in)
    y = _outproj(y_m, y_r, y_l, x, mod, lp['g_post'], lp['w_out'], per_seq_mod)
    new_states = m_out + (r_out, l_out[:, :, 0, :]) if want_state else None
    return y, new_states


def kernel(x_prompt, x_sample, c, state_mlstm_C, state_mlstm_n, state_mlstm_m, state_rwkv, state_rglru, c_ctx, g_pre, g_post, w_mod, b_mod, w_in, w_out, m_bi, m_bf, m_norm, r_mu, r_w0, r_w2, r_a0, r_a2, r_kk, r_ka, r_rk, r_norm, l_conv, l_conv_b, l_wa, l_ba, l_wx, l_bx, l_lambda):
    depth = w_in.shape[0]
    bp = x_prompt.shape[0]
    bs = x_sample.shape[0]
    d = x_prompt.shape[-1]
    rows = -(-(1 + bs) // SUBLANES) * SUBLANES
    cc = jnp.zeros((rows, d), F32).at[0].set(c_ctx).at[1:1 + bs].set(c)
    mod = _modulation(cc, w_mod, b_mod)
    xp, xs = x_prompt, x_sample
    new_states = []
    for l in range(depth):
        lp = _layer_params(l, g_pre, g_post, w_in, w_out, m_bi, m_bf, m_norm, r_mu, r_w0, r_w2, r_a0,
                           r_a2, r_kk, r_ka, r_rk, r_norm, l_conv, l_conv_b, l_wa, l_ba, l_wx, l_bx,
                           l_lambda)
        xp, st = _trunk_layer(xp, mod[l, 0:1][:, None, :], lp, None, True, False, False)
        new_states.append(st)
        cache = (state_mlstm_C[:, l].astype(F32), state_mlstm_n[:, l].astype(F32),
                 state_mlstm_m[:, l].astype(F32), state_rwkv[:, l].astype(F32),
                 state_rglru[:, l].astype(F32))
        xs, _ = _trunk_layer(xs, mod[l, 1:1 + bs][:, None, :], lp, cache, False, True, True)
    stacked = tuple(jnp.stack([st[i] for st in new_states], axis=1) for i in range(5))
    return (xp, xs) + stacked
```

```python
import functools

import numpy as np
import jax
import jax.numpy as jnp
from jax import lax
from jax.experimental import pallas as pl
from jax.experimental.pallas import tpu as pltpu

F32 = jnp.float32
BF16 = jnp.bfloat16
HIGHEST = lax.Precision.HIGHEST

D_MODEL = 1024
EPS = 1e-6
HD = 64
N_HEADS = 6
N_PAIRS = N_HEADS // 2
HW = N_HEADS * HD
CHUNK = 64
GRID_W = 64
LORA = 64
R_SHIFT = 3 * HW + 4 * LORA
L_W = 256
L_BLOCKS = 4
CONV_W = 4
LRU_C = 8.0
RWKV_DECAY_SCALE = 0.6065306597126334
M_COLS = 5 * HW + 4 * N_HEADS
IN_COLS = M_COLS + R_SHIFT + HW + 2 * L_W

LANES = 128
SUBLANES = 8
HALO = 16
HALO_ROWS = GRID_W
LRU_BLOCK = 128
VMEM_LIMIT = 56 * 1024 * 1024
MIXER_VMEM_BUDGET = 36 * 1024 * 1024

SEG_WIDTHS = (3 * HW, 2 * HW, LANES, LANES, R_SHIFT, HW, 2 * L_W)
SEG_DTYPES = (BF16, BF16, F32, F32, BF16, BF16, BF16)
SEG_OFFS = tuple(int(v) for v in np.cumsum((0,) + SEG_WIDTHS))
IN_COLS_PAD = SEG_OFFS[-1]
N_TILE = 512


def _gate_col(d, is_f, p):
    return 8 * d + 4 * is_f + p


def _bdot(a, b):
    return jnp.dot(a.astype(BF16), b.astype(BF16), preferred_element_type=F32)


def _bdot_nt(a, b):
    return lax.dot_general(a.astype(BF16), b.astype(BF16), (((1,), (1,)), ((), ())),
                           preferred_element_type=F32)


def _bdot_tn(a, b):
    return lax.dot_general(a.astype(BF16), b.astype(BF16), (((0,), (0,)), ((), ())),
                           preferred_element_type=F32)


def _fdot(a, b):
    return jnp.dot(a, b, precision=HIGHEST, preferred_element_type=F32)


def _split_bf16(x, terms):
    out = []
    for _ in range(terms - 1):
        hi = x.astype(BF16)
        out.append(hi)
        x = x - hi.astype(F32)
    out.append(x.astype(BF16))
    return out


def _cum_dot(cum_rep, x, terms=3):
    return jnp.dot(cum_rep, jnp.concatenate(_split_bf16(x, terms), axis=0), preferred_element_type=F32)


def _head_sum(x, ones_rep, terms=2):
    outs = []
    for p in range(x.shape[1] // LANES):
        parts = _split_bf16(x[:, p * LANES:(p + 1) * LANES], terms)
        outs.append(jnp.dot(jnp.concatenate(parts, axis=1), ones_rep, preferred_element_type=F32))
    return outs[0] if len(outs) == 1 else jnp.concatenate(outs, axis=1)


def _sigmoid(x):
    return 1.0 / (1.0 + jnp.exp(-x))


def _silu(x):
    return x * _sigmoid(x)


def _softplus(x):
    return jnp.maximum(x, 0.0) + jnp.log1p(jnp.exp(-jnp.abs(x)))


def _log_sigmoid(x):
    return -_softplus(-x)


def _iota2(shape, dim):
    return lax.broadcasted_iota(jnp.int32, shape, dim)


def _head_ones(n):
    return (_iota2((n, n), 0) // HD == _iota2((n, n), 1) // HD).astype(F32)


def _bd_stack(x, lo):
    return jnp.concatenate([jnp.where(lo, x, 0.0), jnp.where(lo, 0.0, x)], axis=0)


def _pair_col(q, j, lo):
    return jnp.where(lo, q[0:CHUNK, j:j + 1], q[CHUNK:2 * CHUNK, j:j + 1])


def _mod_kernel(c_ref, w_ref, b_ref, o_ref):
    o_ref[...] = _fdot(_silu(c_ref[...]), w_ref[...]) + b_ref[...]


def _modulation(cc, w_mod, b_mod):
    depth, d, n = w_mod.shape
    rows = cc.shape[0]
    return pl.pallas_call(
        _mod_kernel,
        grid=(depth, n // N_TILE),
        in_specs=[pl.BlockSpec((rows, d), lambda l, j: (0, 0)),
                  pl.BlockSpec((None, d, N_TILE), lambda l, j: (l, 0, j)),
                  pl.BlockSpec((None, 1, N_TILE), lambda l, j: (l, 0, j))],
        out_specs=pl.BlockSpec((None, rows, N_TILE), lambda l, j: (l, 0, j)),
        out_shape=jax.ShapeDtypeStruct((depth, rows, n), F32),
        compiler_params=pltpu.CompilerParams(dimension_semantics=("arbitrary", "arbitrary")),
        name="modulation",
    )(cc, w_mod, b_mod.reshape(depth, 1, n))


SEG_QKV, SEG_OZ, SEG_GE, SEG_GO, SEG_RS, SEG_RZ, SEG_LXZ = range(7)


def _inproj_kernel(x_ref, xprev_ref, xnext_ref, mod_ref, g_ref, w_ref, mu_ref, gb_ref, *out_refs,
                   grid_shift, single_tile):
    tm, d = x_ref.shape
    i = pl.program_id(1)
    n_tiles = pl.num_programs(1)
    mod = mod_ref[...]
    g = g_ref[...]

    def norm_mod(x):
        h = x * lax.rsqrt(jnp.mean(x * x, axis=-1, keepdims=True) + EPS) * g
        return (h * (1.0 + mod[:, d:2 * d]) + mod[:, 0:d]).astype(BF16)

    h = norm_mod(x_ref[...])
    for seg in (SEG_QKV, SEG_OZ, SEG_RZ, SEG_LXZ):
        o_ref, a, b = out_refs[seg], SEG_OFFS[seg], SEG_OFFS[seg + 1]
        for n0 in range(a, b, N_TILE):
            n1 = min(n0 + N_TILE, b)
            o_ref[:, n0 - a:n1 - a] = jnp.dot(h, w_ref[:, n0:n1],
                                              preferred_element_type=F32).astype(o_ref.dtype)

    is_f = (_iota2((1, LANES), 1) % 8) >= 4
    for par, seg in enumerate((SEG_GE, SEG_GO)):
        a, b = SEG_OFFS[seg], SEG_OFFS[seg + 1]
        pre = jnp.dot(h, w_ref[:, a:b], preferred_element_type=F32) + gb_ref[par:par + 1, :]
        out_refs[seg][...] = jnp.where(is_f, _log_sigmoid(pre), pre)

    if single_tile:
        h_all = h
    else:
        h_all = jnp.concatenate([norm_mod(xprev_ref[...]), h, norm_mod(xnext_ref[...])], axis=0)
    has_prev = i > 0
    has_next = i < n_tiles - 1
    row = _iota2((tm, 1), 0)
    a, b = SEG_OFFS[SEG_RS], SEG_OFFS[SEG_RS + 1]
    o_ref = out_refs[SEG_RS]
    for n0 in range(a, b, N_TILE):
        n1 = min(n0 + N_TILE, b)
        u_all = jnp.dot(h_all, w_ref[:, n0:n1], preferred_element_type=F32)
        if single_tile:
            u = u_all
            u_prev = u_next = jnp.zeros((HALO_ROWS, n1 - n0), F32)
        else:
            u = u_all[HALO_ROWS:HALO_ROWS + tm]
            u_prev = jnp.where(has_prev, u_all[0:HALO_ROWS], 0.0)
            u_next = jnp.where(has_next, u_all[HALO_ROWS + tm:], 0.0)
        left = pltpu.roll(u, 1, 0)
        right = pltpu.roll(u, tm - 1, 0)
        if grid_shift:
            left = jnp.where(row % GRID_W == 0, 0.0, left)
            right = jnp.where(row % GRID_W == GRID_W - 1, 0.0, right)
            up = jnp.concatenate([u_prev, u[0:tm - GRID_W]], axis=0)
            down = jnp.concatenate([u[GRID_W:tm], u_next], axis=0)
            sh = 0.25 * (up + down + left + right)
        else:
            left = jnp.where(row == 0, u_prev[HALO_ROWS - 1:HALO_ROWS, :], left)
            right = jnp.where(row == tm - 1, u_next[0:1, :], right)
            sh = 0.5 * (left + right)
        o_ref[:, n0 - a:n1 - a] = (u + mu_ref[:, n0 - a:n1 - a] * (sh - u)).astype(o_ref.dtype)


def _inproj(x, mod, g_pre, w_in_p, r_mu, gate_bias, per_seq_mod, grid_shift):
    bsz, t, d = x.shape
    tm = min(t, 512)
    hb = tm // HALO_ROWS
    n_halo = t // HALO_ROWS
    mod_idx = (lambda b, i: (b, 0, 0)) if per_seq_mod else (lambda b, i: (0, 0, 0))
    return pl.pallas_call(
        functools.partial(_inproj_kernel, grid_shift=grid_shift, single_tile=(t == tm)),
        grid=(bsz, t // tm),
        in_specs=[pl.BlockSpec((None, tm, d), lambda b, i: (b, i, 0)),
                  pl.BlockSpec((None, HALO_ROWS, d), lambda b, i: (b, jnp.maximum(i * hb - 1, 0), 0)),
                  pl.BlockSpec((None, HALO_ROWS, d),
                               lambda b, i: (b, jnp.minimum((i + 1) * hb, n_halo - 1), 0)),
                  pl.BlockSpec((None, 1, 3 * d), mod_idx),
                  pl.BlockSpec((1, d), lambda b, i: (0, 0)),
                  pl.BlockSpec((d, IN_COLS_PAD), lambda b, i: (0, 0), pipeline_mode=pl.Buffered(1)),
                  pl.BlockSpec((1, R_SHIFT), lambda b, i: (0, 0)),
                  pl.BlockSpec((2, LANES), lambda b, i: (0, 0))],
        out_specs=[pl.BlockSpec((None, tm, w), lambda b, i: (b, i, 0)) for w in SEG_WIDTHS],
        out_shape=[jax.ShapeDtypeStruct((bsz, t, w), dt) for w, dt in zip(SEG_WIDTHS, SEG_DTYPES)],
        compiler_params=pltpu.CompilerParams(dimension_semantics=("arbitrary", "arbitrary"),
                                             vmem_limit_bytes=VMEM_LIMIT),
        name="inproj",
    )(x, x, x, mod, g_pre, w_in_p, r_mu, gate_bias)


def _outproj_kernel(ym_ref, yr_ref, yl_ref, x_ref, mod_ref, g_ref, w_ref, o_ref):
    d = x_ref.shape[-1]
    o = jnp.dot(ym_ref[...], w_ref[0:HW, :], preferred_element_type=F32)
    o = o + jnp.dot(yr_ref[...], w_ref[HW:2 * HW, :], preferred_element_type=F32)
    o = o + jnp.dot(yl_ref[...], w_ref[2 * HW:2 * HW + L_W, :], preferred_element_type=F32)
    on = o * lax.rsqrt(jnp.mean(o * o, axis=-1, keepdims=True) + EPS) * g_ref[...]
    o_ref[...] = x_ref[...] + mod_ref[:, 2 * d:3 * d] * on


def _outproj(ym, yr, yl, x, mod, g_post, w_out_b, per_seq_mod):
    bsz, t, d = x.shape
    tm = min(t, 512)
    mod_idx = (lambda b, i: (b, 0, 0)) if per_seq_mod else (lambda b, i: (0, 0, 0))
    tok = lambda w: pl.BlockSpec((None, tm, w), lambda b, i: (b, i, 0))
    return pl.pallas_call(
        _outproj_kernel,
        grid=(bsz, t // tm),
        in_specs=[tok(HW), tok(HW), tok(L_W), tok(d),
                  pl.BlockSpec((None, 1, 3 * d), mod_idx),
                  pl.BlockSpec((1, d), lambda b, i: (0, 0)),
                  pl.BlockSpec((2 * HW + L_W, d), lambda b, i: (0, 0))],
        out_specs=tok(d),
        out_shape=jax.ShapeDtypeStruct((bsz, t, d), F32),
        compiler_params=pltpu.CompilerParams(dimension_semantics=("arbitrary", "arbitrary"),
                                             vmem_limit_bytes=VMEM_LIMIT),
        name="outproj",
    )(ym, yr, yl, x, mod, g_post, w_out_b)


def _mlstm_kernel(*refs, has_state_in, has_state_out):
    qkv_ref, oz_ref, ge_ref, go_ref, ng_ref = refs[0:5]
    refs = refs[5:]
    if has_state_in:
        c0_ref, n0_ref, m0_ref = refs[0:3]
        refs = refs[3:]
    y_ref = refs[0]
    refs = refs[1:]
    if has_state_out:
        c_out_ref, n_out_ref, m_out_ref = refs[0:3]
        refs = refs[3:]
    hacc, cn_s, m_s = refs
    ns, t = qkv_ref.shape[0], qkv_ref.shape[1]
    nc = t // CHUNK
    lane = _iota2((CHUNK, LANES), 1)
    row = _iota2((CHUNK, LANES), 0)
    lo = lane < HD
    s_idx = lane % HD
    causal = (s_idx <= row, s_idx >= row)
    r2 = _iota2((2 * CHUNK, 2 * CHUNK), 0)
    c2 = _iota2((2 * CHUNK, 2 * CHUNK), 1)
    same = (r2 // CHUNK) == (c2 // CHUNK)
    cum = ((same & (c2 <= r2)).astype(BF16), (same & (c2 >= r2)).astype(BF16))
    cum_rep = tuple(jnp.concatenate([m, m, m], axis=1) for m in cum)
    bd_ones = same.astype(F32)
    ones_rep = jnp.concatenate([same.astype(BF16), same.astype(BF16)], axis=0)
    bd_mask2 = jnp.concatenate([same, same], axis=1)
    colid = _iota2((1, LANES), 1)
    is_f = (colid % 8) >= 4

    hacc[...] = jnp.zeros_like(hacc)
    cn_s[...] = jnp.zeros_like(cn_s)
    m_s[...] = jnp.zeros_like(m_s)
    if has_state_in:
        m_s[...] = m0_ref[...]
        for s in range(ns):
            for d in range(2):
                for p in range(N_PAIRS):
                    for h in range(2):
                        cn_s[s, d, p, h * HD:(h + 1) * HD, h * HD:(h + 1) * HD] = c0_ref[s, d, 2 * p + h]
                    n_rows = jnp.where(same, jnp.broadcast_to(n0_ref[s, d, p], (2 * CHUNK, LANES)), 0.0)
                    cn_s[s, d, p, :, LANES:2 * LANES] = n_rows.T

    def gate_table(s, t0, d):
        x = jnp.concatenate([ge_ref[s, pl.ds(t0, CHUNK), :], go_ref[s, pl.ds(t0, CHUNK), :]], axis=0)
        q = jnp.where(is_f, _cum_dot(cum_rep[d], x), x)
        return q, q.T

    chains = [(s, d, p) for s in range(ns) for d in range(2) for p in range(N_PAIRS)]

    def body(j, carry):
        t0s = (pl.multiple_of(j * CHUNK, CHUNK), pl.multiple_of((nc - 1 - j) * CHUNK, CHUNK))
        tabs = {(s, d): gate_table(s, t0s[d], d) for s in range(ns) for d in range(2)}
        st = []
        for s, d, p in chains:
            t0 = t0s[d]
            q_tab, q_tab_t = tabs[(s, d)]
            ji, jf = _gate_col(d, 0, p), _gate_col(d, 1, p)
            c = dict(t0=t0, sl=slice(p * LANES, (p + 1) * LANES))
            c['bcol'] = _pair_col(q_tab, jf, lo)
            c['icol'] = _pair_col(q_tab, ji, lo)
            brow = q_tab_t[jf:jf + 1, :]
            irow = q_tab_t[ji:ji + 1, :]
            c['q'] = qkv_ref[s, pl.ds(t0, CHUNK), p * LANES:(p + 1) * LANES].astype(F32)
            c['k'] = qkv_ref[s, pl.ds(t0, CHUNK), HW + p * LANES:HW + (p + 1) * LANES].astype(F32) * (HD ** -0.5)
            c['v'] = qkv_ref[s, pl.ds(t0, CHUNK), 2 * HW + p * LANES:2 * HW + (p + 1) * LANES].astype(F32)
            c['mprev'] = m_s[s, d, p]
            c['cn'] = cn_s[s, d, p]
            dmat = jnp.where(causal[d], c['bcol'] - brow + irow, -jnp.inf)
            inter = c['bcol'] + c['mprev']
            mx = jnp.where(lo,
                           jnp.max(jnp.where(lo, dmat, -jnp.inf), axis=1, keepdims=True),
                           jnp.max(jnp.where(lo, -jnp.inf, dmat), axis=1, keepdims=True))
            c['m_t'] = jnp.maximum(inter, mx)
            c['pexp'] = jnp.exp(dmat - c['m_t'])
            c['sc'] = jnp.exp(inter - c['m_t'])
            last = CHUNK - 1 if d == 0 else 0
            b_last = c['bcol'][last:last + 1, :]
            g = b_last - c['bcol'] + c['icol']
            c['m_new'] = jnp.maximum(b_last + c['mprev'], jnp.max(g, axis=0, keepdims=True))
            c['dec'] = jnp.exp(b_last + c['mprev'] - c['m_new'])
            c['wk'] = jnp.exp(g - c['m_new'])
            st.append(c)
        for c in st:
            c['s'] = _bdot_nt(c['q'], _bd_stack(c['k'], lo)) * c['pexp']
        for c in st:
            c['a1'] = _bdot(c['q'], c['cn'])
        for c in st:
            c['upd'] = _bdot_tn(c['wk'] * c['k'], jnp.concatenate([c['v'], jnp.ones_like(c['v'])], axis=1))
        for c in st:
            c['a2'] = _bdot(c['s'], jnp.concatenate([_bd_stack(c['v'], lo), bd_ones], axis=1))
        for (s, d, p), c in zip(chains, st):
            num = c['sc'] * c['a1'][:, 0:LANES] + c['a2'][:, 0:LANES]
            den = c['sc'] * c['a1'][:, LANES:] + c['a2'][:, LANES:]
            h = num / jnp.maximum(jnp.abs(den), jnp.exp(-c['m_t']))
            hacc[s, pl.ds(c['t0'], CHUNK), c['sl']] = hacc[s, pl.ds(c['t0'], CHUNK), c['sl']] + h
            cn_s[s, d, p] = (jnp.concatenate([c['dec'], c['dec']], axis=1) * c['cn']
                             + jnp.where(bd_mask2, c['upd'], 0.0))
            m_s[s, d, p] = c['m_new']
        return carry

    lax.fori_loop(0, nc, body, 0)
    if has_state_out:
        m_out_ref[...] = m_s[...]
        lo2 = _iota2((1, LANES), 1) < HD
        for s in range(ns):
            for d in range(2):
                for p in range(N_PAIRS):
                    for h in range(2):
                        c_out_ref[s, d, 2 * p + h] = cn_s[s, d, p, h * HD:(h + 1) * HD, h * HD:(h + 1) * HD]
                    n_t = cn_s[s, d, p, :, LANES:2 * LANES].T
                    n_out_ref[s, d, p] = jnp.where(lo2, n_t[0:1, :], n_t[HD:HD + 1, :])

    ng = ng_ref[...]
    rb = min(t, 256)

    def epilogue(i, carry):
        r0 = pl.multiple_of(i * rb, rb)
        for s in range(ns):
            h = hacc[s, pl.ds(r0, rb), :]
            hn = h * lax.rsqrt(_head_sum(h * h, ones_rep) * (1.0 / HD) + EPS) * ng
            o = oz_ref[s, pl.ds(r0, rb), 0:HW].astype(F32)
            z = oz_ref[s, pl.ds(r0, rb), HW:2 * HW].astype(F32)
            y_ref[s, pl.ds(r0, rb), :] = (hn * _sigmoid(o) * _silu(z)).astype(y_ref.dtype)
        return carry

    lax.fori_loop(0, t // rb, epilogue, 0)


def _seq_spec(ns, t, w, single_buffer):
    if single_buffer:
        return pl.BlockSpec((ns, t, w), lambda b: (b, 0, 0), pipeline_mode=pl.Buffered(1))
    return pl.BlockSpec((ns, t, w), lambda b: (b, 0, 0))


def _full_spec(shape):
    n = len(shape)
    return pl.BlockSpec(shape, lambda b: (0,) * n)


def _state_spec(ns, shape):
    return pl.BlockSpec((ns,) + shape, lambda b: (b,) + (0,) * len(shape))


def _seqs_per_step(bsz, bytes_per_seq):
    for ns in (2, 1):
        if bsz % ns == 0 and ns * bytes_per_seq <= MIXER_VMEM_BUDGET:
            return ns
    return 1


def _pair_rows(x):
    return jnp.repeat(x.reshape(x.shape[:2] + (N_PAIRS, 1, 2)), HD, axis=-1)


def _mlstm(qkv, oz, ge, go, norm_g, state_in, want_state):
    bsz, t, _ = qkv.shape
    big = t > 512
    in_bufs = 1 if big else 2
    ns = _seqs_per_step(bsz, t * ((5 * HW * qkv.dtype.itemsize + 2 * LANES * 4) * in_bufs
                                  + HW * 2 * 2 + HW * 4))
    c_shape, row_shape = (2, N_HEADS, HD, HD), (2, N_PAIRS, 1, LANES)
    in_specs = [_seq_spec(ns, t, 3 * HW, big), _seq_spec(ns, t, 2 * HW, big),
                _seq_spec(ns, t, LANES, big), _seq_spec(ns, t, LANES, big), _full_spec((1, HW))]
    args = [qkv, oz, ge, go, norm_g]
    if state_in is not None:
        c0, n0, m0 = state_in
        in_specs += [_state_spec(ns, c_shape), _state_spec(ns, row_shape), _state_spec(ns, row_shape)]
        args += [c0, n0.reshape((bsz,) + row_shape), _pair_rows(m0)]
    out_specs = [_seq_spec(ns, t, HW, False)]
    out_shape = [jax.ShapeDtypeStruct((bsz, t, HW), BF16)]
    if want_state:
        out_specs += [_state_spec(ns, c_shape), _state_spec(ns, row_shape), _state_spec(ns, row_shape)]
        out_shape += [jax.ShapeDtypeStruct((bsz,) + c_shape, F32),
                      jax.ShapeDtypeStruct((bsz,) + row_shape, F32),
                      jax.ShapeDtypeStruct((bsz,) + row_shape, F32)]
    outs = pl.pallas_call(
        functools.partial(_mlstm_kernel, has_state_in=state_in is not None, has_state_out=want_state),
        grid=(bsz // ns,),
        in_specs=in_specs,
        out_specs=out_specs,
        out_shape=out_shape,
        scratch_shapes=[pltpu.VMEM((ns, t, HW), F32),
                        pltpu.VMEM((ns, 2, N_PAIRS, 2 * CHUNK, 2 * LANES), F32),
                        pltpu.VMEM((ns,) + row_shape, F32)],
        compiler_params=pltpu.CompilerParams(dimension_semantics=("arbitrary",),
                                             vmem_limit_bytes=VMEM_LIMIT),
        name="mlstm",
    )(*args)
    if not want_state:
        return outs[0], None
    y, c1, n1, m1 = outs
    return y, (c1, n1.reshape(bsz, 2, N_HEADS, HD), m1[:, :, :, 0, ::HD].reshape(bsz, 2, N_HEADS))


def _rwkv_kernel(*refs, has_state_in, has_state_out):
    (rs_ref, rz_ref, w0_ref, a0_ref, w2_ref, a2_ref, kk_ref, ka_ref, rk_ref, ng_ref) = refs[0:10]
    refs = refs[10:]
    if has_state_in:
        s0_ref = refs[0]
        refs = refs[1:]
    y_ref = refs[0]
    refs = refs[1:]
    if has_state_out:
        s_out_ref = refs[0]
        refs = refs[1:]
    yacc, bonus, s_s = refs
    ns, t = rs_ref.shape[0], rs_ref.shape[1]
    nc = t // CHUNK
    lane = _iota2((CHUNK, LANES), 1)
    row = _iota2((CHUNK, LANES), 0)
    lo = lane < HD
    s_idx = lane % HD
    strict = (s_idx < row, s_idx > row)
    incl = (s_idx <= row, s_idx >= row)
    eye_pk = (s_idx == row).astype(F32)
    r64 = _iota2((CHUNK, CHUNK), 0)
    c64 = _iota2((CHUNK, CHUNK), 1)
    cum = ((c64 <= r64).astype(BF16), (c64 >= r64).astype(BF16))
    cum_rep = tuple(jnp.concatenate([m, m, m], axis=1) for m in cum)
    r2 = _iota2((2 * CHUNK, 2 * CHUNK), 0)
    c2 = _iota2((2 * CHUNK, 2 * CHUNK), 1)
    bd_mask = (r2 // CHUNK) == (c2 // CHUNK)
    ones_rep = jnp.concatenate([bd_mask.astype(BF16), bd_mask.astype(BF16)], axis=0)
    kk = kk_ref[...]
    ka = ka_ref[...]
    rk = rk_ref[...]

    yacc[...] = jnp.zeros_like(yacc)
    s_s[...] = jnp.zeros_like(s_s)
    if has_state_in:
        for s in range(ns):
            for d in range(2):
                for p in range(N_PAIRS):
                    for h in range(2):
                        s_s[s, d, p, h * HD:(h + 1) * HD, h * HD:(h + 1) * HD] = s0_ref[s, d, 2 * p + h]

    def prep(s, c, d):
        t0 = pl.multiple_of(c * CHUNK, CHUNK)
        blk = rs_ref[s, pl.ds(t0, CHUNK), :].astype(F32)
        r = blk[:, 0:HW]
        k = blk[:, HW:2 * HW]
        v = blk[:, 2 * HW:3 * HW]
        wl = blk[:, 3 * HW:3 * HW + 2 * LORA]
        al = blk[:, 3 * HW + 2 * LORA:3 * HW + 4 * LORA]
        dsl = slice(d * HW, (d + 1) * HW)
        logw = -RWKV_DECAY_SCALE * _sigmoid(w0_ref[:, dsl] + _bdot(jnp.tanh(wl), w2_ref[:, dsl]))
        a = _sigmoid(a0_ref[:, dsl] + _bdot(al, a2_ref[:, dsl]))
        kappa = k * kk
        khat = kappa * lax.rsqrt(jnp.maximum(_head_sum(kappa * kappa, ones_rep), 1e-24))
        kt = k * (1.0 + (a - 1.0) * ka)
        b = khat * a
        if d == 0:
            bonus[s, pl.ds(t0, CHUNK), :] = _head_sum(r * k * rk, ones_rep) * v
        last = CHUNK - 1 if d == 0 else 0
        logp = _cum_dot(cum_rep[d], logw)
        logp_last = logp[last:last + 1, :]
        e_neg = jnp.exp(-logp)
        e_last = jnp.exp(logp_last - logp)
        return dict(t0=t0, v=v, alpha=jnp.exp(logp - logw) * khat, beta=b * e_neg, kap=kt * e_neg,
                    rho=r * jnp.exp(logp), beta_l=b * e_last, kap_l=kt * e_last,
                    p_last=jnp.exp(logp_last))

    chains = [(s, d, p) for s in range(ns) for d in range(2) for p in range(N_PAIRS)]

    def body(j, carry):
        pre = {(s, d): prep(s, j if d == 0 else nc - 1 - j, d) for s in range(ns) for d in range(2)}
        st = []
        for s, d, p in chains:
            sl = slice(p * LANES, (p + 1) * LANES)
            c = {key: (val if key == 't0' else val[:, sl]) for key, val in pre[(s, d)].items()}
            c['sl'] = sl
            st.append(c)
        for c in st:
            c['g'] = _bdot_nt(jnp.concatenate([c['alpha'], c['rho']], axis=0),
                              jnp.concatenate([_bd_stack(c['beta'], lo), _bd_stack(c['kap'], lo)], axis=0))
        for (s, d, p), c in zip(chains, st):
            g = c.pop('g')
            c['n'] = jnp.where(strict[d], g[0:CHUNK, 0:LANES], 0.0)
            c['mk'] = jnp.where(strict[d], g[0:CHUNK, LANES:], 0.0)
            c['mrb'] = jnp.where(incl[d], g[CHUNK:, 0:LANES], 0.0)
            c['mrk'] = jnp.where(incl[d], g[CHUNK:, LANES:], 0.0)
            c['inv'] = eye_pk - c['n']
        for c in st:
            c['pw'] = _bdot(c['n'], _bd_stack(c['n'], lo))
        for c in st:
            c['mkv'] = _bdot(c['mk'], _bd_stack(c['v'], lo))
        for _ in range(4):
            for c in st:
                rp = _bdot(jnp.concatenate([c['inv'], c['pw']], axis=0), _bd_stack(c['pw'], lo))
                c['inv'] = c['inv'] + rp[0:CHUNK]
                c['pw'] = rp[CHUNK:]
        for c in st:
            c['inv'] = c['inv'] + _bdot(c['inv'], _bd_stack(c['pw'], lo))
        for c in st:
            c['wu'] = _bdot(c['inv'], jnp.concatenate([_bd_stack(c['alpha'], lo), _bd_stack(c['mkv'], lo)], axis=1))
        for (s, d, p), c in zip(chains, st):
            c['s_prev'] = s_s[s, d, p]
            ws = _bdot_nt(jnp.concatenate([c['wu'][:, 0:LANES], c['rho']], axis=0), c['s_prev'])
            c['u'] = ws[0:CHUNK] + c['wu'][:, LANES:]
            c['ys'] = ws[CHUNK:]
        for c in st:
            c['upd'] = _bdot_tn(jnp.concatenate([c['v'], -c['u']], axis=0),
                                jnp.concatenate([c['kap_l'], c['beta_l']], axis=0))
        for c in st:
            c['y'] = c['ys'] + _bdot(jnp.concatenate([c['mrk'], -c['mrb']], axis=1),
                                     jnp.concatenate([_bd_stack(c['v'], lo), _bd_stack(c['u'], lo)], axis=0))
        for (s, d, p), c in zip(chains, st):
            s_s[s, d, p] = c['s_prev'] * c['p_last'] + jnp.where(bd_mask, c['upd'], 0.0)
            yacc[s, pl.ds(c['t0'], CHUNK), c['sl']] = yacc[s, pl.ds(c['t0'], CHUNK), c['sl']] + c['y']
        return carry

    lax.fori_loop(0, nc, body, 0)
    if has_state_out:
        for s in range(ns):
            for d in range(2):
                for p in range(N_PAIRS):
                    for h in range(2):
                        s_out_ref[s, d, 2 * p + h] = s_s[s, d, p, h * HD:(h + 1) * HD, h * HD:(h + 1) * HD]

    ng = ng_ref[...]
    rb = min(t, 256)

    def epilogue(i, carry):
        r0 = pl.multiple_of(i * rb, rb)
        for s in range(ns):
            y = yacc[s, pl.ds(r0, rb), :]
            yn = (y * lax.rsqrt(_head_sum(y * y, ones_rep) * (1.0 / HD) + EPS) * ng
                  + bonus[s, pl.ds(r0, rb), :])
            y_ref[s, pl.ds(r0, rb), :] = (yn * _silu(rz_ref[s, pl.ds(r0, rb), :].astype(F32))).astype(y_ref.dtype)
        return carry

    lax.fori_loop(0, t // rb, epilogue, 0)


def _rwkv(rs, rz, w0, a0, w2bd, a2bd, kk, ka, rk, norm_g, state_in, want_state):
    bsz, t, _ = rs.shape
    big = t > 512
    in_bufs = 1 if big else 2
    itemsize = rs.dtype.itemsize
    ns = _seqs_per_step(bsz, t * ((R_SHIFT + HW) * itemsize * in_bufs + HW * 2 * 2 + 2 * HW * 4))
    s_shape = (2, N_HEADS, HD, HD)
    in_specs = [_seq_spec(ns, t, R_SHIFT, big), _seq_spec(ns, t, HW, big),
                _full_spec((1, 2 * HW)), _full_spec((1, 2 * HW)),
                _full_spec((2 * LORA, 2 * HW)), _full_spec((2 * LORA, 2 * HW)),
                _full_spec((1, HW)), _full_spec((1, HW)), _full_spec((1, HW)), _full_spec((1, HW))]
    args = [rs, rz, w0, a0, w2bd, a2bd, kk, ka, rk, norm_g]
    if state_in is not None:
        in_specs.append(_state_spec(ns, s_shape))
        args.append(state_in)
    out_specs = [_seq_spec(ns, t, HW, False)]
    out_shape = [jax.ShapeDtypeStruct((bsz, t, HW), BF16)]
    if want_state:
        out_specs.append(_state_spec(ns, s_shape))
        out_shape.append(jax.ShapeDtypeStruct((bsz,) + s_shape, F32))
    outs = pl.pallas_call(
        functools.partial(_rwkv_kernel, has_state_in=state_in is not None, has_state_out=want_state),
        grid=(bsz // ns,),
        in_specs=in_specs,
        out_specs=out_specs,
        out_shape=out_shape,
        scratch_shapes=[pltpu.VMEM((ns, t, HW), F32), pltpu.VMEM((ns, t, HW), F32),
                        pltpu.VMEM((ns, 2, N_PAIRS, 2 * CHUNK, LANES), F32)],
        compiler_params=pltpu.CompilerParams(dimension_semantics=("arbitrary",),
                                             vmem_limit_bytes=VMEM_LIMIT),
        name="rwkv",
    )(*args)
    return (outs[0], outs[1]) if want_state else (outs[0], None)


def _lru_kernel(xz_ref, conv_ref, cb_ref, w_ref, b_ref, lam_ref, h0_ref, y_ref, hfin_ref, acc, hc):
    t = xz_ref.shape[0]
    nb = t // LRU_BLOCK
    row = _iota2((LRU_BLOCK, L_W), 0)
    conv = conv_ref[...]
    cb = cb_ref[...]
    sp = _softplus(-lam_ref[...])

    acc[...] = jnp.zeros_like(acc)
    hc[...] = h0_ref[...]

    def process(j, d):
        t0 = pl.multiple_of(j * LRU_BLOCK, LRU_BLOCK)
        x = xz_ref[pl.ds(t0, LRU_BLOCK), 0:L_W].astype(F32)
        prev_rows = xz_ref[pl.ds(pl.multiple_of(jnp.maximum(t0 - HALO, 0), HALO), HALO), 0:L_W]
        next_rows = xz_ref[pl.ds(pl.multiple_of(jnp.minimum(t0 + LRU_BLOCK, t - HALO), HALO), HALO), 0:L_W]
        prev_rows = jnp.where(j > 0, prev_rows.astype(F32), 0.0)
        next_rows = jnp.where(j < nb - 1, next_rows.astype(F32), 0.0)
        p1, p2 = prev_rows[HALO - 1:HALO, :], prev_rows[HALO - 2:HALO - 1, :]
        xm1 = jnp.where(row == 0, p1, pltpu.roll(x, 1, 0))
        xm2 = jnp.where(row == 0, p2, jnp.where(row == 1, p1, pltpu.roll(x, 2, 0)))
        xp1 = jnp.where(row == LRU_BLOCK - 1, next_rows[0:1, :], pltpu.roll(x, LRU_BLOCK - 1, 0))
        xc = conv[0:1, :] * xm2 + conv[1:2, :] * xm1 + conv[2:3, :] * x + conv[3:4, :] * xp1 + cb
        pre = _bdot(xc, w_ref[:, 2 * d * L_W:2 * (d + 1) * L_W]) + b_ref[:, 2 * d * L_W:2 * (d + 1) * L_W]
        rg = _sigmoid(pre[:, 0:L_W])
        ig = _sigmoid(pre[:, L_W:])
        log_a = -LRU_C * rg * sp[:, d * L_W:(d + 1) * L_W]
        a_cum = jnp.exp(log_a)
        b_cum = jnp.sqrt(jnp.tanh(-log_a) * (1.0 + a_cum * a_cum)) * ig * xc
        k = 1
        while k < LRU_BLOCK:
            if d == 0:
                keep = row >= k
                a_sh = jnp.where(keep, pltpu.roll(a_cum, k, 0), 1.0)
                b_sh = jnp.where(keep, pltpu.roll(b_cum, k, 0), 0.0)
            else:
                keep = row < LRU_BLOCK - k
                a_sh = jnp.where(keep, pltpu.roll(a_cum, LRU_BLOCK - k, 0), 1.0)
                b_sh = jnp.where(keep, pltpu.roll(b_cum, LRU_BLOCK - k, 0), 0.0)
            b_cum = a_cum * b_sh + b_cum
            a_cum = a_cum * a_sh
            k *= 2
        h = b_cum + a_cum * hc[d]
        last = LRU_BLOCK - 1 if d == 0 else 0
        hc[d] = h[last:last + 1, :]
        acc[pl.ds(t0, LRU_BLOCK), :] = acc[pl.ds(t0, LRU_BLOCK), :] + h

    def body(j, carry):
        process(j, 0)
        process(nb - 1 - j, 1)
        return carry

    lax.fori_loop(0, nb, body, 0)
    hfin_ref[...] = hc[...]

    def epilogue(i, carry):
        r0 = pl.multiple_of(i * LRU_BLOCK, LRU_BLOCK)
        z = xz_ref[pl.ds(r0, LRU_BLOCK), L_W:2 * L_W].astype(F32)
        y_ref[pl.ds(r0, LRU_BLOCK), :] = (acc[pl.ds(r0, LRU_BLOCK), :] * _silu(z)).astype(y_ref.dtype)
        return carry

    lax.fori_loop(0, nb, epilogue, 0)


def _lru(xz, conv, conv_b, wbd, bias, lam, h0):
    bsz, t, _ = xz.shape
    st = lambda shape: pl.BlockSpec((None,) + shape, lambda b: (b,) + (0,) * len(shape))
    return pl.pallas_call(
        _lru_kernel,
        grid=(bsz,),
        in_specs=[_seq_spec(None, t, 2 * L_W, False), _full_spec((CONV_W, L_W)), _full_spec((1, L_W)),
                  _full_spec((L_W, 4 * L_W)), _full_spec((1, 4 * L_W)), _full_spec((1, 2 * L_W)),
                  st((2, 1, L_W))],
        out_specs=[pl.BlockSpec((None, t, L_W), lambda b: (b, 0, 0)), st((2, 1, L_W))],
        out_shape=[jax.ShapeDtypeStruct((bsz, t, L_W), BF16),
                   jax.ShapeDtypeStruct((bsz, 2, 1, L_W), F32)],
        scratch_shapes=[pltpu.VMEM((t, L_W), F32), pltpu.VMEM((2, 1, L_W), F32)],
        compiler_params=pltpu.CompilerParams(dimension_semantics=("arbitrary",),
                                             vmem_limit_bytes=VMEM_LIMIT),
        name="lru",
    )(xz, conv, conv_b, wbd, bias, lam, h0)


def _permute_w_in(w_in_l):
    valid = jnp.asarray(_GATE_SRC >= 0)
    gates = w_in_l[:, 5 * HW:M_COLS]
    ge = jnp.where(valid, gates[:, np.maximum(_GATE_SRC, 0)], 0.0)
    go = jnp.where(valid, gates[:, np.maximum(_GATE_SRC, 0) + 1], 0.0)
    return jnp.concatenate([w_in_l[:, 0:5 * HW], ge, go, w_in_l[:, M_COLS:]], axis=1).astype(BF16)


def _gate_source_index():
    src = np.full((LANES,), -1, np.int32)
    for dd in range(2):
        for is_f in range(2):
            for p in range(N_PAIRS):
                src[_gate_col(dd, is_f, p)] = is_f * 2 * N_HEADS + dd * N_HEADS + 2 * p
    return src


_GATE_SRC = _gate_source_index()


def _gate_bias(m_bi_l, m_bf_l):
    flat = jnp.concatenate([m_bi_l.reshape(-1), m_bf_l.reshape(-1)])
    idx = np.maximum(_GATE_SRC, 0)
    valid = jnp.asarray(_GATE_SRC >= 0)
    return jnp.stack([jnp.where(valid, flat[idx], 0.0), jnp.where(valid, flat[idx + 1], 0.0)])


def _block_diag(blocks):
    n, a, b = blocks.shape
    eye = jnp.eye(n, dtype=blocks.dtype)
    return (eye[:, None, :, None] * blocks[:, :, None, :]).reshape(n * a, n * b)


def _layer_params(l, g_pre, g_post, w_in, w_out, m_bi, m_bf, m_norm, r_mu, r_w0, r_w2, r_a0, r_a2,
                  r_kk, r_ka, r_rk, r_norm, l_conv, l_conv_b, l_wa, l_ba, l_wx, l_bx, l_lambda):
    row = lambda v: v.reshape(1, -1)
    return dict(
        g_pre=row(g_pre[l]), g_post=row(g_post[l]),
        w_in=_permute_w_in(w_in[l]), w_out=w_out[l].astype(BF16),
        gate_bias=_gate_bias(m_bi[l], m_bf[l]), m_norm=row(m_norm[l]),
        r_mu=row(r_mu[l]), r_w0=row(r_w0[l]), r_a0=row(r_a0[l]),
        r_w2=_block_diag(r_w2[l]).astype(BF16), r_a2=_block_diag(r_a2[l]).astype(BF16),
        r_kk=row(r_kk[l]), r_ka=row(r_ka[l]), r_rk=row(r_rk[l]), r_norm=row(r_norm[l]),
        l_conv=l_conv[l], l_conv_b=row(l_conv_b[l]),
        l_w=jnp.concatenate([_block_diag(l_wa[l][0]), _block_diag(l_wx[l][0]),
                             _block_diag(l_wa[l][1]), _block_diag(l_wx[l][1])], axis=1).astype(BF16),
        l_b=jnp.concatenate([l_ba[l][0], l_bx[l][0], l_ba[l][1], l_bx[l][1]]).reshape(1, -1),
        l_lambda=row(l_lambda[l]),
    )


def _trunk_layer(x, mod, lp, states, want_state, per_seq_mod, grid_shift):
    bsz = x.shape[0]
    qkv, oz, ge, go, rs, rz, lxz = _inproj(x, mod, lp['g_pre'], lp['w_in'], lp['r_mu'], lp['gate_bias'],
                                           per_seq_mod, grid_shift)
    m_in = None if states is None else states[0:3]
    r_in = None if states is None else states[3]
    l_in = jnp.zeros((bsz, 2, 1, L_W), F32) if states is None else states[4][:, :, None, :]
    y_m, m_out = _mlstm(qkv, oz, ge, go, lp['m_norm'], m_in, want_state)
    y_r, r_out = _rwkv(rs, rz, lp['r_w0'], lp['r_a0'], lp['r_w2'], lp['r_a2'], lp['r_kk'],
                       lp['r_ka'], lp['r_rk'], lp['r_norm'], r_in, want_state)
    y_l, l_out = _lru(lxz, lp['l_conv'], lp['l_conv_b'], lp['l_w'], lp['l_b'], lp['l_lambda'], l_in)
    y = _outproj(y_m, y_r, y_l, x, mod, lp['g_post'], lp['w_out'], per_seq_mod)
    new_states = m_out + (r_out, l_out[:, :, 0, :]) if want_state else None
    return y, new_states


def kernel(x_prompt, x_sample, c, state_mlstm_C, state_mlstm_n, state_mlstm_m, state_rwkv, state_rglru, c_ctx, g_pre, g_post, w_mod, b_mod, w_in, w_out, m_bi, m_bf, m_norm, r_mu, r_w0, r_w2, r_a0, r_a2, r_kk, r_ka, r_rk, r_norm, l_conv, l_conv_b, l_wa, l_ba, l_wx, l_bx, l_lambda):
    depth = w_in.shape[0]
    bp = x_prompt.shape[0]
    bs = x_sample.shape[0]
    d = x_prompt.shape[-1]
    rows = -(-(1 + bs) // SUBLANES) * SUBLANES
    cc = jnp.zeros((rows, d), F32).at[0].set(c_ctx).at[1:1 + bs].set(c)
    mod = _modulation(cc, w_mod, b_mod)
    xp, xs = x_prompt, x_sample
    new_states = []
    for l in range(depth):
        lp = _layer_params(l, g_pre, g_post, w_in, w_out, m_bi, m_bf, m_norm, r_mu, r_w0, r_w2, r_a0,
                           r_a2, r_kk, r_ka, r_rk, r_norm, l_conv, l_conv_b, l_wa, l_ba, l_wx, l_bx,
                           l_lambda)
        xp, st = _trunk_layer(xp, mod[l, 0:1][:, None, :], lp, None, True, False, False)
        new_states.append(st)
        cache = (state_mlstm_C[:, l].astype(F32), state_mlstm_n[:, l].astype(F32),
                 state_mlstm_m[:, l].astype(F32), state_rwkv[:, l].astype(F32),
                 state_rglru[:, l].astype(F32))
        xs, _ = _trunk_layer(xs, mod[l, 1:1 + bs][:, None, :], lp, cache, False, True, True)
    stacked = tuple(jnp.stack([st[i] for st in new_states], axis=1) for i in range(5))
    return (xp, xs) + stacked
```

```python
import functools

import numpy as np
import jax
import jax.numpy as jnp
from jax import lax
from jax.experimental import pallas as pl
from jax.experimental.pallas import tpu as pltpu

F32 = jnp.float32
BF16 = jnp.bfloat16
HIGHEST = lax.Precision.HIGHEST

D_MODEL = 1024
EPS = 1e-6
HD = 64
N_HEADS = 6
N_PAIRS = N_HEADS // 2
HW = N_HEADS * HD
CHUNK = 64
GRID_W = 64
LORA = 64
R_SHIFT = 3 * HW + 4 * LORA
L_W = 256
L_BLOCKS = 4
CONV_W = 4
LRU_C = 8.0
RWKV_DECAY_SCALE = 0.6065306597126334
M_COLS = 5 * HW + 4 * N_HEADS
IN_COLS = M_COLS + R_SHIFT + HW + 2 * L_W

LANES = 128
SUBLANES = 8
HALO = 16
HALO_ROWS = GRID_W
LRU_BLOCK = 128
VMEM_LIMIT = 56 * 1024 * 1024
MIXER_VMEM_BUDGET = 36 * 1024 * 1024

SEG_WIDTHS = (3 * HW, 2 * HW, LANES, LANES, R_SHIFT, HW, 2 * L_W)
SEG_DTYPES = (BF16, BF16, F32, F32, BF16, BF16, BF16)
SEG_OFFS = tuple(int(v) for v in np.cumsum((0,) + SEG_WIDTHS))
IN_COLS_PAD = SEG_OFFS[-1]
N_TILE = 512


def _gate_col(d, is_f, p):
    return 8 * d + 4 * is_f + p


def _bdot(a, b):
    return jnp.dot(a.astype(BF16), b.astype(BF16), preferred_element_type=F32)


def _bdot_nt(a, b):
    return lax.dot_general(a.astype(BF16), b.astype(BF16), (((1,), (1,)), ((), ())),
                           preferred_element_type=F32)


def _bdot_tn(a, b):
    return lax.dot_general(a.astype(BF16), b.astype(BF16), (((0,), (0,)), ((), ())),
                           preferred_element_type=F32)


def _fdot(a, b):
    return jnp.dot(a, b, precision=HIGHEST, preferred_element_type=F32)


def _split_bf16(x, terms):
    out = []
    for _ in range(terms - 1):
        hi = x.astype(BF16)
        out.append(hi)
        x = x - hi.astype(F32)
    out.append(x.astype(BF16))
    return out


def _cum_dot(cum_rep, x, terms=3):
    return jnp.dot(cum_rep, jnp.concatenate(_split_bf16(x, terms), axis=0), preferred_element_type=F32)


def _head_sum(x, ones_rep, terms=2):
    outs = []
    for p in range(x.shape[1] // LANES):
        parts = _split_bf16(x[:, p * LANES:(p + 1) * LANES], terms)
        outs.append(jnp.dot(jnp.concatenate(parts, axis=1), ones_rep, preferred_element_type=F32))
    return outs[0] if len(outs) == 1 else jnp.concatenate(outs, axis=1)


def _sigmoid(x):
    return 0.5 * jnp.tanh(0.5 * x) + 0.5


def _silu(x):
    return x * _sigmoid(x)


def _softplus(x):
    return jnp.maximum(x, 0.0) + jnp.log1p(jnp.exp(-jnp.abs(x)))


def _log_sigmoid(x):
    return -_softplus(-x)


def _iota2(shape, dim):
    return lax.broadcasted_iota(jnp.int32, shape, dim)


def _head_ones(n):
    return (_iota2((n, n), 0) // HD == _iota2((n, n), 1) // HD).astype(F32)


def _bd_stack(x, lo):
    return jnp.concatenate([jnp.where(lo, x, 0.0), jnp.where(lo, 0.0, x)], axis=0)


def _pair_col(q, j, lo):
    return jnp.where(lo, q[0:CHUNK, j:j + 1], q[CHUNK:2 * CHUNK, j:j + 1])


def _mod_kernel(c_ref, w_ref, b_ref, o_ref):
    o_ref[...] = _fdot(_silu(c_ref[...]), w_ref[...]) + b_ref[...]


def _modulation(cc, w_mod, b_mod):
    depth, d, n = w_mod.shape
    rows = cc.shape[0]
    return pl.pallas_call(
        _mod_kernel,
        grid=(depth, n // N_TILE),
        in_specs=[pl.BlockSpec((rows, d), lambda l, j: (0, 0)),
                  pl.BlockSpec((None, d, N_TILE), lambda l, j: (l, 0, j)),
                  pl.BlockSpec((None, 1, N_TILE), lambda l, j: (l, 0, j))],
        out_specs=pl.BlockSpec((None, rows, N_TILE), lambda l, j: (l, 0, j)),
        out_shape=jax.ShapeDtypeStruct((depth, rows, n), F32),
        compiler_params=pltpu.CompilerParams(dimension_semantics=("arbitrary", "arbitrary")),
        name="modulation",
    )(cc, w_mod, b_mod.reshape(depth, 1, n))


SEG_QKV, SEG_OZ, SEG_GE, SEG_GO, SEG_RS, SEG_RZ, SEG_LXZ = range(7)


def _inproj_kernel(x_ref, xprev_ref, xnext_ref, mod_ref, g_ref, w_ref, mu_ref, gb_ref, *refs,
                   grid_shift, single_tile):
    tm, d = x_ref.shape
    i = pl.program_id(1)
    n_tiles = pl.num_programs(1)
    mod = mod_ref[...]
    g = g_ref[...]

    def norm_mod(x):
        h = x * lax.rsqrt(jnp.mean(x * x, axis=-1, keepdims=True) + EPS) * g
        return (h * (1.0 + mod[:, d:2 * d]) + mod[:, 0:d]).astype(BF16)

    out_refs, h_s = refs[:-1], refs[-1]
    off = 0 if single_tile else HALO_ROWS
    h_s[off:off + tm, :] = norm_mod(x_ref[...])
    if not single_tile:
        h_s[0:off, :] = norm_mod(xprev_ref[...])
        h_s[off + tm:, :] = norm_mod(xnext_ref[...])
    has_prev = i > 0
    has_next = i < n_tiles - 1
    row = _iota2((tm, 1), 0)
    a, b = SEG_OFFS[SEG_RS], SEG_OFFS[SEG_RS + 1]
    o_ref = out_refs[SEG_RS]
    for n0 in range(a, b, N_TILE):
        n1 = min(n0 + N_TILE, b)
        u_all = jnp.dot(h_s[...], w_ref[:, n0:n1], preferred_element_type=F32)
        if single_tile:
            u = u_all
            u_prev = u_next = jnp.zeros((HALO_ROWS, n1 - n0), F32)
        else:
            u = u_all[HALO_ROWS:HALO_ROWS + tm]
            u_prev = jnp.where(has_prev, u_all[0:HALO_ROWS], 0.0)
            u_next = jnp.where(has_next, u_all[HALO_ROWS + tm:], 0.0)
        left = pltpu.roll(u, 1, 0)
        right = pltpu.roll(u, tm - 1, 0)
        if grid_shift:
            left = jnp.where(row % GRID_W == 0, 0.0, left)
            right = jnp.where(row % GRID_W == GRID_W - 1, 0.0, right)
            up = jnp.concatenate([u_prev, u[0:tm - GRID_W]], axis=0)
            down = jnp.concatenate([u[GRID_W:tm], u_next], axis=0)
            sh = 0.25 * (up + down + left + right)
        else:
            left = jnp.where(row == 0, u_prev[HALO_ROWS - 1:HALO_ROWS, :], left)
            right = jnp.where(row == tm - 1, u_next[0:1, :], right)
            sh = 0.5 * (left + right)
        o_ref[:, n0 - a:n1 - a] = (u + mu_ref[:, n0 - a:n1 - a] * (sh - u)).astype(o_ref.dtype)

    is_f = (_iota2((1, LANES), 1) % 8) >= 4
    for par, seg in enumerate((SEG_GE, SEG_GO)):
        a, b = SEG_OFFS[seg], SEG_OFFS[seg + 1]
        pre = jnp.dot(h_s[off:off + tm, :], w_ref[:, a:b], preferred_element_type=F32) + gb_ref[par:par + 1, :]
        out_refs[seg][...] = jnp.where(is_f, _log_sigmoid(pre), pre)

    for seg in (SEG_QKV, SEG_OZ, SEG_RZ, SEG_LXZ):
        o_ref, a, b = out_refs[seg], SEG_OFFS[seg], SEG_OFFS[seg + 1]
        for n0 in range(a, b, N_TILE):
            n1 = min(n0 + N_TILE, b)
            o_ref[:, n0 - a:n1 - a] = jnp.dot(h_s[off:off + tm, :], w_ref[:, n0:n1],
                                              preferred_element_type=F32).astype(o_ref.dtype)


def _inproj(x, mod, g_pre, w_in_p, r_mu, gate_bias, per_seq_mod, grid_shift):
    bsz, t, d = x.shape
    tm = min(t, 512)
    hb = tm // HALO_ROWS
    n_halo = t // HALO_ROWS
    mod_idx = (lambda b, i: (b, 0, 0)) if per_seq_mod else (lambda b, i: (0, 0, 0))
    return pl.pallas_call(
        functools.partial(_inproj_kernel, grid_shift=grid_shift, single_tile=(t == tm)),
        grid=(bsz, t // tm),
        in_specs=[pl.BlockSpec((None, tm, d), lambda b, i: (b, i, 0)),
                  pl.BlockSpec((None, HALO_ROWS, d), lambda b, i: (b, jnp.maximum(i * hb - 1, 0), 0)),
                  pl.BlockSpec((None, HALO_ROWS, d),
                               lambda b, i: (b, jnp.minimum((i + 1) * hb, n_halo - 1), 0)),
                  pl.BlockSpec((None, 1, 3 * d), mod_idx),
                  pl.BlockSpec((1, d), lambda b, i: (0, 0)),
                  pl.BlockSpec((d, IN_COLS_PAD), lambda b, i: (0, 0), pipeline_mode=pl.Buffered(1)),
                  pl.BlockSpec((1, R_SHIFT), lambda b, i: (0, 0)),
                  pl.BlockSpec((2, LANES), lambda b, i: (0, 0))],
        out_specs=[pl.BlockSpec((None, tm, w), lambda b, i: (b, i, 0)) for w in SEG_WIDTHS],
        out_shape=[jax.ShapeDtypeStruct((bsz, t, w), dt) for w, dt in zip(SEG_WIDTHS, SEG_DTYPES)],
        scratch_shapes=[pltpu.VMEM((tm if t == tm else tm + 2 * HALO_ROWS, d), BF16)],
        compiler_params=pltpu.CompilerParams(dimension_semantics=("arbitrary", "arbitrary"),
                                             vmem_limit_bytes=VMEM_LIMIT),
        name="inproj",
    )(x, x, x, mod, g_pre, w_in_p, r_mu, gate_bias)


def _outproj_kernel(ym_ref, yr_ref, yl_ref, x_ref, mod_ref, g_ref, w_ref, o_ref):
    d = x_ref.shape[-1]
    o = jnp.dot(ym_ref[...], w_ref[0:HW, :], preferred_element_type=F32)
    o = o + jnp.dot(yr_ref[...], w_ref[HW:2 * HW, :], preferred_element_type=F32)
    o = o + jnp.dot(yl_ref[...], w_ref[2 * HW:2 * HW + L_W, :], preferred_element_type=F32)
    on = o * lax.rsqrt(jnp.mean(o * o, axis=-1, keepdims=True) + EPS) * g_ref[...]
    o_ref[...] = x_ref[...] + mod_ref[:, 2 * d:3 * d] * on


def _outproj(ym, yr, yl, x, mod, g_post, w_out_b, per_seq_mod):
    bsz, t, d = x.shape
    tm = min(t, 512)
    mod_idx = (lambda b, i: (b, 0, 0)) if per_seq_mod else (lambda b, i: (0, 0, 0))
    tok = lambda w: pl.BlockSpec((None, tm, w), lambda b, i: (b, i, 0))
    return pl.pallas_call(
        _outproj_kernel,
        grid=(bsz, t // tm),
        in_specs=[tok(HW), tok(HW), tok(L_W), tok(d),
                  pl.BlockSpec((None, 1, 3 * d), mod_idx),
                  pl.BlockSpec((1, d), lambda b, i: (0, 0)),
                  pl.BlockSpec((2 * HW + L_W, d), lambda b, i: (0, 0))],
        out_specs=tok(d),
        out_shape=jax.ShapeDtypeStruct((bsz, t, d), F32),
        compiler_params=pltpu.CompilerParams(dimension_semantics=("arbitrary", "arbitrary"),
                                             vmem_limit_bytes=VMEM_LIMIT),
        name="outproj",
    )(ym, yr, yl, x, mod, g_post, w_out_b)


def _mlstm_kernel(*refs, has_state_in, has_state_out):
    qkv_ref, oz_ref, ge_ref, go_ref, ng_ref = refs[0:5]
    refs = refs[5:]
    if has_state_in:
        c0_ref, n0_ref, m0_ref = refs[0:3]
        refs = refs[3:]
    y_ref = refs[0]
    refs = refs[1:]
    if has_state_out:
        c_out_ref, n_out_ref, m_out_ref = refs[0:3]
        refs = refs[3:]
    hacc, cn_s, m_s = refs
    ns, t = qkv_ref.shape[0], qkv_ref.shape[1]
    nc = t // CHUNK
    lane = _iota2((CHUNK, LANES), 1)
    row = _iota2((CHUNK, LANES), 0)
    lo = lane < HD
    s_idx = lane % HD
    causal = (s_idx <= row, s_idx >= row)
    r2 = _iota2((2 * CHUNK, 2 * CHUNK), 0)
    c2 = _iota2((2 * CHUNK, 2 * CHUNK), 1)
    same = (r2 // CHUNK) == (c2 // CHUNK)
    cum = ((same & (c2 <= r2)).astype(BF16), (same & (c2 >= r2)).astype(BF16))
    cum_rep = tuple(jnp.concatenate([m, m, m], axis=1) for m in cum)
    bd_ones = same.astype(F32)
    ones_rep = jnp.concatenate([same.astype(BF16), same.astype(BF16)], axis=0)
    bd_mask2 = jnp.concatenate([same, same], axis=1)
    colid = _iota2((1, LANES), 1)
    is_f = (colid % 8) >= 4

    hacc[...] = jnp.zeros_like(hacc)
    cn_s[...] = jnp.zeros_like(cn_s)
    m_s[...] = jnp.zeros_like(m_s)
    if has_state_in:
        m_s[...] = m0_ref[...]
        for s in range(ns):
            for d in range(2):
                for p in range(N_PAIRS):
                    for h in range(2):
                        cn_s[s, d, p, h * HD:(h + 1) * HD, h * HD:(h + 1) * HD] = c0_ref[s, d, 2 * p + h]
                    n_rows = jnp.where(same, jnp.broadcast_to(n0_ref[s, d, p], (2 * CHUNK, LANES)), 0.0)
                    cn_s[s, d, p, :, LANES:2 * LANES] = n_rows.T

    def gate_table(s, t0, d):
        x = jnp.concatenate([ge_ref[s, pl.ds(t0, CHUNK), :], go_ref[s, pl.ds(t0, CHUNK), :]], axis=0)
        q = jnp.where(is_f, _cum_dot(cum_rep[d], x), x)
        return q, q.T

    chains = [(s, d, p) for s in range(ns) for d in range(2) for p in range(N_PAIRS)]

    def body(j, carry):
        t0s = (pl.multiple_of(j * CHUNK, CHUNK), pl.multiple_of((nc - 1 - j) * CHUNK, CHUNK))
        tabs = {(s, d): gate_table(s, t0s[d], d) for s in range(ns) for d in range(2)}
        st = []
        for s, d, p in chains:
            t0 = t0s[d]
            q_tab, q_tab_t = tabs[(s, d)]
            ji, jf = _gate_col(d, 0, p), _gate_col(d, 1, p)
            c = dict(t0=t0, sl=slice(p * LANES, (p + 1) * LANES))
            c['bcol'] = _pair_col(q_tab, jf, lo)
            c['icol'] = _pair_col(q_tab, ji, lo)
            brow = q_tab_t[jf:jf + 1, :]
            irow = q_tab_t[ji:ji + 1, :]
            c['q'] = qkv_ref[s, pl.ds(t0, CHUNK), p * LANES:(p + 1) * LANES].astype(F32)
            c['k'] = qkv_ref[s, pl.ds(t0, CHUNK), HW + p * LANES:HW + (p + 1) * LANES].astype(F32) * (HD ** -0.5)
            c['v'] = qkv_ref[s, pl.ds(t0, CHUNK), 2 * HW + p * LANES:2 * HW + (p + 1) * LANES].astype(F32)
            c['mprev'] = m_s[s, d, p]
            c['cn'] = cn_s[s, d, p]
            dmat = jnp.where(causal[d], c['bcol'] - brow + irow, -jnp.inf)
            inter = c['bcol'] + c['mprev']
            mx = jnp.where(lo,
                           jnp.max(jnp.where(lo, dmat, -jnp.inf), axis=1, keepdims=True),
                           jnp.max(jnp.where(lo, -jnp.inf, dmat), axis=1, keepdims=True))
            c['m_t'] = jnp.maximum(inter, mx)
            c['pexp'] = jnp.exp(dmat - c['m_t'])
            c['sc'] = jnp.exp(inter - c['m_t'])
            last = CHUNK - 1 if d == 0 else 0
            b_last = c['bcol'][last:last + 1, :]
            g = b_last - c['bcol'] + c['icol']
            c['m_new'] = jnp.maximum(b_last + c['mprev'], jnp.max(g, axis=0, keepdims=True))
            c['dec'] = jnp.exp(b_last + c['mprev'] - c['m_new'])
            c['wk'] = jnp.exp(g - c['m_new'])
            st.append(c)
        for c in st:
            c['s'] = _bdot_nt(c['q'], _bd_stack(c['k'], lo)) * c['pexp']
        for c in st:
            c['a1'] = _bdot(c['q'], c['cn'])
        for c in st:
            c['upd'] = _bdot_tn(c['wk'] * c['k'], jnp.concatenate([c['v'], jnp.ones_like(c['v'])], axis=1))
        for c in st:
            c['a2'] = _bdot(c['s'], jnp.concatenate([_bd_stack(c['v'], lo), bd_ones], axis=1))
        for (s, d, p), c in zip(chains, st):
            num = c['sc'] * c['a1'][:, 0:LANES] + c['a2'][:, 0:LANES]
            den = c['sc'] * c['a1'][:, LANES:] + c['a2'][:, LANES:]
            h = num / jnp.maximum(jnp.abs(den), jnp.exp(-c['m_t']))
            hacc[s, pl.ds(c['t0'], CHUNK), c['sl']] = hacc[s, pl.ds(c['t0'], CHUNK), c['sl']] + h
            cn_s[s, d, p] = (jnp.concatenate([c['dec'], c['dec']], axis=1) * c['cn']
                             + jnp.where(bd_mask2, c['upd'], 0.0))
            m_s[s, d, p] = c['m_new']
        return carry

    lax.fori_loop(0, nc, body, 0)
    if has_state_out:
        m_out_ref[...] = m_s[...]
        lo2 = _iota2((1, LANES), 1) < HD
        for s in range(ns):
            for d in range(2):
                for p in range(N_PAIRS):
                    for h in range(2):
                        c_out_ref[s, d, 2 * p + h] = cn_s[s, d, p, h * HD:(h + 1) * HD, h * HD:(h + 1) * HD]
                    n_t = cn_s[s, d, p, :, LANES:2 * LANES].T
                    n_out_ref[s, d, p] = jnp.where(lo2, n_t[0:1, :], n_t[HD:HD + 1, :])

    ng = ng_ref[...]
    rb = min(t, 256)

    def epilogue(i, carry):
        r0 = pl.multiple_of(i * rb, rb)
        for s in range(ns):
            h = hacc[s, pl.ds(r0, rb), :]
            hn = h * lax.rsqrt(_head_sum(h * h, ones_rep) * (1.0 / HD) + EPS) * ng
            o = oz_ref[s, pl.ds(r0, rb), 0:HW].astype(F32)
            z = oz_ref[s, pl.ds(r0, rb), HW:2 * HW].astype(F32)
            y_ref[s, pl.ds(r0, rb), :] = (hn * _sigmoid(o) * _silu(z)).astype(y_ref.dtype)
        return carry

    lax.fori_loop(0, t // rb, epilogue, 0)


def _seq_spec(ns, t, w, single_buffer):
    if single_buffer:
        return pl.BlockSpec((ns, t, w), lambda b: (b, 0, 0), pipeline_mode=pl.Buffered(1))
    return pl.BlockSpec((ns, t, w), lambda b: (b, 0, 0))


def _full_spec(shape):
    n = len(shape)
    return pl.BlockSpec(shape, lambda b: (0,) * n)


def _state_spec(ns, shape):
    return pl.BlockSpec((ns,) + shape, lambda b: (b,) + (0,) * len(shape))


def _seqs_per_step(bsz, bytes_per_seq):
    for ns in (2, 1):
        if bsz % ns == 0 and ns * bytes_per_seq <= MIXER_VMEM_BUDGET:
            return ns
    return 1


def _pair_rows(x):
    return jnp.repeat(x.reshape(x.shape[:2] + (N_PAIRS, 1, 2)), HD, axis=-1)


def _mlstm(qkv, oz, ge, go, norm_g, state_in, want_state):
    bsz, t, _ = qkv.shape
    big = t > 512
    in_bufs = 1 if big else 2
    ns = _seqs_per_step(bsz, t * ((5 * HW * qkv.dtype.itemsize + 2 * LANES * 4) * in_bufs
                                  + HW * 2 * 2 + HW * 4))
    c_shape, row_shape = (2, N_HEADS, HD, HD), (2, N_PAIRS, 1, LANES)
    in_specs = [_seq_spec(ns, t, 3 * HW, big), _seq_spec(ns, t, 2 * HW, big),
                _seq_spec(ns, t, LANES, big), _seq_spec(ns, t, LANES, big), _full_spec((1, HW))]
    args = [qkv, oz, ge, go, norm_g]
    if state_in is not None:
        c0, n0, m0 = state_in
        in_specs += [_state_spec(ns, c_shape), _state_spec(ns, row_shape), _state_spec(ns, row_shape)]
        args += [c0, n0.reshape((bsz,) + row_shape), _pair_rows(m0)]
    out_specs = [_seq_spec(ns, t, HW, False)]
    out_shape = [jax.ShapeDtypeStruct((bsz, t, HW), BF16)]
    if want_state:
        out_specs += [_state_spec(ns, c_shape), _state_spec(ns, row_shape), _state_spec(ns, row_shape)]
        out_shape += [jax.ShapeDtypeStruct((bsz,) + c_shape, F32),
                      jax.ShapeDtypeStruct((bsz,) + row_shape, F32),
                      jax.ShapeDtypeStruct((bsz,) + row_shape, F32)]
    outs = pl.pallas_call(
        functools.partial(_mlstm_kernel, has_state_in=state_in is not None, has_state_out=want_state),
        grid=(bsz // ns,),
        in_specs=in_specs,
        out_specs=out_specs,
        out_shape=out_shape,
        scratch_shapes=[pltpu.VMEM((ns, t, HW), F32),
                        pltpu.VMEM((ns, 2, N_PAIRS, 2 * CHUNK, 2 * LANES), F32),
                        pltpu.VMEM((ns,) + row_shape, F32)],
        compiler_params=pltpu.CompilerParams(dimension_semantics=("arbitrary",),
                                             vmem_limit_bytes=VMEM_LIMIT),
        name="mlstm",
    )(*args)
    if not want_state:
        return outs[0], None
    y, c1, n1, m1 = outs
    return y, (c1, n1.reshape(bsz, 2, N_HEADS, HD), m1[:, :, :, 0, ::HD].reshape(bsz, 2, N_HEADS))


def _rwkv_kernel(*refs, has_state_in, has_state_out):
    (rs_ref, rz_ref, w0_ref, a0_ref, w2_ref, a2_ref, kk_ref, ka_ref, rk_ref, ng_ref) = refs[0:10]
    refs = refs[10:]
    if has_state_in:
        s0_ref = refs[0]
        refs = refs[1:]
    y_ref = refs[0]
    refs = refs[1:]
    if has_state_out:
        s_out_ref = refs[0]
        refs = refs[1:]
    yacc, bonus, s_s = refs
    ns, t = rs_ref.shape[0], rs_ref.shape[1]
    nc = t // CHUNK
    lane = _iota2((CHUNK, LANES), 1)
    row = _iota2((CHUNK, LANES), 0)
    lo = lane < HD
    s_idx = lane % HD
    strict = (s_idx < row, s_idx > row)
    incl = (s_idx <= row, s_idx >= row)
    eye_pk = (s_idx == row).astype(F32)
    r64 = _iota2((CHUNK, CHUNK), 0)
    c64 = _iota2((CHUNK, CHUNK), 1)
    cum = ((c64 <= r64).astype(BF16), (c64 >= r64).astype(BF16))
    cum_rep = tuple(jnp.concatenate([m, m, m], axis=1) for m in cum)
    r2 = _iota2((2 * CHUNK, 2 * CHUNK), 0)
    c2 = _iota2((2 * CHUNK, 2 * CHUNK), 1)
    bd_mask = (r2 // CHUNK) == (c2 // CHUNK)
    ones_rep = jnp.concatenate([bd_mask.astype(BF16), bd_mask.astype(BF16)], axis=0)
    kk = kk_ref[...]
    ka = ka_ref[...]
    rk = rk_ref[...]

    yacc[...] = jnp.zeros_like(yacc)
    s_s[...] = jnp.zeros_like(s_s)
    if has_state_in:
        for s in range(ns):
            for d in range(2):
                for p in range(N_PAIRS):
                    for h in range(2):
                        s_s[s, d, p, h * HD:(h + 1) * HD, h * HD:(h + 1) * HD] = s0_ref[s, d, 2 * p + h]

    def prep(s, c, d):
        t0 = pl.multiple_of(c * CHUNK, CHUNK)
        blk = rs_ref[s, pl.ds(t0, CHUNK), :].astype(F32)
        r = blk[:, 0:HW]
        k = blk[:, HW:2 * HW]
        v = blk[:, 2 * HW:3 * HW]
        wl = blk[:, 3 * HW:3 * HW + 2 * LORA]
        al = blk[:, 3 * HW + 2 * LORA:3 * HW + 4 * LORA]
        dsl = slice(d * HW, (d + 1) * HW)
        logw = -RWKV_DECAY_SCALE * _sigmoid(w0_ref[:, dsl] + _bdot(jnp.tanh(wl), w2_ref[:, dsl]))
        a = _sigmoid(a0_ref[:, dsl] + _bdot(al, a2_ref[:, dsl]))
        kappa = k * kk
        khat = kappa * lax.rsqrt(jnp.maximum(_head_sum(kappa * kappa, ones_rep), 1e-24))
        kt = k * (1.0 + (a - 1.0) * ka)
        b = khat * a
        if d == 0:
            bonus[s, pl.ds(t0, CHUNK), :] = _head_sum(r * k * rk, ones_rep) * v
        last = CHUNK - 1 if d == 0 else 0
        logp = _cum_dot(cum_rep[d], logw)
        logp_last = logp[last:last + 1, :]
        e_neg = jnp.exp(-logp)
        e_last = jnp.exp(logp_last - logp)
        return dict(t0=t0, v=v, alpha=jnp.exp(logp - logw) * khat, beta=b * e_neg, kap=kt * e_neg,
                    rho=r * jnp.exp(logp), beta_l=b * e_last, kap_l=kt * e_last,
                    p_last=jnp.exp(logp_last))

    chains = [(s, d, p) for s in range(ns) for d in range(2) for p in range(N_PAIRS)]

    def body(j, carry):
        pre = {(s, d): prep(s, j if d == 0 else nc - 1 - j, d) for s in range(ns) for d in range(2)}
        st = []
        for s, d, p in chains:
            sl = slice(p * LANES, (p + 1) * LANES)
            c = {key: (val if key == 't0' else val[:, sl]) for key, val in pre[(s, d)].items()}
            c['sl'] = sl
            st.append(c)
        for c in st:
            c['g'] = _bdot_nt(jnp.concatenate([c['alpha'], c['rho']], axis=0),
                              jnp.concatenate([_bd_stack(c['beta'], lo), _bd_stack(c['kap'], lo)], axis=0))
        for (s, d, p), c in zip(chains, st):
            g = c.pop('g')
            c['n'] = jnp.where(strict[d], g[0:CHUNK, 0:LANES], 0.0)
            c['mk'] = jnp.where(strict[d], g[0:CHUNK, LANES:], 0.0)
            c['mrb'] = jnp.where(incl[d], g[CHUNK:, 0:LANES], 0.0)
            c['mrk'] = jnp.where(incl[d], g[CHUNK:, LANES:], 0.0)
            c['inv'] = eye_pk - c['n']
        for c in st:
            c['pw'] = _bdot(c['n'], _bd_stack(c['n'], lo))
        for c in st:
            c['mkv'] = _bdot(c['mk'], _bd_stack(c['v'], lo))
        for _ in range(4):
            for c in st:
                rp = _bdot(jnp.concatenate([c['inv'], c['pw']], axis=0), _bd_stack(c['pw'], lo))
                c['inv'] = c['inv'] + rp[0:CHUNK]
                c['pw'] = rp[CHUNK:]
        for c in st:
            c['inv'] = c['inv'] + _bdot(c['inv'], _bd_stack(c['pw'], lo))
        for c in st:
            c['wu'] = _bdot(c['inv'], jnp.concatenate([_bd_stack(c['alpha'], lo), _bd_stack(c['mkv'], lo)], axis=1))
        for (s, d, p), c in zip(chains, st):
            c['s_prev'] = s_s[s, d, p]
            ws = _bdot_nt(jnp.concatenate([c['wu'][:, 0:LANES], c['rho']], axis=0), c['s_prev'])
            c['u'] = ws[0:CHUNK] + c['wu'][:, LANES:]
            c['ys'] = ws[CHUNK:]
        for c in st:
            c['upd'] = _bdot_tn(jnp.concatenate([c['v'], -c['u']], axis=0),
                                jnp.concatenate([c['kap_l'], c['beta_l']], axis=0))
        for c in st:
            c['y'] = c['ys'] + _bdot(jnp.concatenate([c['mrk'], -c['mrb']], axis=1),
                                     jnp.concatenate([_bd_stack(c['v'], lo), _bd_stack(c['u'], lo)], axis=0))
        for (s, d, p), c in zip(chains, st):
            s_s[s, d, p] = c['s_prev'] * c['p_last'] + jnp.where(bd_mask, c['upd'], 0.0)
            yacc[s, pl.ds(c['t0'], CHUNK), c['sl']] = yacc[s, pl.ds(c['t0'], CHUNK), c['sl']] + c['y']
        return carry

    lax.fori_loop(0, nc, body, 0)
    if has_state_out:
        for s in range(ns):
            for d in range(2):
                for p in range(N_PAIRS):
                    for h in range(2):
                        s_out_ref[s, d, 2 * p + h] = s_s[s, d, p, h * HD:(h + 1) * HD, h * HD:(h + 1) * HD]

    ng = ng_ref[...]
    rb = min(t, 256)

    def epilogue(i, carry):
        r0 = pl.multiple_of(i * rb, rb)
        for s in range(ns):
            y = yacc[s, pl.ds(r0, rb), :]
            yn = (y * lax.rsqrt(_head_sum(y * y, ones_rep) * (1.0 / HD) + EPS) * ng
                  + bonus[s, pl.ds(r0, rb), :])
            y_ref[s, pl.ds(r0, rb), :] = (yn * _silu(rz_ref[s, pl.ds(r0, rb), :].astype(F32))).astype(y_ref.dtype)
        return carry

    lax.fori_loop(0, t // rb, epilogue, 0)


def _rwkv(rs, rz, w0, a0, w2bd, a2bd, kk, ka, rk, norm_g, state_in, want_state):
    bsz, t, _ = rs.shape
    big = t > 512
    in_bufs = 1 if big else 2
    itemsize = rs.dtype.itemsize
    ns = _seqs_per_step(bsz, t * ((R_SHIFT + HW) * itemsize * in_bufs + HW * 2 * 2 + 2 * HW * 4))
    s_shape = (2, N_HEADS, HD, HD)
    in_specs = [_seq_spec(ns, t, R_SHIFT, big), _seq_spec(ns, t, HW, big),
                _full_spec((1, 2 * HW)), _full_spec((1, 2 * HW)),
                _full_spec((2 * LORA, 2 * HW)), _full_spec((2 * LORA, 2 * HW)),
                _full_spec((1, HW)), _full_spec((1, HW)), _full_spec((1, HW)), _full_spec((1, HW))]
    args = [rs, rz, w0, a0, w2bd, a2bd, kk, ka, rk, norm_g]
    if state_in is not None:
        in_specs.append(_state_spec(ns, s_shape))
        args.append(state_in)
    out_specs = [_seq_spec(ns, t, HW, False)]
    out_shape = [jax.ShapeDtypeStruct((bsz, t, HW), BF16)]
    if want_state:
        out_specs.append(_state_spec(ns, s_shape))
        out_shape.append(jax.ShapeDtypeStruct((bsz,) + s_shape, F32))
    outs = pl.pallas_call(
        functools.partial(_rwkv_kernel, has_state_in=state_in is not None, has_state_out=want_state),
        grid=(bsz // ns,),
        in_specs=in_specs,
        out_specs=out_specs,
        out_shape=out_shape,
        scratch_shapes=[pltpu.VMEM((ns, t, HW), F32), pltpu.VMEM((ns, t, HW), F32),
                        pltpu.VMEM((ns, 2, N_PAIRS, 2 * CHUNK, LANES), F32)],
        compiler_params=pltpu.CompilerParams(dimension_semantics=("arbitrary",),
                                             vmem_limit_bytes=VMEM_LIMIT),
        name="rwkv",
    )(*args)
    return (outs[0], outs[1]) if want_state else (outs[0], None)


def _lru_kernel(xz_ref, conv_ref, cb_ref, w_ref, b_ref, lam_ref, h0_ref, y_ref, hfin_ref, a_s, b_s, acc):
    t = xz_ref.shape[0]
    nb = t // LRU_BLOCK
    n_groups = LRU_BLOCK // SUBLANES
    row = _iota2((LRU_BLOCK, L_W), 0)
    sub = _iota2((n_groups, SUBLANES, L_W), 1)
    conv = conv_ref[...]
    cb = cb_ref[...]
    sp = _softplus(-lam_ref[...])

    def coefficients(j, carry):
        t0 = pl.multiple_of(j * LRU_BLOCK, LRU_BLOCK)
        x = xz_ref[pl.ds(t0, LRU_BLOCK), 0:L_W].astype(F32)
        prev_rows = xz_ref[pl.ds(pl.multiple_of(jnp.maximum(t0 - HALO, 0), HALO), HALO), 0:L_W]
        next_rows = xz_ref[pl.ds(pl.multiple_of(jnp.minimum(t0 + LRU_BLOCK, t - HALO), HALO), HALO), 0:L_W]
        prev_rows = jnp.where(j > 0, prev_rows.astype(F32), 0.0)
        next_rows = jnp.where(j < nb - 1, next_rows.astype(F32), 0.0)
        p1, p2 = prev_rows[HALO - 1:HALO, :], prev_rows[HALO - 2:HALO - 1, :]
        xm1 = jnp.where(row == 0, p1, pltpu.roll(x, 1, 0))
        xm2 = jnp.where(row == 0, p2, jnp.where(row == 1, p1, pltpu.roll(x, 2, 0)))
        xp1 = jnp.where(row == LRU_BLOCK - 1, next_rows[0:1, :], pltpu.roll(x, LRU_BLOCK - 1, 0))
        xc = conv[0:1, :] * xm2 + conv[1:2, :] * xm1 + conv[2:3, :] * x + conv[3:4, :] * xp1 + cb
        pre = _bdot(xc, w_ref[...]) + b_ref[...]
        for d in range(2):
            rg = _sigmoid(pre[:, 2 * d * L_W:(2 * d + 1) * L_W])
            ig = _sigmoid(pre[:, (2 * d + 1) * L_W:(2 * d + 2) * L_W])
            log_a = -LRU_C * rg * sp[:, d * L_W:(d + 1) * L_W]
            a_step = jnp.exp(log_a)
            b_step = jnp.sqrt(jnp.tanh(-log_a) * (1.0 + a_step * a_step)) * ig * xc
            a_cum = a_step.reshape(n_groups, SUBLANES, L_W)
            b_cum = b_step.reshape(n_groups, SUBLANES, L_W)
            k = 1
            while k < SUBLANES:
                keep = (sub >= k) if d == 0 else (sub < SUBLANES - k)
                shift = k if d == 0 else SUBLANES - k
                a_sh = jnp.where(keep, pltpu.roll(a_cum, shift, 1), 1.0)
                b_sh = jnp.where(keep, pltpu.roll(b_cum, shift, 1), 0.0)
                b_cum = a_cum * b_sh + b_cum
                a_cum = a_cum * a_sh
                k *= 2
            a_s[d, pl.ds(t0, LRU_BLOCK), :] = a_cum.reshape(LRU_BLOCK, L_W)
            b_s[d, pl.ds(t0, LRU_BLOCK), :] = b_cum.reshape(LRU_BLOCK, L_W)
        return carry

    lax.fori_loop(0, nb, coefficients, 0)

    def scan_block(d, j, carry):
        t0 = pl.multiple_of(j * LRU_BLOCK, LRU_BLOCK)
        a_cum = a_s[d, pl.ds(t0, LRU_BLOCK), :]
        b_cum = b_s[d, pl.ds(t0, LRU_BLOCK), :]
        pieces = [None] * n_groups
        for gi in (range(n_groups) if d == 0 else reversed(range(n_groups))):
            rows = slice(gi * SUBLANES, (gi + 1) * SUBLANES)
            hg = b_cum[rows, :] + a_cum[rows, :] * carry
            carry = hg[SUBLANES - 1:SUBLANES, :] if d == 0 else hg[0:1, :]
            pieces[gi] = hg
        return t0, jnp.concatenate(pieces, axis=0), carry

    def forward(j, carry):
        t0, h, carry = scan_block(0, j, carry)
        acc[pl.ds(t0, LRU_BLOCK), :] = h
        return carry

    def backward(jj, carry):
        t0, h, carry = scan_block(1, nb - 1 - jj, carry)
        z = xz_ref[pl.ds(t0, LRU_BLOCK), L_W:2 * L_W].astype(F32)
        y_ref[pl.ds(t0, LRU_BLOCK), :] = ((acc[pl.ds(t0, LRU_BLOCK), :] + h) * _silu(z)).astype(y_ref.dtype)
        return carry

    hfin_ref[0] = lax.fori_loop(0, nb, forward, h0_ref[0])
    hfin_ref[1] = lax.fori_loop(0, nb, backward, h0_ref[1])


def _lru(xz, conv, conv_b, wbd, bias, lam, h0):
    bsz, t, _ = xz.shape
    st = lambda shape: pl.BlockSpec((None,) + shape, lambda b: (b,) + (0,) * len(shape))
    return pl.pallas_call(
        _lru_kernel,
        grid=(bsz,),
        in_specs=[_seq_spec(None, t, 2 * L_W, False), _full_spec((CONV_W, L_W)), _full_spec((1, L_W)),
                  _full_spec((L_W, 4 * L_W)), _full_spec((1, 4 * L_W)), _full_spec((1, 2 * L_W)),
                  st((2, 1, L_W))],
        out_specs=[pl.BlockSpec((None, t, L_W), lambda b: (b, 0, 0)), st((2, 1, L_W))],
        out_shape=[jax.ShapeDtypeStruct((bsz, t, L_W), BF16),
                   jax.ShapeDtypeStruct((bsz, 2, 1, L_W), F32)],
        scratch_shapes=[pltpu.VMEM((2, t, L_W), F32), pltpu.VMEM((2, t, L_W), F32),
                        pltpu.VMEM((t, L_W), F32)],
        compiler_params=pltpu.CompilerParams(dimension_semantics=("arbitrary",),
                                             vmem_limit_bytes=VMEM_LIMIT),
        name="lru",
    )(xz, conv, conv_b, wbd, bias, lam, h0)


def _permute_w_in(w_in_l):
    valid = jnp.asarray(_GATE_SRC >= 0)
    gates = w_in_l[:, 5 * HW:M_COLS]
    ge = jnp.where(valid, gates[:, np.maximum(_GATE_SRC, 0)], 0.0)
    go = jnp.where(valid, gates[:, np.maximum(_GATE_SRC, 0) + 1], 0.0)
    return jnp.concatenate([w_in_l[:, 0:5 * HW], ge, go, w_in_l[:, M_COLS:]], axis=1).astype(BF16)


def _gate_source_index():
    src = np.full((LANES,), -1, np.int32)
    for dd in range(2):
        for is_f in range(2):
            for p in range(N_PAIRS):
                src[_gate_col(dd, is_f, p)] = is_f * 2 * N_HEADS + dd * N_HEADS + 2 * p
    return src


_GATE_SRC = _gate_source_index()


def _gate_bias(m_bi_l, m_bf_l):
    flat = jnp.concatenate([m_bi_l.reshape(-1), m_bf_l.reshape(-1)])
    idx = np.maximum(_GATE_SRC, 0)
    valid = jnp.asarray(_GATE_SRC >= 0)
    return jnp.stack([jnp.where(valid, flat[idx], 0.0), jnp.where(valid, flat[idx + 1], 0.0)])


def _block_diag(blocks):
    n, a, b = blocks.shape
    eye = jnp.eye(n, dtype=blocks.dtype)
    return (eye[:, None, :, None] * blocks[:, :, None, :]).reshape(n * a, n * b)


def _layer_params(l, g_pre, g_post, w_in, w_out, m_bi, m_bf, m_norm, r_mu, r_w0, r_w2, r_a0, r_a2,
                  r_kk, r_ka, r_rk, r_norm, l_conv, l_conv_b, l_wa, l_ba, l_wx, l_bx, l_lambda):
    row = lambda v: v.reshape(1, -1)
    return dict(
        g_pre=row(g_pre[l]), g_post=row(g_post[l]),
        w_in=_permute_w_in(w_in[l]), w_out=w_out[l].astype(BF16),
        gate_bias=_gate_bias(m_bi[l], m_bf[l]), m_norm=row(m_norm[l]),
        r_mu=row(r_mu[l]), r_w0=row(r_w0[l]), r_a0=row(r_a0[l]),
        r_w2=_block_diag(r_w2[l]).astype(BF16), r_a2=_block_diag(r_a2[l]).astype(BF16),
        r_kk=row(r_kk[l]), r_ka=row(r_ka[l]), r_rk=row(r_rk[l]), r_norm=row(r_norm[l]),
        l_conv=l_conv[l], l_conv_b=row(l_conv_b[l]),
        l_w=jnp.concatenate([_block_diag(l_wa[l][0]), _block_diag(l_wx[l][0]),
                             _block_diag(l_wa[l][1]), _block_diag(l_wx[l][1])], axis=1).astype(BF16),
        l_b=jnp.concatenate([l_ba[l][0], l_bx[l][0], l_ba[l][1], l_bx[l][1]]).reshape(1, -1),
        l_lambda=row(l_lambda[l]),
    )


def _trunk_layer(x, mod, lp, states, want_state, per_seq_mod, grid_shift):
    bsz = x.shape[0]
    qkv, oz, ge, go, rs, rz, lxz = _inproj(x, mod, lp['g_pre'], lp['w_in'], lp['r_mu'], lp['gate_bias'],
                                           per_seq_mod, grid_shift)
    m_in = None if states is None else states[0:3]
    r_in = None if states is None else states[3]
    l_in = jnp.zeros((bsz, 2, 1, L_W), F32) if states is None else states[4][:, :, None, :]
    y_m, m_out = _mlstm(qkv, oz, ge, go, lp['m_norm'], m_in, want_state)
    y_r, r_out = _rwkv(rs, rz, lp['r_w0'], lp['r_a0'], lp['r_w2'], lp['r_a2'], lp['r_kk'],
                       lp['r_ka'], lp['r_rk'], lp['r_norm'], r_in, want_state)
    y_l, l_out = _lru(lxz, lp['l_conv'], lp['l_conv_b'], lp['l_w'], lp['l_b'], lp['l_lambda'], l_in)
    y = _outproj(y_m, y_r, y_l, x, mod, lp['g_post'], lp['w_out'], per_seq_mod)
    new_states = m_out + (r_out, l_out[:, :, 0, :]) if want_state else None
    return y, new_states


def kernel(x_prompt, x_sample, c, state_mlstm_C, state_mlstm_n, state_mlstm_m, state_rwkv, state_rglru, c_ctx, g_pre, g_post, w_mod, b_mod, w_in, w_out, m_bi, m_bf, m_norm, r_mu, r_w0, r_w2, r_a0, r_a2, r_kk, r_ka, r_rk, r_norm, l_conv, l_conv_b, l_wa, l_ba, l_wx, l_bx, l_lambda):
    depth = w_in.shape[0]
    bp = x_prompt.shape[0]
    bs = x_sample.shape[0]
    d = x_prompt.shape[-1]
    rows = -(-(1 + bs) // SUBLANES) * SUBLANES
    cc = jnp.zeros((rows, d), F32).at[0].set(c_ctx).at[1:1 + bs].set(c)
    mod = _modulation(cc, w_mod, b_mod)
    xp, xs = x_prompt, x_sample
    new_states = []
    for l in range(depth):
        lp = _layer_params(l, g_pre, g_post, w_in, w_out, m_bi, m_bf, m_norm, r_mu, r_w0, r_w2, r_a0,
                           r_a2, r_kk, r_ka, r_rk, r_norm, l_conv, l_conv_b, l_wa, l_ba, l_wx, l_bx,
                           l_lambda)
        xp, st = _trunk_layer(xp, mod[l, 0:1][:, None, :], lp, None, True, False, False)
        new_states.append(st)
        cache = (state_mlstm_C[:, l].astype(F32), state_mlstm_n[:, l].astype(F32),
                 state_mlstm_m[:, l].astype(F32), state_rwkv[:, l].astype(F32),
                 state_rglru[:, l].astype(F32))
        xs, _ = _trunk_layer(xs, mod[l, 1:1 + bs][:, None, :], lp, cache, False, True, True)
    stacked = tuple(jnp.stack([st[i] for st in new_states], axis=1) for i in range(5))
    return (xp, xs) + stacked
```

```python
import functools

import numpy as np
import jax
import jax.numpy as jnp
from jax import lax
from jax.experimental import pallas as pl
from jax.experimental.pallas import tpu as pltpu

F32 = jnp.float32
BF16 = jnp.bfloat16
HIGHEST = lax.Precision.HIGHEST

D_MODEL = 1024
EPS = 1e-6
HD = 64
N_HEADS = 6
N_PAIRS = N_HEADS // 2
HW = N_HEADS * HD
CHUNK = 64
GRID_W = 64
LORA = 64
R_SHIFT = 3 * HW + 4 * LORA
L_W = 256
L_BLOCKS = 4
CONV_W = 4
LRU_C = 8.0
RWKV_DECAY_SCALE = 0.6065306597126334
M_COLS = 5 * HW + 4 * N_HEADS
IN_COLS = M_COLS + R_SHIFT + HW + 2 * L_W

LANES = 128
SUBLANES = 8
HALO = 16
HALO_ROWS = GRID_W
LRU_BLOCK = 128
VMEM_LIMIT = 56 * 1024 * 1024
MIXER_VMEM_BUDGET = 36 * 1024 * 1024

SEG_WIDTHS = (3 * HW, 2 * HW, LANES, R_SHIFT, HW, 2 * L_W)
SEG_DTYPES = (BF16, BF16, F32, BF16, BF16, BF16)
SEG_OFFS = tuple(int(v) for v in np.cumsum((0,) + SEG_WIDTHS))
IN_COLS_PAD = SEG_OFFS[-1]
N_TILE = 512
IN_TILE = 512


def _gate_col(d, is_f, p):
    return 8 * d + 4 * is_f + p


def _bdot(a, b):
    return jnp.dot(a.astype(BF16), b.astype(BF16), preferred_element_type=F32)


def _bdot_nt(a, b):
    return lax.dot_general(a.astype(BF16), b.astype(BF16), (((1,), (1,)), ((), ())),
                           preferred_element_type=F32)


def _bdot_tn(a, b):
    return lax.dot_general(a.astype(BF16), b.astype(BF16), (((0,), (0,)), ((), ())),
                           preferred_element_type=F32)


def _fdot(a, b):
    return jnp.dot(a, b, precision=HIGHEST, preferred_element_type=F32)


def _split_bf16(x, terms):
    out = []
    for _ in range(terms - 1):
        hi = x.astype(BF16)
        out.append(hi)
        x = x - hi.astype(F32)
    out.append(x.astype(BF16))
    return out


def _cum_dot(cum_rep, x, terms=3):
    return jnp.dot(cum_rep, jnp.concatenate(_split_bf16(x, terms), axis=0), preferred_element_type=F32)


def _head_sum(x, ones_rep, terms=2):
    outs = []
    for p in range(x.shape[1] // LANES):
        parts = _split_bf16(x[:, p * LANES:(p + 1) * LANES], terms)
        outs.append(jnp.dot(jnp.concatenate(parts, axis=1), ones_rep, preferred_element_type=F32))
    return outs[0] if len(outs) == 1 else jnp.concatenate(outs, axis=1)


def _sigmoid(x):
    return 0.5 * jnp.tanh(0.5 * x) + 0.5


def _silu(x):
    return x * _sigmoid(x)


def _softplus(x):
    return jnp.maximum(x, 0.0) + jnp.log1p(jnp.exp(-jnp.abs(x)))


def _log_sigmoid(x):
    return -_softplus(-x)


def _iota2(shape, dim):
    return lax.broadcasted_iota(jnp.int32, shape, dim)


def _head_ones(n):
    return (_iota2((n, n), 0) // HD == _iota2((n, n), 1) // HD).astype(F32)


def _bd_stack(x, lo):
    return jnp.concatenate([jnp.where(lo, x, 0.0), jnp.where(lo, 0.0, x)], axis=0)


def _pair_col(q, j, lo):
    return jnp.where(lo, q[0:CHUNK, j:j + 1], q[0:CHUNK, HD + j:HD + j + 1])


def _pair_row(qt, j, lo_row):
    return jnp.where(lo_row, qt[j:j + 1, :], qt[HD + j:HD + j + 1, :])


def _mod_kernel(c_ref, w_ref, b_ref, o_ref):
    o_ref[...] = _fdot(_silu(c_ref[...]), w_ref[...]) + b_ref[...]


def _modulation(cc, w_mod, b_mod):
    depth, d, n = w_mod.shape
    rows = cc.shape[0]
    return pl.pallas_call(
        _mod_kernel,
        grid=(depth, n // N_TILE),
        in_specs=[pl.BlockSpec((rows, d), lambda l, j: (0, 0)),
                  pl.BlockSpec((None, d, N_TILE), lambda l, j: (l, 0, j)),
                  pl.BlockSpec((None, 1, N_TILE), lambda l, j: (l, 0, j))],
        out_specs=pl.BlockSpec((None, rows, N_TILE), lambda l, j: (l, 0, j)),
        out_shape=jax.ShapeDtypeStruct((depth, rows, n), F32),
        compiler_params=pltpu.CompilerParams(dimension_semantics=("arbitrary", "arbitrary")),
        name="modulation",
    )(cc, w_mod, b_mod.reshape(depth, 1, n))


SEG_QKV, SEG_OZ, SEG_GATES, SEG_RS, SEG_RZ, SEG_LXZ = range(6)


def _inproj_kernel(x_ref, xprev_ref, xnext_ref, mod_ref, g_ref, w_ref, mu_ref, gb_ref, *refs,
                   grid_shift, seq_len):
    tm, d = x_ref.shape
    single_tile = tm >= seq_len
    span = seq_len if single_tile else tm
    i = pl.program_id(1)
    n_tiles = pl.num_programs(1)
    mod = mod_ref[...]
    g = g_ref[...]

    def norm_mod(x):
        h = x * lax.rsqrt(jnp.mean(x * x, axis=-1, keepdims=True) + EPS) * g
        return (h * (1.0 + mod[:, d:2 * d]) + mod[:, 0:d]).astype(BF16)

    out_refs, h_s = refs[:-1], refs[-1]
    off = 0 if single_tile else HALO_ROWS
    h_s[off:off + tm, :] = norm_mod(x_ref[...])
    if not single_tile:
        h_s[0:off, :] = norm_mod(xprev_ref[...])
        h_s[off + tm:, :] = norm_mod(xnext_ref[...])
    has_prev = i > 0
    has_next = i < n_tiles - 1
    row = _iota2((tm, 1), 0) % span
    a, b = SEG_OFFS[SEG_RS], SEG_OFFS[SEG_RS + 1]
    o_ref = out_refs[SEG_RS]
    for n0 in range(a, b, N_TILE):
        n1 = min(n0 + N_TILE, b)
        u_all = jnp.dot(h_s[...], w_ref[:, n0:n1], preferred_element_type=F32)
        if single_tile:
            u = u_all
            u_prev = u_next = jnp.zeros((HALO_ROWS, n1 - n0), F32)
        else:
            u = u_all[HALO_ROWS:HALO_ROWS + tm]
            u_prev = jnp.where(has_prev, u_all[0:HALO_ROWS], 0.0)
            u_next = jnp.where(has_next, u_all[HALO_ROWS + tm:], 0.0)
        left = pltpu.roll(u, 1, 0)
        right = pltpu.roll(u, tm - 1, 0)
        if grid_shift:
            left = jnp.where(row % GRID_W == 0, 0.0, left)
            right = jnp.where(row % GRID_W == GRID_W - 1, 0.0, right)
            up = jnp.concatenate([u_prev, u[0:tm - GRID_W]], axis=0)
            down = jnp.concatenate([u[GRID_W:tm], u_next], axis=0)
            if tm > span:
                up = jnp.where(row < GRID_W, 0.0, up)
                down = jnp.where(row >= span - GRID_W, 0.0, down)
            sh = 0.25 * (up + down + left + right)
        else:
            left = jnp.where(row == 0, u_prev[HALO_ROWS - 1:HALO_ROWS, :], left)
            right = jnp.where(row == span - 1, u_next[0:1, :], right)
            sh = 0.5 * (left + right)
        o_ref[:, n0 - a:n1 - a] = (u + mu_ref[:, n0 - a:n1 - a] * (sh - u)).astype(o_ref.dtype)

    is_f = (_iota2((1, LANES), 1) % 8) >= 4
    a, b = SEG_OFFS[SEG_GATES], SEG_OFFS[SEG_GATES + 1]
    pre = jnp.dot(h_s[off:off + tm, :], w_ref[:, a:b], preferred_element_type=F32) + gb_ref[...]
    out_refs[SEG_GATES][...] = jnp.where(is_f, _log_sigmoid(pre), pre)

    for seg in (SEG_QKV, SEG_OZ, SEG_RZ, SEG_LXZ):
        o_ref, a, b = out_refs[seg], SEG_OFFS[seg], SEG_OFFS[seg + 1]
        for n0 in range(a, b, N_TILE):
            n1 = min(n0 + N_TILE, b)
            o_ref[:, n0 - a:n1 - a] = jnp.dot(h_s[off:off + tm, :], w_ref[:, n0:n1],
                                              preferred_element_type=F32).astype(o_ref.dtype)


def _inproj(x, mod, g_pre, w_in_p, r_mu, gate_bias, per_seq_mod, grid_shift):
    bsz0, t0, d = x.shape
    group = 1
    if not per_seq_mod and t0 < IN_TILE and IN_TILE % t0 == 0 and bsz0 % (IN_TILE // t0) == 0:
        group = IN_TILE // t0
    x = x.reshape(bsz0 // group, group * t0, d)
    bsz, t, _ = x.shape
    tm = min(t, IN_TILE)
    hb = tm // HALO_ROWS
    n_halo = t // HALO_ROWS
    mod_idx = (lambda b, i: (b, 0, 0)) if per_seq_mod else (lambda b, i: (0, 0, 0))
    outs = pl.pallas_call(
        functools.partial(_inproj_kernel, grid_shift=grid_shift, seq_len=t0),
        grid=(bsz, t // tm),
        in_specs=[pl.BlockSpec((None, tm, d), lambda b, i: (b, i, 0)),
                  pl.BlockSpec((None, HALO_ROWS, d), lambda b, i: (b, jnp.maximum(i * hb - 1, 0), 0)),
                  pl.BlockSpec((None, HALO_ROWS, d),
                               lambda b, i: (b, jnp.minimum((i + 1) * hb, n_halo - 1), 0)),
                  pl.BlockSpec((None, 1, 3 * d), mod_idx),
                  pl.BlockSpec((1, d), lambda b, i: (0, 0)),
                  pl.BlockSpec((d, IN_COLS_PAD), lambda b, i: (0, 0), pipeline_mode=pl.Buffered(1)),
                  pl.BlockSpec((1, R_SHIFT), lambda b, i: (0, 0)),
                  pl.BlockSpec((1, LANES), lambda b, i: (0, 0))],
        out_specs=[pl.BlockSpec((None, tm, w), lambda b, i: (b, i, 0)) for w in SEG_WIDTHS],
        out_shape=[jax.ShapeDtypeStruct((bsz, t, w), dt) for w, dt in zip(SEG_WIDTHS, SEG_DTYPES)],
        scratch_shapes=[pltpu.VMEM((tm if t == tm else tm + 2 * HALO_ROWS, d), BF16)],
        compiler_params=pltpu.CompilerParams(dimension_semantics=("arbitrary", "arbitrary"),
                                             vmem_limit_bytes=VMEM_LIMIT),
        name="inproj",
    )(x, x, x, mod, g_pre, w_in_p, r_mu, gate_bias)
    return [o.reshape(bsz0, t0, o.shape[-1]) for o in outs]


def _outproj_kernel(ym_ref, yr_ref, yl_ref, x_ref, mod_ref, g_ref, w_ref, o_ref):
    d = x_ref.shape[-1]
    o = jnp.dot(ym_ref[...], w_ref[0:HW, :], preferred_element_type=F32)
    o = o + jnp.dot(yr_ref[...], w_ref[HW:2 * HW, :], preferred_element_type=F32)
    o = o + jnp.dot(yl_ref[...], w_ref[2 * HW:2 * HW + L_W, :], preferred_element_type=F32)
    on = o * lax.rsqrt(jnp.mean(o * o, axis=-1, keepdims=True) + EPS) * g_ref[...]
    o_ref[...] = x_ref[...] + mod_ref[:, 2 * d:3 * d] * on


def _outproj(ym, yr, yl, x, mod, g_post, w_out_b, per_seq_mod):
    bsz0, t0, d = x.shape
    group = 1
    if not per_seq_mod and t0 < IN_TILE and IN_TILE % t0 == 0 and bsz0 % (IN_TILE // t0) == 0:
        group = IN_TILE // t0
    ym, yr, yl, x = (a.reshape(bsz0 // group, group * t0, a.shape[-1]) for a in (ym, yr, yl, x))
    bsz, t, _ = x.shape
    tm = min(t, IN_TILE)
    mod_idx = (lambda b, i: (b, 0, 0)) if per_seq_mod else (lambda b, i: (0, 0, 0))
    tok = lambda w: pl.BlockSpec((None, tm, w), lambda b, i: (b, i, 0))
    out = pl.pallas_call(
        _outproj_kernel,
        grid=(bsz, t // tm),
        in_specs=[tok(HW), tok(HW), tok(L_W), tok(d),
                  pl.BlockSpec((None, 1, 3 * d), mod_idx),
                  pl.BlockSpec((1, d), lambda b, i: (0, 0)),
                  pl.BlockSpec((2 * HW + L_W, d), lambda b, i: (0, 0))],
        out_specs=tok(d),
        out_shape=jax.ShapeDtypeStruct((bsz, t, d), F32),
        compiler_params=pltpu.CompilerParams(dimension_semantics=("arbitrary", "arbitrary"),
                                             vmem_limit_bytes=VMEM_LIMIT),
        name="outproj",
    )(ym, yr, yl, x, mod, g_post, w_out_b)
    return out.reshape(bsz0, t0, d)


def _mlstm_kernel(*refs, has_state_in, has_state_out):
    qkv_ref, oz_ref, g_ref, ng_ref = refs[0:4]
    refs = refs[4:]
    if has_state_in:
        c0_ref, n0_ref, m0_ref = refs[0:3]
        refs = refs[3:]
    y_ref = refs[0]
    refs = refs[1:]
    if has_state_out:
        c_out_ref, n_out_ref, m_out_ref = refs[0:3]
        refs = refs[3:]
    hacc, cn_s, m_s = refs
    ns, t = qkv_ref.shape[0], qkv_ref.shape[1]
    nc = t // CHUNK
    lane = _iota2((CHUNK, LANES), 1)
    row = _iota2((CHUNK, LANES), 0)
    lo = lane < HD
    s_idx = lane % HD
    causal = (s_idx <= row, s_idx >= row)
    r2 = _iota2((2 * CHUNK, 2 * CHUNK), 0)
    c2 = _iota2((2 * CHUNK, 2 * CHUNK), 1)
    same = (r2 // CHUNK) == (c2 // CHUNK)
    cum = ((same & (c2 <= r2)).astype(BF16), (same & (c2 >= r2)).astype(BF16))
    cum_rep = tuple(jnp.concatenate([m, m, m], axis=1) for m in cum)
    bd_ones = same.astype(F32)
    ones_rep = jnp.concatenate([same.astype(BF16), same.astype(BF16)], axis=0)
    bd_mask2 = jnp.concatenate([same, same], axis=1)
    colid = _iota2((1, LANES), 1)
    is_f = (colid % 8) >= 4

    hacc[...] = jnp.zeros_like(hacc)
    cn_s[...] = jnp.zeros_like(cn_s)
    m_s[...] = jnp.zeros_like(m_s)
    if has_state_in:
        m_s[...] = m0_ref[...]
        for s in range(ns):
            for d in range(2):
                for p in range(N_PAIRS):
                    for h in range(2):
                        cn_s[s, d, p, h * HD:(h + 1) * HD, h * HD:(h + 1) * HD] = c0_ref[s, d, 2 * p + h]
                    n_rows = jnp.where(same, jnp.broadcast_to(n0_ref[s, d, p], (2 * CHUNK, LANES)), 0.0)
                    cn_s[s, d, p, :, LANES:2 * LANES] = n_rows.T

    def gate_table(s, t0, d):
        gates = g_ref[s, pl.ds(t0, CHUNK), :]
        x = jnp.concatenate([gates, gates], axis=0)
        q = jnp.where(is_f, _cum_dot(cum_rep[d], x), x)
        return q, q.T

    chains = [(s, d, p) for s in range(ns) for d in range(2) for p in range(N_PAIRS)]

    def body(j, carry):
        t0s = (pl.multiple_of(j * CHUNK, CHUNK), pl.multiple_of((nc - 1 - j) * CHUNK, CHUNK))
        tabs = {(s, d): gate_table(s, t0s[d], d) for s in range(ns) for d in range(2)}
        st = []
        for s, d, p in chains:
            t0 = t0s[d]
            q_tab, q_tab_t = tabs[(s, d)]
            ji, jf = _gate_col(d, 0, p), _gate_col(d, 1, p)
            c = dict(t0=t0, sl=slice(p * LANES, (p + 1) * LANES))
            c['bcol'] = _pair_col(q_tab, jf, lo)
            c['icol'] = _pair_col(q_tab, ji, lo)
            brow = _pair_row(q_tab_t, jf, lo[0:1, :])
            irow = _pair_row(q_tab_t, ji, lo[0:1, :])
            c['q'] = qkv_ref[s, pl.ds(t0, CHUNK), p * LANES:(p + 1) * LANES].astype(F32)
            c['k'] = qkv_ref[s, pl.ds(t0, CHUNK), HW + p * LANES:HW + (p + 1) * LANES].astype(F32) * (HD ** -0.5)
            c['v'] = qkv_ref[s, pl.ds(t0, CHUNK), 2 * HW + p * LANES:2 * HW + (p + 1) * LANES].astype(F32)
            c['mprev'] = m_s[s, d, p]
            c['cn'] = cn_s[s, d, p]
            dmat = jnp.where(causal[d], c['bcol'] - brow + irow, -jnp.inf)
            inter = c['bcol'] + c['mprev']
            mx = jnp.where(lo,
                           jnp.max(jnp.where(lo, dmat, -jnp.inf), axis=1, keepdims=True),
                           jnp.max(jnp.where(lo, -jnp.inf, dmat), axis=1, keepdims=True))
            c['m_t'] = jnp.maximum(inter, mx)
            c['pexp'] = jnp.exp(dmat - c['m_t'])
            c['sc'] = jnp.exp(inter - c['m_t'])
            last = CHUNK - 1 if d == 0 else 0
            b_last = c['bcol'][last:last + 1, :]
            g = b_last - c['bcol'] + c['icol']
            c['m_new'] = jnp.maximum(b_last + c['mprev'], jnp.max(g, axis=0, keepdims=True))
            c['dec'] = jnp.exp(b_last + c['mprev'] - c['m_new'])
            c['wk'] = jnp.exp(g - c['m_new'])
            st.append(c)
        for c in st:
            c['s'] = _bdot_nt(c['q'], _bd_stack(c['k'], lo)) * c['pexp']
        for c in st:
            c['a1'] = _bdot(c['q'], c['cn'])
        for c in st:
            c['upd'] = _bdot_tn(c['wk'] * c['k'], jnp.concatenate([c['v'], jnp.ones_like(c['v'])], axis=1))
        for c in st:
            c['a2'] = _bdot(c['s'], jnp.concatenate([_bd_stack(c['v'], lo), bd_ones], axis=1))
        for (s, d, p), c in zip(chains, st):
            num = c['sc'] * c['a1'][:, 0:LANES] + c['a2'][:, 0:LANES]
            den = c['sc'] * c['a1'][:, LANES:] + c['a2'][:, LANES:]
            h = num / jnp.maximum(jnp.abs(den), jnp.exp(-c['m_t']))
            hacc[s, pl.ds(c['t0'], CHUNK), c['sl']] = hacc[s, pl.ds(c['t0'], CHUNK), c['sl']] + h
            cn_s[s, d, p] = (jnp.concatenate([c['dec'], c['dec']], axis=1) * c['cn']
                             + jnp.where(bd_mask2, c['upd'], 0.0))
            m_s[s, d, p] = c['m_new']
        return carry

    lax.fori_loop(0, nc, body, 0)
    if has_state_out:
        m_out_ref[...] = m_s[...]
        lo2 = _iota2((1, LANES), 1) < HD
        for s in range(ns):
            for d in range(2):
                for p in range(N_PAIRS):
                    for h in range(2):
                        c_out_ref[s, d, 2 * p + h] = cn_s[s, d, p, h * HD:(h + 1) * HD, h * HD:(h + 1) * HD]
                    n_t = cn_s[s, d, p, :, LANES:2 * LANES].T
                    n_out_ref[s, d, p] = jnp.where(lo2, n_t[0:1, :], n_t[HD:HD + 1, :])

    ng = ng_ref[...]
    rb = min(t, 256)

    def epilogue(i, carry):
        r0 = pl.multiple_of(i * rb, rb)
        for s in range(ns):
            h = hacc[s, pl.ds(r0, rb), :]
            hn = h * lax.rsqrt(_head_sum(h * h, ones_rep) * (1.0 / HD) + EPS) * ng
            o = oz_ref[s, pl.ds(r0, rb), 0:HW].astype(F32)
            z = oz_ref[s, pl.ds(r0, rb), HW:2 * HW].astype(F32)
            y_ref[s, pl.ds(r0, rb), :] = (hn * _sigmoid(o) * _silu(z)).astype(y_ref.dtype)
        return carry

    lax.fori_loop(0, t // rb, epilogue, 0)


def _seq_spec(ns, t, w, single_buffer):
    if single_buffer:
        return pl.BlockSpec((ns, t, w), lambda b: (b, 0, 0), pipeline_mode=pl.Buffered(1))
    return pl.BlockSpec((ns, t, w), lambda b: (b, 0, 0))


def _full_spec(shape):
    n = len(shape)
    return pl.BlockSpec(shape, lambda b: (0,) * n)


def _state_spec(ns, shape):
    return pl.BlockSpec((ns,) + shape, lambda b: (b,) + (0,) * len(shape))


def _seqs_per_step(bsz, bytes_per_seq):
    for ns in (2, 1):
        if bsz % ns == 0 and ns * bytes_per_seq <= MIXER_VMEM_BUDGET:
            return ns
    return 1


def _pair_rows(x):
    return jnp.repeat(x.reshape(x.shape[:2] + (N_PAIRS, 1, 2)), HD, axis=-1)


def _mlstm(qkv, oz, gates, norm_g, state_in, want_state):
    bsz, t, _ = qkv.shape
    big = t > 512
    in_bufs = 1 if big else 2
    ns = _seqs_per_step(bsz, t * ((5 * HW * qkv.dtype.itemsize + LANES * 4) * in_bufs
                                  + HW * 2 * 2 + HW * 4))
    c_shape, row_shape = (2, N_HEADS, HD, HD), (2, N_PAIRS, 1, LANES)
    in_specs = [_seq_spec(ns, t, 3 * HW, big), _seq_spec(ns, t, 2 * HW, big),
                _seq_spec(ns, t, LANES, big), _full_spec((1, HW))]
    args = [qkv, oz, gates, norm_g]
    if state_in is not None:
        c0, n0, m0 = state_in
        in_specs += [_state_spec(ns, c_shape), _state_spec(ns, row_shape), _state_spec(ns, row_shape)]
        args += [c0, n0.reshape((bsz,) + row_shape), _pair_rows(m0)]
    out_specs = [_seq_spec(ns, t, HW, False)]
    out_shape = [jax.ShapeDtypeStruct((bsz, t, HW), BF16)]
    if want_state:
        out_specs += [_state_spec(ns, c_shape), _state_spec(ns, row_shape), _state_spec(ns, row_shape)]
        out_shape += [jax.ShapeDtypeStruct((bsz,) + c_shape, F32),
                      jax.ShapeDtypeStruct((bsz,) + row_shape, F32),
                      jax.ShapeDtypeStruct((bsz,) + row_shape, F32)]
    outs = pl.pallas_call(
        functools.partial(_mlstm_kernel, has_state_in=state_in is not None, has_state_out=want_state),
        grid=(bsz // ns,),
        in_specs=in_specs,
        out_specs=out_specs,
        out_shape=out_shape,
        scratch_shapes=[pltpu.VMEM((ns, t, HW), F32),
                        pltpu.VMEM((ns, 2, N_PAIRS, 2 * CHUNK, 2 * LANES), F32),
                        pltpu.VMEM((ns,) + row_shape, F32)],
        compiler_params=pltpu.CompilerParams(dimension_semantics=("arbitrary",),
                                             vmem_limit_bytes=VMEM_LIMIT),
        name="mlstm",
    )(*args)
    if not want_state:
        return outs[0], None
    y, c1, n1, m1 = outs
    return y, (c1, n1.reshape(bsz, 2, N_HEADS, HD), m1[:, :, :, 0, ::HD].reshape(bsz, 2, N_HEADS))


def _rwkv_kernel(*refs, has_state_in, has_state_out):
    (rs_ref, rz_ref, w0_ref, a0_ref, w2_ref, a2_ref, kk_ref, ka_ref, rk_ref, ng_ref) = refs[0:10]
    refs = refs[10:]
    if has_state_in:
        s0_ref = refs[0]
        refs = refs[1:]
    y_ref = refs[0]
    refs = refs[1:]
    if has_state_out:
        s_out_ref = refs[0]
        refs = refs[1:]
    yacc, bonus, s_s = refs
    ns, t = rs_ref.shape[0], rs_ref.shape[1]
    nc = t // CHUNK
    lane = _iota2((CHUNK, LANES), 1)
    row = _iota2((CHUNK, LANES), 0)
    lo = lane < HD
    s_idx = lane % HD
    strict = (s_idx < row, s_idx > row)
    incl = (s_idx <= row, s_idx >= row)
    eye_pk = (s_idx == row).astype(F32)
    r64 = _iota2((CHUNK, CHUNK), 0)
    c64 = _iota2((CHUNK, CHUNK), 1)
    cum = ((c64 <= r64).astype(BF16), (c64 >= r64).astype(BF16))
    cum_rep = tuple(jnp.concatenate([m, m, m], axis=1) for m in cum)
    r2 = _iota2((2 * CHUNK, 2 * CHUNK), 0)
    c2 = _iota2((2 * CHUNK, 2 * CHUNK), 1)
    bd_mask = (r2 // CHUNK) == (c2 // CHUNK)
    ones_rep = jnp.concatenate([bd_mask.astype(BF16), bd_mask.astype(BF16)], axis=0)
    kk = kk_ref[...]
    ka = ka_ref[...]
    rk = rk_ref[...]

    yacc[...] = jnp.zeros_like(yacc)
    s_s[...] = jnp.zeros_like(s_s)
    if has_state_in:
        for s in range(ns):
            for d in range(2):
                for p in range(N_PAIRS):
                    for h in range(2):
                        s_s[s, d, p, h * HD:(h + 1) * HD, h * HD:(h + 1) * HD] = s0_ref[s, d, 2 * p + h]

    def prep(s, c, d):
        t0 = pl.multiple_of(c * CHUNK, CHUNK)
        blk = rs_ref[s, pl.ds(t0, CHUNK), :].astype(F32)
        r = blk[:, 0:HW]
        k = blk[:, HW:2 * HW]
        v = blk[:, 2 * HW:3 * HW]
        wl = blk[:, 3 * HW:3 * HW + 2 * LORA]
        al = blk[:, 3 * HW + 2 * LORA:3 * HW + 4 * LORA]
        dsl = slice(d * HW, (d + 1) * HW)
        logw = -RWKV_DECAY_SCALE * _sigmoid(w0_ref[:, dsl] + _bdot(jnp.tanh(wl), w2_ref[:, dsl]))
        a = _sigmoid(a0_ref[:, dsl] + _bdot(al, a2_ref[:, dsl]))
        kappa = k * kk
        khat = kappa * lax.rsqrt(jnp.maximum(_head_sum(kappa * kappa, ones_rep), 1e-24))
        kt = k * (1.0 + (a - 1.0) * ka)
        b = khat * a
        if d == 0:
            bonus[s, pl.ds(t0, CHUNK), :] = _head_sum(r * k * rk, ones_rep) * v
        last = CHUNK - 1 if d == 0 else 0
        logp = _cum_dot(cum_rep[d], logw)
        logp_last = logp[last:last + 1, :]
        e_neg = jnp.exp(-logp)
        e_last = jnp.exp(logp_last - logp)
        return dict(t0=t0, v=v, alpha=jnp.exp(logp - logw) * khat, beta=b * e_neg, kap=kt * e_neg,
                    rho=r * jnp.exp(logp), beta_l=b * e_last, kap_l=kt * e_last,
                    p_last=jnp.exp(logp_last))

    chains = [(s, d, p) for s in range(ns) for d in range(2) for p in range(N_PAIRS)]

    def body(j, carry):
        pre = {(s, d): prep(s, j if d == 0 else nc - 1 - j, d) for s in range(ns) for d in range(2)}
        st = []
        for s, d, p in chains:
            sl = slice(p * LANES, (p + 1) * LANES)
            c = {key: (val if key == 't0' else val[:, sl]) for key, val in pre[(s, d)].items()}
            c['sl'] = sl
            st.append(c)
        for c in st:
            c['g'] = _bdot_nt(jnp.concatenate([c['alpha'], c['rho']], axis=0),
                              jnp.concatenate([_bd_stack(c['beta'], lo), _bd_stack(c['kap'], lo)], axis=0))
        for (s, d, p), c in zip(chains, st):
            g = c.pop('g')
            c['n'] = jnp.where(strict[d], g[0:CHUNK, 0:LANES], 0.0)
            c['mk'] = jnp.where(strict[d], g[0:CHUNK, LANES:], 0.0)
            c['mrb'] = jnp.where(incl[d], g[CHUNK:, 0:LANES], 0.0)
            c['mrk'] = jnp.where(incl[d], g[CHUNK:, LANES:], 0.0)
            c['inv'] = eye_pk - c['n']
        for c in st:
            c['pw'] = _bdot(c['n'], _bd_stack(c['n'], lo))
        for c in st:
            c['mkv'] = _bdot(c['mk'], _bd_stack(c['v'], lo))
        for _ in range(4):
            for c in st:
                rp = _bdot(jnp.concatenate([c['inv'], c['pw']], axis=0), _bd_stack(c['pw'], lo))
                c['inv'] = c['inv'] + rp[0:CHUNK]
                c['pw'] = rp[CHUNK:]
        for c in st:
            c['inv'] = c['inv'] + _bdot(c['inv'], _bd_stack(c['pw'], lo))
        for c in st:
            c['wu'] = _bdot(c['inv'], jnp.concatenate([_bd_stack(c['alpha'], lo), _bd_stack(c['mkv'], lo)], axis=1))
        for (s, d, p), c in zip(chains, st):
            c['s_prev'] = s_s[s, d, p]
            ws = _bdot_nt(jnp.concatenate([c['wu'][:, 0:LANES], c['rho']], axis=0), c['s_prev'])
            c['u'] = ws[0:CHUNK] + c['wu'][:, LANES:]
            c['ys'] = ws[CHUNK:]
        for c in st:
            c['upd'] = _bdot_tn(jnp.concatenate([c['v'], -c['u']], axis=0),
                                jnp.concatenate([c['kap_l'], c['beta_l']], axis=0))
        for c in st:
            c['y'] = c['ys'] + _bdot(jnp.concatenate([c['mrk'], -c['mrb']], axis=1),
                                     jnp.concatenate([_bd_stack(c['v'], lo), _bd_stack(c['u'], lo)], axis=0))
        for (s, d, p), c in zip(chains, st):
            s_s[s, d, p] = c['s_prev'] * c['p_last'] + jnp.where(bd_mask, c['upd'], 0.0)
            yacc[s, pl.ds(c['t0'], CHUNK), c['sl']] = yacc[s, pl.ds(c['t0'], CHUNK), c['sl']] + c['y']
        return carry

    lax.fori_loop(0, nc, body, 0)
    if has_state_out:
        for s in range(ns):
            for d in range(2):
                for p in range(N_PAIRS):
                    for h in range(2):
                        s_out_ref[s, d, 2 * p + h] = s_s[s, d, p, h * HD:(h + 1) * HD, h * HD:(h + 1) * HD]

    ng = ng_ref[...]
    rb = min(t, 256)

    def epilogue(i, carry):
        r0 = pl.multiple_of(i * rb, rb)
        for s in range(ns):
            y = yacc[s, pl.ds(r0, rb), :]
            yn = (y * lax.rsqrt(_head_sum(y * y, ones_rep) * (1.0 / HD) + EPS) * ng
                  + bonus[s, pl.ds(r0, rb), :])
            y_ref[s, pl.ds(r0, rb), :] = (yn * _silu(rz_ref[s, pl.ds(r0, rb), :].astype(F32))).astype(y_ref.dtype)
        return carry

    lax.fori_loop(0, t // rb, epilogue, 0)


def _rwkv(rs, rz, w0, a0, w2bd, a2bd, kk, ka, rk, norm_g, state_in, want_state):
    bsz, t, _ = rs.shape
    big = t > 512
    in_bufs = 1 if big else 2
    itemsize = rs.dtype.itemsize
    ns = _seqs_per_step(bsz, t * ((R_SHIFT + HW) * itemsize * in_bufs + HW * 2 * 2 + 2 * HW * 4))
    s_shape = (2, N_HEADS, HD, HD)
    in_specs = [_seq_spec(ns, t, R_SHIFT, big), _seq_spec(ns, t, HW, big),
                _full_spec((1, 2 * HW)), _full_spec((1, 2 * HW)),
                _full_spec((2 * LORA, 2 * HW)), _full_spec((2 * LORA, 2 * HW)),
                _full_spec((1, HW)), _full_spec((1, HW)), _full_spec((1, HW)), _full_spec((1, HW))]
    args = [rs, rz, w0, a0, w2bd, a2bd, kk, ka, rk, norm_g]
    if state_in is not None:
        in_specs.append(_state_spec(ns, s_shape))
        args.append(state_in)
    out_specs = [_seq_spec(ns, t, HW, False)]
    out_shape = [jax.ShapeDtypeStruct((bsz, t, HW), BF16)]
    if want_state:
        out_specs.append(_state_spec(ns, s_shape))
        out_shape.append(jax.ShapeDtypeStruct((bsz,) + s_shape, F32))
    outs = pl.pallas_call(
        functools.partial(_rwkv_kernel, has_state_in=state_in is not None, has_state_out=want_state),
        grid=(bsz // ns,),
        in_specs=in_specs,
        out_specs=out_specs,
        out_shape=out_shape,
        scratch_shapes=[pltpu.VMEM((ns, t, HW), F32), pltpu.VMEM((ns, t, HW), F32),
                        pltpu.VMEM((ns, 2, N_PAIRS, 2 * CHUNK, LANES), F32)],
        compiler_params=pltpu.CompilerParams(dimension_semantics=("arbitrary",),
                                             vmem_limit_bytes=VMEM_LIMIT),
        name="rwkv",
    )(*args)
    return (outs[0], outs[1]) if want_state else (outs[0], None)


def _lru_kernel(xz_ref, conv_ref, cb_ref, w_ref, b_ref, lam_ref, h0_ref, y_ref, hfin_ref, a_s, b_s, acc):
    t = xz_ref.shape[0]
    nb = t // LRU_BLOCK
    n_groups = LRU_BLOCK // SUBLANES
    row = _iota2((LRU_BLOCK, L_W), 0)
    sub = _iota2((n_groups, SUBLANES, L_W), 1)
    conv = conv_ref[...]
    cb = cb_ref[...]
    sp = _softplus(-lam_ref[...])

    def coefficients(j, carry):
        t0 = pl.multiple_of(j * LRU_BLOCK, LRU_BLOCK)
        x = xz_ref[pl.ds(t0, LRU_BLOCK), 0:L_W].astype(F32)
        prev_rows = xz_ref[pl.ds(pl.multiple_of(jnp.maximum(t0 - HALO, 0), HALO), HALO), 0:L_W]
        next_rows = xz_ref[pl.ds(pl.multiple_of(jnp.minimum(t0 + LRU_BLOCK, t - HALO), HALO), HALO), 0:L_W]
        prev_rows = jnp.where(j > 0, prev_rows.astype(F32), 0.0)
        next_rows = jnp.where(j < nb - 1, next_rows.astype(F32), 0.0)
        p1, p2 = prev_rows[HALO - 1:HALO, :], prev_rows[HALO - 2:HALO - 1, :]
        xm1 = jnp.where(row == 0, p1, pltpu.roll(x, 1, 0))
        xm2 = jnp.where(row == 0, p2, jnp.where(row == 1, p1, pltpu.roll(x, 2, 0)))
        xp1 = jnp.where(row == LRU_BLOCK - 1, next_rows[0:1, :], pltpu.roll(x, LRU_BLOCK - 1, 0))
        xc = conv[0:1, :] * xm2 + conv[1:2, :] * xm1 + conv[2:3, :] * x + conv[3:4, :] * xp1 + cb
        pre = _bdot(xc, w_ref[...]) + b_ref[...]
        for d in range(2):
            rg = _sigmoid(pre[:, 2 * d * L_W:(2 * d + 1) * L_W])
            ig = _sigmoid(pre[:, (2 * d + 1) * L_W:(2 * d + 2) * L_W])
            log_a = -LRU_C * rg * sp[:, d * L_W:(d + 1) * L_W]
            a_step = jnp.exp(log_a)
            b_step = jnp.sqrt(jnp.tanh(-log_a) * (1.0 + a_step * a_step)) * ig * xc
            a_cum = a_step.reshape(n_groups, SUBLANES, L_W)
            b_cum = b_step.reshape(n_groups, SUBLANES, L_W)
            k = 1
            while k < SUBLANES:
                keep = (sub >= k) if d == 0 else (sub < SUBLANES - k)
                shift = k if d == 0 else SUBLANES - k
                a_sh = jnp.where(keep, pltpu.roll(a_cum, shift, 1), 1.0)
                b_sh = jnp.where(keep, pltpu.roll(b_cum, shift, 1), 0.0)
                b_cum = a_cum * b_sh + b_cum
                a_cum = a_cum * a_sh
                k *= 2
            a_s[d, pl.ds(t0, LRU_BLOCK), :] = a_cum.reshape(LRU_BLOCK, L_W)
            b_s[d, pl.ds(t0, LRU_BLOCK), :] = b_cum.reshape(LRU_BLOCK, L_W)
        return carry

    lax.fori_loop(0, nb, coefficients, 0)

    def scan_block(d, j, carry):
        t0 = pl.multiple_of(j * LRU_BLOCK, LRU_BLOCK)
        a_cum = a_s[d, pl.ds(t0, LRU_BLOCK), :]
        b_cum = b_s[d, pl.ds(t0, LRU_BLOCK), :]
        pieces = [None] * n_groups
        for gi in (range(n_groups) if d == 0 else reversed(range(n_groups))):
            rows = slice(gi * SUBLANES, (gi + 1) * SUBLANES)
            hg = b_cum[rows, :] + a_cum[rows, :] * carry
            carry = hg[SUBLANES - 1:SUBLANES, :] if d == 0 else hg[0:1, :]
            pieces[gi] = hg
        return t0, jnp.concatenate(pieces, axis=0), carry

    def forward(j, carry):
        t0, h, carry = scan_block(0, j, carry)
        acc[pl.ds(t0, LRU_BLOCK), :] = h
        return carry

    def backward(jj, carry):
        t0, h, carry = scan_block(1, nb - 1 - jj, carry)
        z = xz_ref[pl.ds(t0, LRU_BLOCK), L_W:2 * L_W].astype(F32)
        y_ref[pl.ds(t0, LRU_BLOCK), :] = ((acc[pl.ds(t0, LRU_BLOCK), :] + h) * _silu(z)).astype(y_ref.dtype)
        return carry

    hfin_ref[0] = lax.fori_loop(0, nb, forward, h0_ref[0])
    hfin_ref[1] = lax.fori_loop(0, nb, backward, h0_ref[1])


def _lru(xz, conv, conv_b, wbd, bias, lam, h0):
    bsz, t, _ = xz.shape
    st = lambda shape: pl.BlockSpec((None,) + shape, lambda b: (b,) + (0,) * len(shape))
    return pl.pallas_call(
        _lru_kernel,
        grid=(bsz,),
        in_specs=[_seq_spec(None, t, 2 * L_W, False), _full_spec((CONV_W, L_W)), _full_spec((1, L_W)),
                  _full_spec((L_W, 4 * L_W)), _full_spec((1, 4 * L_W)), _full_spec((1, 2 * L_W)),
                  st((2, 1, L_W))],
        out_specs=[pl.BlockSpec((None, t, L_W), lambda b: (b, 0, 0)), st((2, 1, L_W))],
        out_shape=[jax.ShapeDtypeStruct((bsz, t, L_W), BF16),
                   jax.ShapeDtypeStruct((bsz, 2, 1, L_W), F32)],
        scratch_shapes=[pltpu.VMEM((2, t, L_W), F32), pltpu.VMEM((2, t, L_W), F32),
                        pltpu.VMEM((t, L_W), F32)],
        compiler_params=pltpu.CompilerParams(dimension_semantics=("arbitrary",),
                                             vmem_limit_bytes=VMEM_LIMIT),
        name="lru",
    )(xz, conv, conv_b, wbd, bias, lam, h0)


def _permute_w_in(w_in_l):
    valid = jnp.asarray(_GATE_SRC >= 0)
    gates = jnp.where(valid, w_in_l[:, 5 * HW:M_COLS][:, np.maximum(_GATE_SRC, 0)], 0.0)
    return jnp.concatenate([w_in_l[:, 0:5 * HW], gates, w_in_l[:, M_COLS:]], axis=1).astype(BF16)


def _gate_source_index():
    src = np.full((LANES,), -1, np.int32)
    for dd in range(2):
        for is_f in range(2):
            for p in range(N_PAIRS):
                for parity in range(2):
                    src[parity * HD + _gate_col(dd, is_f, p)] = (is_f * 2 * N_HEADS + dd * N_HEADS
                                                                 + 2 * p + parity)
    return src


_GATE_SRC = _gate_source_index()


def _gate_bias(m_bi_l, m_bf_l):
    flat = jnp.concatenate([m_bi_l.reshape(-1), m_bf_l.reshape(-1)])
    return jnp.where(jnp.asarray(_GATE_SRC >= 0), flat[np.maximum(_GATE_SRC, 0)], 0.0).reshape(1, LANES)


def _block_diag(blocks):
    n, a, b = blocks.shape
    eye = jnp.eye(n, dtype=blocks.dtype)
    return (eye[:, None, :, None] * blocks[:, :, None, :]).reshape(n * a, n * b)


def _layer_params(l, g_pre, g_post, w_in, w_out, m_bi, m_bf, m_norm, r_mu, r_w0, r_w2, r_a0, r_a2,
                  r_kk, r_ka, r_rk, r_norm, l_conv, l_conv_b, l_wa, l_ba, l_wx, l_bx, l_lambda):
    row = lambda v: v.reshape(1, -1)
    return dict(
        g_pre=row(g_pre[l]), g_post=row(g_post[l]),
        w_in=_permute_w_in(w_in[l]), w_out=w_out[l].astype(BF16),
        gate_bias=_gate_bias(m_bi[l], m_bf[l]), m_norm=row(m_norm[l]),
        r_mu=row(r_mu[l]), r_w0=row(r_w0[l]), r_a0=row(r_a0[l]),
        r_w2=_block_diag(r_w2[l]).astype(BF16), r_a2=_block_diag(r_a2[l]).astype(BF16),
        r_kk=row(r_kk[l]), r_ka=row(r_ka[l]), r_rk=row(r_rk[l]), r_norm=row(r_norm[l]),
        l_conv=l_conv[l], l_conv_b=row(l_conv_b[l]),
        l_w=jnp.concatenate([_block_diag(l_wa[l][0]), _block_diag(l_wx[l][0]),
                             _block_diag(l_wa[l][1]), _block_diag(l_wx[l][1])], axis=1).astype(BF16),
        l_b=jnp.concatenate([l_ba[l][0], l_bx[l][0], l_ba[l][1], l_bx[l][1]]).reshape(1, -1),
        l_lambda=row(l_lambda[l]),
    )


def _trunk_layer(x, mod, lp, states, want_state, per_seq_mod, grid_shift):
    bsz = x.shape[0]
    qkv, oz, gates, rs, rz, lxz = _inproj(x, mod, lp['g_pre'], lp['w_in'], lp['r_mu'], lp['gate_bias'],
                                          per_seq_mod, grid_shift)
    m_in = None if states is None else states[0:3]
    r_in = None if states is None else states[3]
    l_in = jnp.zeros((bsz, 2, 1, L_W), F32) if states is None else states[4][:, :, None, :]
    y_m, m_out = _mlstm(qkv, oz, gates, lp['m_norm'], m_in, want_state)
    y_r, r_out = _rwkv(rs, rz, lp['r_w0'], lp['r_a0'], lp['r_w2'], lp['r_a2'], lp['r_kk'],
                       lp['r_ka'], lp['r_rk'], lp['r_norm'], r_in, want_state)
    y_l, l_out = _lru(lxz, lp['l_conv'], lp['l_conv_b'], lp['l_w'], lp['l_b'], lp['l_lambda'], l_in)
    y = _outproj(y_m, y_r, y_l, x, mod, lp['g_post'], lp['w_out'], per_seq_mod)
    new_states = m_out + (r_out, l_out[:, :, 0, :]) if want_state else None
    return y, new_states


def kernel(x_prompt, x_sample, c, state_mlstm_C, state_mlstm_n, state_mlstm_m, state_rwkv, state_rglru, c_ctx, g_pre, g_post, w_mod, b_mod, w_in, w_out, m_bi, m_bf, m_norm, r_mu, r_w0, r_w2, r_a0, r_a2, r_kk, r_ka, r_rk, r_norm, l_conv, l_conv_b, l_wa, l_ba, l_wx, l_bx, l_lambda):
    depth = w_in.shape[0]
    bp = x_prompt.shape[0]
    bs = x_sample.shape[0]
    d = x_prompt.shape[-1]
    rows = -(-(1 + bs) // SUBLANES) * SUBLANES
    cc = jnp.zeros((rows, d), F32).at[0].set(c_ctx).at[1:1 + bs].set(c)
    mod = _modulation(cc, w_mod, b_mod)
    xp, xs = x_prompt, x_sample
    new_states = []
    for l in range(depth):
        lp = _layer_params(l, g_pre, g_post, w_in, w_out, m_bi, m_bf, m_norm, r_mu, r_w0, r_w2, r_a0,
                           r_a2, r_kk, r_ka, r_rk, r_norm, l_conv, l_conv_b, l_wa, l_ba, l_wx, l_bx,
                           l_lambda)
        xp, st = _trunk_layer(xp, mod[l, 0:1][:, None, :], lp, None, True, False, False)
        new_states.append(st)
        cache = (state_mlstm_C[:, l].astype(F32), state_mlstm_n[:, l].astype(F32),
                 state_mlstm_m[:, l].astype(F32), state_rwkv[:, l].astype(F32),
                 state_rglru[:, l].astype(F32))
        xs, _ = _trunk_layer(xs, mod[l, 1:1 + bs][:, None, :], lp, cache, False, True, True)
    stacked = tuple(jnp.stack([st[i] for st in new_states], axis=1) for i in range(5))
    return (xp, xs) + stacked
```

```python
import functools

import numpy as np
import jax
import jax.numpy as jnp
from jax import lax
from jax.experimental import pallas as pl
from jax.experimental.pallas import tpu as pltpu

F32 = jnp.float32
BF16 = jnp.bfloat16
HIGHEST = lax.Precision.HIGHEST

D_MODEL = 1024
EPS = 1e-6
HD = 64
N_HEADS = 6
N_PAIRS = N_HEADS // 2
HW = N_HEADS * HD
CHUNK = 64
GRID_W = 64
LORA = 64
R_SHIFT = 3 * HW + 4 * LORA
L_W = 256
L_BLOCKS = 4
CONV_W = 4
LRU_C = 8.0
RWKV_DECAY_SCALE = 0.6065306597126334
M_COLS = 5 * HW + 4 * N_HEADS
IN_COLS = M_COLS + R_SHIFT + HW + 2 * L_W

LANES = 128
SUBLANES = 8
HALO = 16
HALO_ROWS = GRID_W
LRU_BLOCK = 128
VMEM_LIMIT = 56 * 1024 * 1024
MIXER_VMEM_BUDGET = 36 * 1024 * 1024

SEG_WIDTHS = (3 * HW, 2 * HW, LANES, R_SHIFT, HW, 2 * L_W)
SEG_DTYPES = (BF16, BF16, F32, BF16, BF16, BF16)
SEG_OFFS = tuple(int(v) for v in np.cumsum((0,) + SEG_WIDTHS))
IN_COLS_PAD = SEG_OFFS[-1]
N_TILE = 512
IN_TILE = 512
OUT_TILE = 1024


def _gate_col(d, is_f, p):
    return 8 * d + 4 * is_f + p


def _bdot(a, b):
    return jnp.dot(a.astype(BF16), b.astype(BF16), preferred_element_type=F32)


def _bdot_nt(a, b):
    return lax.dot_general(a.astype(BF16), b.astype(BF16), (((1,), (1,)), ((), ())),
                           preferred_element_type=F32)


def _bdot_tn(a, b):
    return lax.dot_general(a.astype(BF16), b.astype(BF16), (((0,), (0,)), ((), ())),
                           preferred_element_type=F32)


def _fdot(a, b):
    return jnp.dot(a, b, precision=HIGHEST, preferred_element_type=F32)


def _split_bf16(x, terms):
    out = []
    for _ in range(terms - 1):
        hi = x.astype(BF16)
        out.append(hi)
        x = x - hi.astype(F32)
    out.append(x.astype(BF16))
    return out


def _cum_dot(cum_rep, x, terms=3):
    return jnp.dot(cum_rep, jnp.concatenate(_split_bf16(x, terms), axis=0), preferred_element_type=F32)


def _head_sum(x, ones_rep, terms=2):
    outs = []
    for p in range(x.shape[1] // LANES):
        parts = _split_bf16(x[:, p * LANES:(p + 1) * LANES], terms)
        outs.append(jnp.dot(jnp.concatenate(parts, axis=1), ones_rep, preferred_element_type=F32))
    return outs[0] if len(outs) == 1 else jnp.concatenate(outs, axis=1)


def _sigmoid(x):
    return 0.5 * jnp.tanh(0.5 * x) + 0.5


def _silu(x):
    return x * _sigmoid(x)


def _softplus(x):
    return jnp.maximum(x, 0.0) + jnp.log1p(jnp.exp(-jnp.abs(x)))


def _log_sigmoid(x):
    return -_softplus(-x)


def _iota2(shape, dim):
    return lax.broadcasted_iota(jnp.int32, shape, dim)


def _bd_stack(x, lo):
    return jnp.concatenate([jnp.where(lo, x, 0.0), jnp.where(lo, 0.0, x)], axis=0)


def _pair_col(q, j, lo):
    return jnp.where(lo, q[0:CHUNK, j:j + 1], q[0:CHUNK, HD + j:HD + j + 1])


def _pair_row(qt, j, lo_row):
    return jnp.where(lo_row, qt[j:j + 1, :], qt[HD + j:HD + j + 1, :])


def _mod_kernel(c_ref, w_ref, b_ref, o_ref):
    o_ref[...] = _fdot(_silu(c_ref[...]), w_ref[...]) + b_ref[...]


def _modulation(cc, w_mod, b_mod):
    depth, d, n = w_mod.shape
    rows = cc.shape[0]
    return pl.pallas_call(
        _mod_kernel,
        grid=(depth, n // N_TILE),
        in_specs=[pl.BlockSpec((rows, d), lambda l, j: (0, 0)),
                  pl.BlockSpec((None, d, N_TILE), lambda l, j: (l, 0, j)),
                  pl.BlockSpec((None, 1, N_TILE), lambda l, j: (l, 0, j))],
        out_specs=pl.BlockSpec((None, rows, N_TILE), lambda l, j: (l, 0, j)),
        out_shape=jax.ShapeDtypeStruct((depth, rows, n), F32),
        compiler_params=pltpu.CompilerParams(dimension_semantics=("arbitrary", "arbitrary")),
        name="modulation",
    )(cc, w_mod, b_mod.reshape(depth, 1, n))


SEG_QKV, SEG_OZ, SEG_GATES, SEG_RS, SEG_RZ, SEG_LXZ = range(6)


def _inproj_kernel(x_ref, xprev_ref, xnext_ref, mod_ref, g_ref, w_ref, mu_ref, gb_ref, *refs,
                   grid_shift, seq_len):
    tm, d = x_ref.shape
    single_tile = tm >= seq_len
    span = seq_len if single_tile else tm
    i = pl.program_id(1)
    n_tiles = pl.num_programs(1)
    mod = mod_ref[...]
    g = g_ref[...]

    def norm_mod(x):
        h = x * lax.rsqrt(jnp.mean(x * x, axis=-1, keepdims=True) + EPS) * g
        return (h * (1.0 + mod[:, d:2 * d]) + mod[:, 0:d]).astype(BF16)

    out_refs, h_s = refs[:-1], refs[-1]
    off = 0 if single_tile else HALO_ROWS
    h_s[off:off + tm, :] = norm_mod(x_ref[...])
    if not single_tile:
        h_s[0:off, :] = norm_mod(xprev_ref[...])
        h_s[off + tm:, :] = norm_mod(xnext_ref[...])
    has_prev = i > 0
    has_next = i < n_tiles - 1
    row = _iota2((tm, 1), 0) % span
    a, b = SEG_OFFS[SEG_RS], SEG_OFFS[SEG_RS + 1]
    o_ref = out_refs[SEG_RS]
    for n0 in range(a, b, N_TILE):
        n1 = min(n0 + N_TILE, b)
        u_all = jnp.dot(h_s[...], w_ref[:, n0:n1], preferred_element_type=F32)
        if single_tile:
            u = u_all
            u_prev = u_next = jnp.zeros((HALO_ROWS, n1 - n0), F32)
        else:
            u = u_all[HALO_ROWS:HALO_ROWS + tm]
            u_prev = jnp.where(has_prev, u_all[0:HALO_ROWS], 0.0)
            u_next = jnp.where(has_next, u_all[HALO_ROWS + tm:], 0.0)
        left = pltpu.roll(u, 1, 0)
        right = pltpu.roll(u, tm - 1, 0)
        if grid_shift:
            left = jnp.where(row % GRID_W == 0, 0.0, left)
            right = jnp.where(row % GRID_W == GRID_W - 1, 0.0, right)
            up = jnp.concatenate([u_prev, u[0:tm - GRID_W]], axis=0)
            down = jnp.concatenate([u[GRID_W:tm], u_next], axis=0)
            if tm > span:
                up = jnp.where(row < GRID_W, 0.0, up)
                down = jnp.where(row >= span - GRID_W, 0.0, down)
            sh = 0.25 * (up + down + left + right)
        else:
            left = jnp.where(row == 0, u_prev[HALO_ROWS - 1:HALO_ROWS, :], left)
            right = jnp.where(row == span - 1, u_next[0:1, :], right)
            sh = 0.5 * (left + right)
        o_ref[:, n0 - a:n1 - a] = (u + mu_ref[:, n0 - a:n1 - a] * (sh - u)).astype(o_ref.dtype)

    is_f = (_iota2((1, LANES), 1) % 8) >= 4
    a, b = SEG_OFFS[SEG_GATES], SEG_OFFS[SEG_GATES + 1]
    pre = jnp.dot(h_s[off:off + tm, :], w_ref[:, a:b], preferred_element_type=F32) + gb_ref[...]
    out_refs[SEG_GATES][...] = jnp.where(is_f, _log_sigmoid(pre), pre)

    for seg in (SEG_QKV, SEG_OZ, SEG_RZ, SEG_LXZ):
        o_ref, a, b = out_refs[seg], SEG_OFFS[seg], SEG_OFFS[seg + 1]
        for n0 in range(a, b, N_TILE):
            n1 = min(n0 + N_TILE, b)
            o_ref[:, n0 - a:n1 - a] = jnp.dot(h_s[off:off + tm, :], w_ref[:, n0:n1],
                                              preferred_element_type=F32).astype(o_ref.dtype)


def _inproj(x, mod, g_pre, w_in_p, r_mu, gate_bias, per_seq_mod, grid_shift):
    bsz0, t0, d = x.shape
    group = 1
    if not per_seq_mod and t0 < IN_TILE and IN_TILE % t0 == 0 and bsz0 % (IN_TILE // t0) == 0:
        group = IN_TILE // t0
    x = x.reshape(bsz0 // group, group * t0, d)
    bsz, t, _ = x.shape
    tm = min(t, IN_TILE)
    hb = tm // HALO_ROWS
    n_halo = t // HALO_ROWS
    mod_idx = (lambda b, i: (b, 0, 0)) if per_seq_mod else (lambda b, i: (0, 0, 0))
    outs = pl.pallas_call(
        functools.partial(_inproj_kernel, grid_shift=grid_shift, seq_len=t0),
        grid=(bsz, t // tm),
        in_specs=[pl.BlockSpec((None, tm, d), lambda b, i: (b, i, 0)),
                  pl.BlockSpec((None, HALO_ROWS, d), lambda b, i: (b, jnp.maximum(i * hb - 1, 0), 0)),
                  pl.BlockSpec((None, HALO_ROWS, d),
                               lambda b, i: (b, jnp.minimum((i + 1) * hb, n_halo - 1), 0)),
                  pl.BlockSpec((None, 1, 3 * d), mod_idx),
                  pl.BlockSpec((1, d), lambda b, i: (0, 0)),
                  pl.BlockSpec((d, IN_COLS_PAD), lambda b, i: (0, 0), pipeline_mode=pl.Buffered(1)),
                  pl.BlockSpec((1, R_SHIFT), lambda b, i: (0, 0)),
                  pl.BlockSpec((1, LANES), lambda b, i: (0, 0))],
        out_specs=[pl.BlockSpec((None, tm, w), lambda b, i: (b, i, 0)) for w in SEG_WIDTHS],
        out_shape=[jax.ShapeDtypeStruct((bsz, t, w), dt) for w, dt in zip(SEG_WIDTHS, SEG_DTYPES)],
        scratch_shapes=[pltpu.VMEM((tm if t == tm else tm + 2 * HALO_ROWS, d), BF16)],
        compiler_params=pltpu.CompilerParams(dimension_semantics=("arbitrary", "arbitrary"),
                                             vmem_limit_bytes=VMEM_LIMIT),
        name="inproj",
    )(x, x, x, mod, g_pre, w_in_p, r_mu, gate_bias)
    return [o.reshape(bsz0, t0, o.shape[-1]) for o in outs]


def _outproj_kernel(ym_ref, yr_ref, yl_ref, x_ref, mod_ref, g_ref, w_ref, o_ref):
    d = x_ref.shape[-1]
    o = jnp.dot(ym_ref[...], w_ref[0:HW, :], preferred_element_type=F32)
    o = o + jnp.dot(yr_ref[...], w_ref[HW:2 * HW, :], preferred_element_type=F32)
    o = o + jnp.dot(yl_ref[...], w_ref[2 * HW:2 * HW + L_W, :], preferred_element_type=F32)
    on = o * lax.rsqrt(jnp.mean(o * o, axis=-1, keepdims=True) + EPS) * g_ref[...]
    o_ref[...] = x_ref[...] + mod_ref[:, 2 * d:3 * d] * on


def _outproj(ym, yr, yl, x, mod, g_post, w_out_b, per_seq_mod):
    bsz0, t0, d = x.shape
    group = 1
    if not per_seq_mod and t0 < OUT_TILE and OUT_TILE % t0 == 0 and bsz0 % (OUT_TILE // t0) == 0:
        group = OUT_TILE // t0
    ym, yr, yl, x = (a.reshape(bsz0 // group, group * t0, a.shape[-1]) for a in (ym, yr, yl, x))
    bsz, t, _ = x.shape
    tm = min(t, OUT_TILE)
    mod_idx = (lambda b, i: (b, 0, 0)) if per_seq_mod else (lambda b, i: (0, 0, 0))
    tok = lambda w: pl.BlockSpec((None, tm, w), lambda b, i: (b, i, 0))
    out = pl.pallas_call(
        _outproj_kernel,
        grid=(bsz, t // tm),
        in_specs=[tok(HW), tok(HW), tok(L_W), tok(d),
                  pl.BlockSpec((None, 1, 3 * d), mod_idx),
                  pl.BlockSpec((1, d), lambda b, i: (0, 0)),
                  pl.BlockSpec((2 * HW + L_W, d), lambda b, i: (0, 0))],
        out_specs=tok(d),
        out_shape=jax.ShapeDtypeStruct((bsz, t, d), F32),
        compiler_params=pltpu.CompilerParams(dimension_semantics=("arbitrary", "arbitrary"),
                                             vmem_limit_bytes=VMEM_LIMIT),
        name="outproj",
    )(ym, yr, yl, x, mod, g_post, w_out_b)
    return out.reshape(bsz0, t0, d)


def _mlstm_kernel(*refs, has_state_in, has_state_out):
    qkv_ref, oz_ref, g_ref, ng_ref = refs[0:4]
    refs = refs[4:]
    if has_state_in:
        c0_ref, n0_ref, m0_ref = refs[0:3]
        refs = refs[3:]
    y_ref = refs[0]
    refs = refs[1:]
    if has_state_out:
        c_out_ref, n_out_ref, m_out_ref = refs[0:3]
        refs = refs[3:]
    hacc, cn_s, m_s = refs
    ns, t = qkv_ref.shape[0], qkv_ref.shape[1]
    nc = t // CHUNK
    lane = _iota2((CHUNK, LANES), 1)
    row = _iota2((CHUNK, LANES), 0)
    lo = lane < HD
    s_idx = lane % HD
    causal = (s_idx <= row, s_idx >= row)
    r2 = _iota2((2 * CHUNK, 2 * CHUNK), 0)
    c2 = _iota2((2 * CHUNK, 2 * CHUNK), 1)
    same = (r2 // CHUNK) == (c2 // CHUNK)
    cum = ((same & (c2 <= r2)).astype(BF16), (same & (c2 >= r2)).astype(BF16))
    cum_rep = tuple(jnp.concatenate([m, m, m], axis=1) for m in cum)
    bd_ones = same.astype(F32)
    ones_rep = jnp.concatenate([same.astype(BF16), same.astype(BF16)], axis=0)
    bd_mask2 = jnp.concatenate([same, same], axis=1)
    colid = _iota2((1, LANES), 1)
    is_f = (colid % 8) >= 4

    hacc[...] = jnp.zeros_like(hacc)
    cn_s[...] = jnp.zeros_like(cn_s)
    m_s[...] = jnp.zeros_like(m_s)
    if has_state_in:
        m_s[...] = m0_ref[...]
        for s in range(ns):
            for d in range(2):
                for p in range(N_PAIRS):
                    for h in range(2):
                        cn_s[s, d, p, h * HD:(h + 1) * HD, h * HD:(h + 1) * HD] = c0_ref[s, d, 2 * p + h]
                    n_rows = jnp.where(same, jnp.broadcast_to(n0_ref[s, d, p], (2 * CHUNK, LANES)), 0.0)
                    cn_s[s, d, p, :, LANES:2 * LANES] = n_rows.T

    def gate_table(s, t0, d):
        gates = g_ref[s, pl.ds(t0, CHUNK), :]
        x = jnp.concatenate([gates, gates], axis=0)
        q = jnp.where(is_f, _cum_dot(cum_rep[d], x), x)
        return q, q.T

    chains = [(s, d, p) for s in range(ns) for d in range(2) for p in range(N_PAIRS)]

    def body(j, carry):
        t0s = (pl.multiple_of(j * CHUNK, CHUNK), pl.multiple_of((nc - 1 - j) * CHUNK, CHUNK))
        tabs = {(s, d): gate_table(s, t0s[d], d) for s in range(ns) for d in range(2)}
        st = []
        for s, d, p in chains:
            t0 = t0s[d]
            q_tab, q_tab_t = tabs[(s, d)]
            ji, jf = _gate_col(d, 0, p), _gate_col(d, 1, p)
            c = dict(t0=t0, sl=slice(p * LANES, (p + 1) * LANES))
            c['bcol'] = _pair_col(q_tab, jf, lo)
            c['icol'] = _pair_col(q_tab, ji, lo)
            brow = _pair_row(q_tab_t, jf, lo[0:1, :])
            irow = _pair_row(q_tab_t, ji, lo[0:1, :])
            c['q'] = qkv_ref[s, pl.ds(t0, CHUNK), p * LANES:(p + 1) * LANES].astype(F32)
            c['k'] = qkv_ref[s, pl.ds(t0, CHUNK), HW + p * LANES:HW + (p + 1) * LANES].astype(F32) * (HD ** -0.5)
            c['v'] = qkv_ref[s, pl.ds(t0, CHUNK), 2 * HW + p * LANES:2 * HW + (p + 1) * LANES].astype(F32)
            c['mprev'] = m_s[s, d, p]
            c['cn'] = cn_s[s, d, p]
            dmat = jnp.where(causal[d], c['bcol'] - brow + irow, -jnp.inf)
            inter = c['bcol'] + c['mprev']
            mx = jnp.where(lo,
                           jnp.max(jnp.where(lo, dmat, -jnp.inf), axis=1, keepdims=True),
                           jnp.max(jnp.where(lo, -jnp.inf, dmat), axis=1, keepdims=True))
            c['m_t'] = jnp.maximum(inter, mx)
            c['pexp'] = jnp.exp(dmat - c['m_t'])
            c['sc'] = jnp.exp(inter - c['m_t'])
            last = CHUNK - 1 if d == 0 else 0
            b_last = c['bcol'][last:last + 1, :]
            g = b_last - c['bcol'] + c['icol']
            c['m_new'] = jnp.maximum(b_last + c['mprev'], jnp.max(g, axis=0, keepdims=True))
            c['dec'] = jnp.exp(b_last + c['mprev'] - c['m_new'])
            c['wk'] = jnp.exp(g - c['m_new'])
            st.append(c)
        for c in st:
            c['s'] = _bdot_nt(c['q'], _bd_stack(c['k'], lo)) * c['pexp']
        for c in st:
            c['a1'] = _bdot(c['q'], c['cn'])
        for c in st:
            c['upd'] = _bdot_tn(c['wk'] * c['k'], jnp.concatenate([c['v'], jnp.ones_like(c['v'])], axis=1))
        for c in st:
            c['a2'] = _bdot(c['s'], jnp.concatenate([_bd_stack(c['v'], lo), bd_ones], axis=1))
        for (s, d, p), c in zip(chains, st):
            num = c['sc'] * c['a1'][:, 0:LANES] + c['a2'][:, 0:LANES]
            den = c['sc'] * c['a1'][:, LANES:] + c['a2'][:, LANES:]
            h = num / jnp.maximum(jnp.abs(den), jnp.exp(-c['m_t']))
            hacc[s, pl.ds(c['t0'], CHUNK), c['sl']] = hacc[s, pl.ds(c['t0'], CHUNK), c['sl']] + h
            cn_s[s, d, p] = (jnp.concatenate([c['dec'], c['dec']], axis=1) * c['cn']
                             + jnp.where(bd_mask2, c['upd'], 0.0))
            m_s[s, d, p] = c['m_new']
        return carry

    lax.fori_loop(0, nc, body, 0)
    if has_state_out:
        m_out_ref[...] = m_s[...]
        lo2 = _iota2((1, LANES), 1) < HD
        for s in range(ns):
            for d in range(2):
                for p in range(N_PAIRS):
                    for h in range(2):
                        c_out_ref[s, d, 2 * p + h] = cn_s[s, d, p, h * HD:(h + 1) * HD, h * HD:(h + 1) * HD]
                    n_t = cn_s[s, d, p, :, LANES:2 * LANES].T
                    n_out_ref[s, d, p] = jnp.where(lo2, n_t[0:1, :], n_t[HD:HD + 1, :])

    ng = ng_ref[...]
    rb = min(t, 256)

    def epilogue(i, carry):
        r0 = pl.multiple_of(i * rb, rb)
        for s in range(ns):
            h = hacc[s, pl.ds(r0, rb), :]
            hn = h * lax.rsqrt(_head_sum(h * h, ones_rep) * (1.0 / HD) + EPS) * ng
            o = oz_ref[s, pl.ds(r0, rb), 0:HW].astype(F32)
            z = oz_ref[s, pl.ds(r0, rb), HW:2 * HW].astype(F32)
            y_ref[s, pl.ds(r0, rb), :] = (hn * _sigmoid(o) * _silu(z)).astype(y_ref.dtype)
        return carry

    lax.fori_loop(0, t // rb, epilogue, 0)


def _seq_spec(ns, t, w, single_buffer):
    if single_buffer:
        return pl.BlockSpec((ns, t, w), lambda b: (b, 0, 0), pipeline_mode=pl.Buffered(1))
    return pl.BlockSpec((ns, t, w), lambda b: (b, 0, 0))


def _full_spec(shape):
    n = len(shape)
    return pl.BlockSpec(shape, lambda b: (0,) * n)


def _state_spec(ns, shape, layer=None):
    if layer is None:
        return pl.BlockSpec((ns,) + shape, lambda b: (b,) + (0,) * len(shape))
    return pl.BlockSpec((ns, None) + shape, lambda b: (b, layer) + (0,) * len(shape))


def _seqs_per_step(bsz, bytes_per_seq):
    for ns in (2, 1):
        if bsz % ns == 0 and ns * bytes_per_seq <= MIXER_VMEM_BUDGET:
            return ns
    return 1


def _pair_rows(x):
    return jnp.repeat(x.reshape(x.shape[:-1] + (N_PAIRS, 1, 2)), HD, axis=-1)


def _mlstm(qkv, oz, gates, norm_g, state_in, want_state, layer=None):
    bsz, t, _ = qkv.shape
    big = t > 512
    in_bufs = 1 if big else 2
    ns = _seqs_per_step(bsz, t * ((5 * HW * qkv.dtype.itemsize + LANES * 4) * in_bufs
                                  + HW * 2 * 2 + HW * 4))
    c_shape, row_shape = (2, N_HEADS, HD, HD), (2, N_PAIRS, 1, LANES)
    in_specs = [_seq_spec(ns, t, 3 * HW, big), _seq_spec(ns, t, 2 * HW, big),
                _seq_spec(ns, t, LANES, big), _full_spec((1, HW))]
    args = [qkv, oz, gates, norm_g]
    if state_in is not None:
        c0, n0, m0 = state_in
        in_specs += [_state_spec(ns, c_shape, layer), _state_spec(ns, row_shape, layer),
                     _state_spec(ns, row_shape, layer)]
        args += [c0, n0.reshape(n0.shape[:-3] + row_shape), _pair_rows(m0)]
    out_specs = [_seq_spec(ns, t, HW, False)]
    out_shape = [jax.ShapeDtypeStruct((bsz, t, HW), BF16)]
    if want_state:
        out_specs += [_state_spec(ns, c_shape), _state_spec(ns, row_shape), _state_spec(ns, row_shape)]
        out_shape += [jax.ShapeDtypeStruct((bsz,) + c_shape, F32),
                      jax.ShapeDtypeStruct((bsz,) + row_shape, F32),
                      jax.ShapeDtypeStruct((bsz,) + row_shape, F32)]
    outs = pl.pallas_call(
        functools.partial(_mlstm_kernel, has_state_in=state_in is not None, has_state_out=want_state),
        grid=(bsz // ns,),
        in_specs=in_specs,
        out_specs=out_specs,
        out_shape=out_shape,
        scratch_shapes=[pltpu.VMEM((ns, t, HW), F32),
                        pltpu.VMEM((ns, 2, N_PAIRS, 2 * CHUNK, 2 * LANES), F32),
                        pltpu.VMEM((ns,) + row_shape, F32)],
        compiler_params=pltpu.CompilerParams(dimension_semantics=("arbitrary",),
                                             vmem_limit_bytes=VMEM_LIMIT),
        name="mlstm",
    )(*args)
    if not want_state:
        return outs[0], None
    y, c1, n1, m1 = outs
    return y, (c1, n1.reshape(bsz, 2, N_HEADS, HD), m1[:, :, :, 0, ::HD].reshape(bsz, 2, N_HEADS))


def _rwkv_kernel(*refs, has_state_in, has_state_out):
    (rs_ref, rz_ref, w0_ref, a0_ref, w2_ref, a2_ref, kk_ref, ka_ref, rk_ref, ng_ref) = refs[0:10]
    refs = refs[10:]
    if has_state_in:
        s0_ref = refs[0]
        refs = refs[1:]
    y_ref = refs[0]
    refs = refs[1:]
    if has_state_out:
        s_out_ref = refs[0]
        refs = refs[1:]
    yacc, bonus, s_s = refs
    ns, t = rs_ref.shape[0], rs_ref.shape[1]
    nc = t // CHUNK
    lane = _iota2((CHUNK, LANES), 1)
    row = _iota2((CHUNK, LANES), 0)
    lo = lane < HD
    s_idx = lane % HD
    strict = (s_idx < row, s_idx > row)
    incl = (s_idx <= row, s_idx >= row)
    eye_pk = (s_idx == row).astype(F32)
    r64 = _iota2((CHUNK, CHUNK), 0)
    c64 = _iota2((CHUNK, CHUNK), 1)
    cum = ((c64 <= r64).astype(BF16), (c64 >= r64).astype(BF16))
    cum_rep = tuple(jnp.concatenate([m, m, m], axis=1) for m in cum)
    r2 = _iota2((2 * CHUNK, 2 * CHUNK), 0)
    c2 = _iota2((2 * CHUNK, 2 * CHUNK), 1)
    bd_mask = (r2 // CHUNK) == (c2 // CHUNK)
    ones_rep = jnp.concatenate([bd_mask.astype(BF16), bd_mask.astype(BF16)], axis=0)
    kk = kk_ref[...]
    ka = ka_ref[...]
    rk = rk_ref[...]

    yacc[...] = jnp.zeros_like(yacc)
    s_s[...] = jnp.zeros_like(s_s)
    if has_state_in:
        for s in range(ns):
            for d in range(2):
                for p in range(N_PAIRS):
                    for h in range(2):
                        s_s[s, d, p, h * HD:(h + 1) * HD, h * HD:(h + 1) * HD] = s0_ref[s, d, 2 * p + h]

    def prep(s, c, d):
        t0 = pl.multiple_of(c * CHUNK, CHUNK)
        blk = rs_ref[s, pl.ds(t0, CHUNK), :].astype(F32)
        r = blk[:, 0:HW]
        k = blk[:, HW:2 * HW]
        v = blk[:, 2 * HW:3 * HW]
        wl = blk[:, 3 * HW:3 * HW + 2 * LORA]
        al = blk[:, 3 * HW + 2 * LORA:3 * HW + 4 * LORA]
        dsl = slice(d * HW, (d + 1) * HW)
        logw = -RWKV_DECAY_SCALE * _sigmoid(w0_ref[:, dsl] + _bdot(jnp.tanh(wl), w2_ref[:, dsl]))
        a = _sigmoid(a0_ref[:, dsl] + _bdot(al, a2_ref[:, dsl]))
        kappa = k * kk
        khat = kappa * lax.rsqrt(jnp.maximum(_head_sum(kappa * kappa, ones_rep), 1e-24))
        kt = k * (1.0 + (a - 1.0) * ka)
        b = khat * a
        if d == 0:
            bonus[s, pl.ds(t0, CHUNK), :] = _head_sum(r * k * rk, ones_rep) * v
        last = CHUNK - 1 if d == 0 else 0
        logp = _cum_dot(cum_rep[d], logw)
        logp_last = logp[last:last + 1, :]
        e_neg = jnp.exp(-logp)
        e_last = jnp.exp(logp_last - logp)
        return dict(t0=t0, v=v, alpha=jnp.exp(logp - logw) * khat, beta=b * e_neg, kap=kt * e_neg,
                    rho=r * jnp.exp(logp), beta_l=b * e_last, kap_l=kt * e_last,
                    p_last=jnp.exp(logp_last))

    chains = [(s, d, p) for s in range(ns) for d in range(2) for p in range(N_PAIRS)]

    def body(j, carry):
        pre = {(s, d): prep(s, j if d == 0 else nc - 1 - j, d) for s in range(ns) for d in range(2)}
        st = []
        for s, d, p in chains:
            sl = slice(p * LANES, (p + 1) * LANES)
            c = {key: (val if key == 't0' else val[:, sl]) for key, val in pre[(s, d)].items()}
            c['sl'] = sl
            st.append(c)
        for c in st:
            c['g'] = _bdot_nt(jnp.concatenate([c['alpha'], c['rho']], axis=0),
                              jnp.concatenate([_bd_stack(c['beta'], lo), _bd_stack(c['kap'], lo)], axis=0))
        for (s, d, p), c in zip(chains, st):
            g = c.pop('g')
            c['n'] = jnp.where(strict[d], g[0:CHUNK, 0:LANES], 0.0)
            c['mk'] = jnp.where(strict[d], g[0:CHUNK, LANES:], 0.0)
            c['mrb'] = jnp.where(incl[d], g[CHUNK:, 0:LANES], 0.0)
            c['mrk'] = jnp.where(incl[d], g[CHUNK:, LANES:], 0.0)
            c['inv'] = eye_pk - c['n']
        for c in st:
            c['pw'] = _bdot(c['n'], _bd_stack(c['n'], lo))
        for c in st:
            c['mkv'] = _bdot(c['mk'], _bd_stack(c['v'], lo))
        for _ in range(4):
            for c in st:
                rp = _bdot(jnp.concatenate([c['inv'], c['pw']], axis=0), _bd_stack(c['pw'], lo))
                c['inv'] = c['inv'] + rp[0:CHUNK]
                c['pw'] = rp[CHUNK:]
        for c in st:
            c['inv'] = c['inv'] + _bdot(c['inv'], _bd_stack(c['pw'], lo))
        for c in st:
            c['wu'] = _bdot(c['inv'], jnp.concatenate([_bd_stack(c['alpha'], lo), _bd_stack(c['mkv'], lo)], axis=1))
        for (s, d, p), c in zip(chains, st):
            c['s_prev'] = s_s[s, d, p]
            ws = _bdot_nt(jnp.concatenate([c['wu'][:, 0:LANES], c['rho']], axis=0), c['s_prev'])
            c['u'] = ws[0:CHUNK] + c['wu'][:, LANES:]
            c['ys'] = ws[CHUNK:]
        for c in st:
            c['upd'] = _bdot_tn(jnp.concatenate([c['v'], -c['u']], axis=0),
                                jnp.concatenate([c['kap_l'], c['beta_l']], axis=0))
        for c in st:
            c['y'] = c['ys'] + _bdot(jnp.concatenate([c['mrk'], -c['mrb']], axis=1),
                                     jnp.concatenate([_bd_stack(c['v'], lo), _bd_stack(c['u'], lo)], axis=0))
        for (s, d, p), c in zip(chains, st):
            s_s[s, d, p] = c['s_prev'] * c['p_last'] + jnp.where(bd_mask, c['upd'], 0.0)
            yacc[s, pl.ds(c['t0'], CHUNK), c['sl']] = yacc[s, pl.ds(c['t0'], CHUNK), c['sl']] + c['y']
        return carry

    lax.fori_loop(0, nc, body, 0)
    if has_state_out:
        for s in range(ns):
            for d in range(2):
                for p in range(N_PAIRS):
                    for h in range(2):
                        s_out_ref[s, d, 2 * p + h] = s_s[s, d, p, h * HD:(h + 1) * HD, h * HD:(h + 1) * HD]

    ng = ng_ref[...]
    rb = min(t, 256)

    def epilogue(i, carry):
        r0 = pl.multiple_of(i * rb, rb)
        for s in range(ns):
            y = yacc[s, pl.ds(r0, rb), :]
            yn = (y * lax.rsqrt(_head_sum(y * y, ones_rep) * (1.0 / HD) + EPS) * ng
                  + bonus[s, pl.ds(r0, rb), :])
            y_ref[s, pl.ds(r0, rb), :] = (yn * _silu(rz_ref[s, pl.ds(r0, rb), :].astype(F32))).astype(y_ref.dtype)
        return carry

    lax.fori_loop(0, t // rb, epilogue, 0)


def _rwkv(rs, rz, w0, a0, w2bd, a2bd, kk, ka, rk, norm_g, state_in, want_state, layer=None):
    bsz, t, _ = rs.shape
    big = t > 512
    in_bufs = 1 if big else 2
    itemsize = rs.dtype.itemsize
    ns = _seqs_per_step(bsz, t * ((R_SHIFT + HW) * itemsize * in_bufs + HW * 2 * 2 + 2 * HW * 4))
    s_shape = (2, N_HEADS, HD, HD)
    in_specs = [_seq_spec(ns, t, R_SHIFT, big), _seq_spec(ns, t, HW, big),
                _full_spec((1, 2 * HW)), _full_spec((1, 2 * HW)),
                _full_spec((2 * LORA, 2 * HW)), _full_spec((2 * LORA, 2 * HW)),
                _full_spec((1, HW)), _full_spec((1, HW)), _full_spec((1, HW)), _full_spec((1, HW))]
    args = [rs, rz, w0, a0, w2bd, a2bd, kk, ka, rk, norm_g]
    if state_in is not None:
        in_specs.append(_state_spec(ns, s_shape, layer))
        args.append(state_in)
    out_specs = [_seq_spec(ns, t, HW, False)]
    out_shape = [jax.ShapeDtypeStruct((bsz, t, HW), BF16)]
    if want_state:
        out_specs.append(_state_spec(ns, s_shape))
        out_shape.append(jax.ShapeDtypeStruct((bsz,) + s_shape, F32))
    outs = pl.pallas_call(
        functools.partial(_rwkv_kernel, has_state_in=state_in is not None, has_state_out=want_state),
        grid=(bsz // ns,),
        in_specs=in_specs,
        out_specs=out_specs,
        out_shape=out_shape,
        scratch_shapes=[pltpu.VMEM((ns, t, HW), F32), pltpu.VMEM((ns, t, HW), F32),
                        pltpu.VMEM((ns, 2, N_PAIRS, 2 * CHUNK, LANES), F32)],
        compiler_params=pltpu.CompilerParams(dimension_semantics=("arbitrary",),
                                             vmem_limit_bytes=VMEM_LIMIT),
        name="rwkv",
    )(*args)
    return (outs[0], outs[1]) if want_state else (outs[0], None)


def _lru_kernel(xz_ref, conv_ref, cb_ref, w_ref, b_ref, lam_ref, h0_ref, y_ref, hfin_ref, a_s, b_s, acc):
    t = xz_ref.shape[0]
    nb = t // LRU_BLOCK
    n_groups = LRU_BLOCK // SUBLANES
    row = _iota2((LRU_BLOCK, L_W), 0)
    sub = _iota2((n_groups, SUBLANES, L_W), 1)
    conv = conv_ref[...]
    cb = cb_ref[...]
    sp = _softplus(-lam_ref[...])

    def coefficients(j, carry):
        t0 = pl.multiple_of(j * LRU_BLOCK, LRU_BLOCK)
        x = xz_ref[pl.ds(t0, LRU_BLOCK), 0:L_W].astype(F32)
        prev_rows = xz_ref[pl.ds(pl.multiple_of(jnp.maximum(t0 - HALO, 0), HALO), HALO), 0:L_W]
        next_rows = xz_ref[pl.ds(pl.multiple_of(jnp.minimum(t0 + LRU_BLOCK, t - HALO), HALO), HALO), 0:L_W]
        prev_rows = jnp.where(j > 0, prev_rows.astype(F32), 0.0)
        next_rows = jnp.where(j < nb - 1, next_rows.astype(F32), 0.0)
        p1, p2 = prev_rows[HALO - 1:HALO, :], prev_rows[HALO - 2:HALO - 1, :]
        xm1 = jnp.where(row == 0, p1, pltpu.roll(x, 1, 0))
        xm2 = jnp.where(row == 0, p2, jnp.where(row == 1, p1, pltpu.roll(x, 2, 0)))
        xp1 = jnp.where(row == LRU_BLOCK - 1, next_rows[0:1, :], pltpu.roll(x, LRU_BLOCK - 1, 0))
        xc = conv[0:1, :] * xm2 + conv[1:2, :] * xm1 + conv[2:3, :] * x + conv[3:4, :] * xp1 + cb
        pre = _bdot(xc, w_ref[...]) + b_ref[...]
        for d in range(2):
            rg = _sigmoid(pre[:, 2 * d * L_W:(2 * d + 1) * L_W])
            ig = _sigmoid(pre[:, (2 * d + 1) * L_W:(2 * d + 2) * L_W])
            log_a = -LRU_C * rg * sp[:, d * L_W:(d + 1) * L_W]
            a_step = jnp.exp(log_a)
            b_step = jnp.sqrt(jnp.tanh(-log_a) * (1.0 + a_step * a_step)) * ig * xc
            a_cum = a_step.reshape(n_groups, SUBLANES, L_W)
            b_cum = b_step.reshape(n_groups, SUBLANES, L_W)
            k = 1
            while k < SUBLANES:
                keep = (sub >= k) if d == 0 else (sub < SUBLANES - k)
                shift = k if d == 0 else SUBLANES - k
                a_sh = jnp.where(keep, pltpu.roll(a_cum, shift, 1), 1.0)
                b_sh = jnp.where(keep, pltpu.roll(b_cum, shift, 1), 0.0)
                b_cum = a_cum * b_sh + b_cum
                a_cum = a_cum * a_sh
                k *= 2
            a_s[d, pl.ds(t0, LRU_BLOCK), :] = a_cum.reshape(LRU_BLOCK, L_W)
            b_s[d, pl.ds(t0, LRU_BLOCK), :] = b_cum.reshape(LRU_BLOCK, L_W)
        return carry

    lax.fori_loop(0, nb, coefficients, 0)

    def scan_block(d, j, carry):
        t0 = pl.multiple_of(j * LRU_BLOCK, LRU_BLOCK)
        a_cum = a_s[d, pl.ds(t0, LRU_BLOCK), :]
        b_cum = b_s[d, pl.ds(t0, LRU_BLOCK), :]
        pieces = [None] * n_groups
        for gi in (range(n_groups) if d == 0 else reversed(range(n_groups))):
            rows = slice(gi * SUBLANES, (gi + 1) * SUBLANES)
            hg = b_cum[rows, :] + a_cum[rows, :] * carry
            carry = hg[SUBLANES - 1:SUBLANES, :] if d == 0 else hg[0:1, :]
            pieces[gi] = hg
        return t0, jnp.concatenate(pieces, axis=0), carry

    def forward(j, carry):
        t0, h, carry = scan_block(0, j, carry)
        acc[pl.ds(t0, LRU_BLOCK), :] = h
        return carry

    def backward(jj, carry):
        t0, h, carry = scan_block(1, nb - 1 - jj, carry)
        z = xz_ref[pl.ds(t0, LRU_BLOCK), L_W:2 * L_W].astype(F32)
        y_ref[pl.ds(t0, LRU_BLOCK), :] = ((acc[pl.ds(t0, LRU_BLOCK), :] + h) * _silu(z)).astype(y_ref.dtype)
        return carry

    hfin_ref[0] = lax.fori_loop(0, nb, forward, h0_ref[0])
    hfin_ref[1] = lax.fori_loop(0, nb, backward, h0_ref[1])


def _lru(xz, conv, conv_b, wbd, bias, lam, h0):
    bsz, t, _ = xz.shape
    st = lambda shape: pl.BlockSpec((None,) + shape, lambda b: (b,) + (0,) * len(shape))
    return pl.pallas_call(
        _lru_kernel,
        grid=(bsz,),
        in_specs=[_seq_spec(None, t, 2 * L_W, False), _full_spec((CONV_W, L_W)), _full_spec((1, L_W)),
                  _full_spec((L_W, 4 * L_W)), _full_spec((1, 4 * L_W)), _full_spec((1, 2 * L_W)),
                  st((2, 1, L_W))],
        out_specs=[pl.BlockSpec((None, t, L_W), lambda b: (b, 0, 0)), st((2, 1, L_W))],
        out_shape=[jax.ShapeDtypeStruct((bsz, t, L_W), BF16),
                   jax.ShapeDtypeStruct((bsz, 2, 1, L_W), F32)],
        scratch_shapes=[pltpu.VMEM((2, t, L_W), F32), pltpu.VMEM((2, t, L_W), F32),
                        pltpu.VMEM((t, L_W), F32)],
        compiler_params=pltpu.CompilerParams(dimension_semantics=("arbitrary",),
                                             vmem_limit_bytes=VMEM_LIMIT),
        name="lru",
    )(xz, conv, conv_b, wbd, bias, lam, h0)


def _permute_w_in(w_in_l):
    valid = jnp.asarray(_GATE_SRC >= 0)
    gates = jnp.where(valid, w_in_l[:, 5 * HW:M_COLS][:, np.maximum(_GATE_SRC, 0)], 0.0)
    return jnp.concatenate([w_in_l[:, 0:5 * HW], gates, w_in_l[:, M_COLS:]], axis=1).astype(BF16)


def _gate_source_index():
    src = np.full((LANES,), -1, np.int32)
    for dd in range(2):
        for is_f in range(2):
            for p in range(N_PAIRS):
                for parity in range(2):
                    src[parity * HD + _gate_col(dd, is_f, p)] = (is_f * 2 * N_HEADS + dd * N_HEADS
                                                                 + 2 * p + parity)
    return src


_GATE_SRC = _gate_source_index()


def _gate_bias(m_bi_l, m_bf_l):
    flat = jnp.concatenate([m_bi_l.reshape(-1), m_bf_l.reshape(-1)])
    return jnp.where(jnp.asarray(_GATE_SRC >= 0), flat[np.maximum(_GATE_SRC, 0)], 0.0).reshape(1, LANES)


def _block_diag(blocks):
    n, a, b = blocks.shape
    eye = jnp.eye(n, dtype=blocks.dtype)
    return (eye[:, None, :, None] * blocks[:, :, None, :]).reshape(n * a, n * b)


def _layer_params(l, g_pre, g_post, w_in, w_out, m_bi, m_bf, m_norm, r_mu, r_w0, r_w2, r_a0, r_a2,
                  r_kk, r_ka, r_rk, r_norm, l_conv, l_conv_b, l_wa, l_ba, l_wx, l_bx, l_lambda):
    row = lambda v: v.reshape(1, -1)
    return dict(
        g_pre=row(g_pre[l]), g_post=row(g_post[l]),
        w_in=_permute_w_in(w_in[l]), w_out=w_out[l].astype(BF16),
        gate_bias=_gate_bias(m_bi[l], m_bf[l]), m_norm=row(m_norm[l]),
        r_mu=row(r_mu[l]), r_w0=row(r_w0[l]), r_a0=row(r_a0[l]),
        r_w2=_block_diag(r_w2[l]).astype(BF16), r_a2=_block_diag(r_a2[l]).astype(BF16),
        r_kk=row(r_kk[l]), r_ka=row(r_ka[l]), r_rk=row(r_rk[l]), r_norm=row(r_norm[l]),
        l_conv=l_conv[l], l_conv_b=row(l_conv_b[l]),
        l_w=jnp.concatenate([_block_diag(l_wa[l][0]), _block_diag(l_wx[l][0]),
                             _block_diag(l_wa[l][1]), _block_diag(l_wx[l][1])], axis=1).astype(BF16),
        l_b=jnp.concatenate([l_ba[l][0], l_bx[l][0], l_ba[l][1], l_bx[l][1]]).reshape(1, -1),
        l_lambda=row(l_lambda[l]),
    )


def _trunk_layer(x, mod, lp, states, want_state, per_seq_mod, grid_shift, layer=None):
    bsz = x.shape[0]
    qkv, oz, gates, rs, rz, lxz = _inproj(x, mod, lp['g_pre'], lp['w_in'], lp['r_mu'], lp['gate_bias'],
                                          per_seq_mod, grid_shift)
    m_in = None if states is None else states[0:3]
    r_in = None if states is None else states[3]
    if states is None:
        l_in = jnp.zeros((bsz, 2, 1, L_W), F32)
    else:
        l_in = (states[4] if layer is None else states[4][:, layer])[:, :, None, :]
    y_m, m_out = _mlstm(qkv, oz, gates, lp['m_norm'], m_in, want_state, layer)
    y_r, r_out = _rwkv(rs, rz, lp['r_w0'], lp['r_a0'], lp['r_w2'], lp['r_a2'], lp['r_kk'],
                       lp['r_ka'], lp['r_rk'], lp['r_norm'], r_in, want_state, layer)
    y_l, l_out = _lru(lxz, lp['l_conv'], lp['l_conv_b'], lp['l_w'], lp['l_b'], lp['l_lambda'], l_in)
    y = _outproj(y_m, y_r, y_l, x, mod, lp['g_post'], lp['w_out'], per_seq_mod)
    new_states = m_out + (r_out, l_out[:, :, 0, :]) if want_state else None
    return y, new_states


def kernel(x_prompt, x_sample, c, state_mlstm_C, state_mlstm_n, state_mlstm_m, state_rwkv, state_rglru, c_ctx, g_pre, g_post, w_mod, b_mod, w_in, w_out, m_bi, m_bf, m_norm, r_mu, r_w0, r_w2, r_a0, r_a2, r_kk, r_ka, r_rk, r_norm, l_conv, l_conv_b, l_wa, l_ba, l_wx, l_bx, l_lambda):
    depth = w_in.shape[0]
    bs = x_sample.shape[0]
    d = x_prompt.shape[-1]
    rows = -(-(1 + bs) // SUBLANES) * SUBLANES
    cc = jnp.zeros((rows, d), F32).at[0].set(c_ctx).at[1:1 + bs].set(c)
    mod = _modulation(cc, w_mod, b_mod)
    xp, xs = x_prompt, x_sample
    cache = tuple(a.astype(F32) for a in
                  (state_mlstm_C, state_mlstm_n, state_mlstm_m, state_rwkv, state_rglru))
    new_states = []
    for l in range(depth):
        lp = _layer_params(l, g_pre, g_post, w_in, w_out, m_bi, m_bf, m_norm, r_mu, r_w0, r_w2, r_a0,
                           r_a2, r_kk, r_ka, r_rk, r_norm, l_conv, l_conv_b, l_wa, l_ba, l_wx, l_bx,
                           l_lambda)
        xp, st = _trunk_layer(xp, mod[l, 0:1][:, None, :], lp, None, True, False, False)
        new_states.append(st)
        xs, _ = _trunk_layer(xs, mod[l, 1:1 + bs][:, None, :], lp, cache, False, True, True, layer=l)
    stacked = tuple(jnp.stack([st[i] for st in new_states], axis=1) for i in range(5))
    return (xp, xs) + stacked
```

```python
import functools

import numpy as np
import jax
import jax.numpy as jnp
from jax import lax
from jax.experimental import pallas as pl
from jax.experimental.pallas import tpu as pltpu

F32 = jnp.float32
BF16 = jnp.bfloat16
HIGHEST = lax.Precision.HIGHEST

D_MODEL = 1024
EPS = 1e-6
HD = 64
N_HEADS = 6
N_PAIRS = N_HEADS // 2
HW = N_HEADS * HD
CHUNK = 64
GRID_W = 64
LORA = 64
R_SHIFT = 3 * HW + 4 * LORA
L_W = 256
L_BLOCKS = 4
CONV_W = 4
LRU_C = 8.0
RWKV_DECAY_SCALE = 0.6065306597126334
M_COLS = 5 * HW + 4 * N_HEADS
IN_COLS = M_COLS + R_SHIFT + HW + 2 * L_W

LANES = 128
SUBLANES = 8
HALO = 16
HALO_ROWS = GRID_W
LRU_BLOCK = 128
VMEM_LIMIT = 60 * 1024 * 1024
MIXER_VMEM_BUDGET = 50 * 1024 * 1024

SEG_WIDTHS = (3 * HW, 2 * HW, LANES, R_SHIFT, HW, 2 * L_W)
SEG_DTYPES = (BF16, BF16, F32, BF16, BF16, BF16)
SEG_OFFS = tuple(int(v) for v in np.cumsum((0,) + SEG_WIDTHS))
IN_COLS_PAD = SEG_OFFS[-1]
N_TILE = 512
IN_TILE = 512
OUT_TILE = 1024


def _gate_col(d, is_f, p):
    return 8 * d + 4 * is_f + p


def _bdot(a, b):
    return jnp.dot(a.astype(BF16), b.astype(BF16), preferred_element_type=F32)


def _bdot_nt(a, b):
    return lax.dot_general(a.astype(BF16), b.astype(BF16), (((1,), (1,)), ((), ())),
                           preferred_element_type=F32)


def _bdot_tn(a, b):
    return lax.dot_general(a.astype(BF16), b.astype(BF16), (((0,), (0,)), ((), ())),
                           preferred_element_type=F32)


def _fdot(a, b):
    return jnp.dot(a, b, precision=HIGHEST, preferred_element_type=F32)


def _split_bf16(x, terms):
    out = []
    for _ in range(terms - 1):
        hi = x.astype(BF16)
        out.append(hi)
        x = x - hi.astype(F32)
    out.append(x.astype(BF16))
    return out


def _cum_dot(cum_rep, x, terms=3):
    return jnp.dot(cum_rep, jnp.concatenate(_split_bf16(x, terms), axis=0), preferred_element_type=F32)


def _head_sum(x, ones_rep, terms=2):
    outs = []
    for p in range(x.shape[1] // LANES):
        parts = _split_bf16(x[:, p * LANES:(p + 1) * LANES], terms)
        outs.append(jnp.dot(jnp.concatenate(parts, axis=1), ones_rep, preferred_element_type=F32))
    return outs[0] if len(outs) == 1 else jnp.concatenate(outs, axis=1)


def _sigmoid(x):
    return 0.5 * jnp.tanh(0.5 * x) + 0.5


def _silu(x):
    return x * _sigmoid(x)


def _softplus(x):
    return jnp.maximum(x, 0.0) + jnp.log1p(jnp.exp(-jnp.abs(x)))


def _log_sigmoid(x):
    return -_softplus(-x)


def _iota2(shape, dim):
    return lax.broadcasted_iota(jnp.int32, shape, dim)


def _bd_stack(x, lo):
    return jnp.concatenate([jnp.where(lo, x, 0.0), jnp.where(lo, 0.0, x)], axis=0)


def _pair_col(q, j, lo):
    return jnp.where(lo, q[0:CHUNK, j:j + 1], q[0:CHUNK, HD + j:HD + j + 1])


def _pair_row(qt, j, lo_row):
    return jnp.where(lo_row, qt[j:j + 1, :], qt[HD + j:HD + j + 1, :])


def _mod_kernel(c_ref, w_ref, b_ref, o_ref):
    o_ref[...] = _fdot(_silu(c_ref[...]), w_ref[...]) + b_ref[...]


def _modulation(cc, w_mod, b_mod):
    depth, d, n = w_mod.shape
    rows = cc.shape[0]
    return pl.pallas_call(
        _mod_kernel,
        grid=(depth, n // N_TILE),
        in_specs=[pl.BlockSpec((rows, d), lambda l, j: (0, 0)),
                  pl.BlockSpec((None, d, N_TILE), lambda l, j: (l, 0, j)),
                  pl.BlockSpec((None, 1, N_TILE), lambda l, j: (l, 0, j))],
        out_specs=pl.BlockSpec((None, rows, N_TILE), lambda l, j: (l, 0, j)),
        out_shape=jax.ShapeDtypeStruct((depth, rows, n), F32),
        compiler_params=pltpu.CompilerParams(dimension_semantics=("arbitrary", "arbitrary")),
        name="modulation",
    )(cc, w_mod, b_mod.reshape(depth, 1, n))


SEG_QKV, SEG_OZ, SEG_GATES, SEG_RS, SEG_RZ, SEG_LXZ = range(6)


def _inproj_kernel(x_ref, xprev_ref, xnext_ref, mod_ref, g_ref, w_ref, mu_ref, gb_ref, *refs,
                   grid_shift, seq_len):
    tm, d = x_ref.shape
    single_tile = tm >= seq_len
    span = seq_len if single_tile else tm
    i = pl.program_id(1)
    n_tiles = pl.num_programs(1)
    mod = mod_ref[...]
    g = g_ref[...]

    def norm_mod(x):
        h = x * lax.rsqrt(jnp.mean(x * x, axis=-1, keepdims=True) + EPS) * g
        return (h * (1.0 + mod[:, d:2 * d]) + mod[:, 0:d]).astype(BF16)

    out_refs, h_s = refs[:-1], refs[-1]
    off = 0 if single_tile else HALO_ROWS
    h_s[off:off + tm, :] = norm_mod(x_ref[...])
    if not single_tile:
        h_s[0:off, :] = norm_mod(xprev_ref[...])
        h_s[off + tm:, :] = norm_mod(xnext_ref[...])
    has_prev = i > 0
    has_next = i < n_tiles - 1
    row = _iota2((tm, 1), 0) % span
    a, b = SEG_OFFS[SEG_RS], SEG_OFFS[SEG_RS + 1]
    o_ref = out_refs[SEG_RS]
    for n0 in range(a, b, N_TILE):
        n1 = min(n0 + N_TILE, b)
        u_all = jnp.dot(h_s[...], w_ref[:, n0:n1], preferred_element_type=F32)
        if single_tile:
            u = u_all
            u_prev = u_next = jnp.zeros((HALO_ROWS, n1 - n0), F32)
        else:
            u = u_all[HALO_ROWS:HALO_ROWS + tm]
            u_prev = jnp.where(has_prev, u_all[0:HALO_ROWS], 0.0)
            u_next = jnp.where(has_next, u_all[HALO_ROWS + tm:], 0.0)
        left = pltpu.roll(u, 1, 0)
        right = pltpu.roll(u, tm - 1, 0)
        if grid_shift:
            left = jnp.where(row % GRID_W == 0, 0.0, left)
            right = jnp.where(row % GRID_W == GRID_W - 1, 0.0, right)
            up = jnp.concatenate([u_prev, u[0:tm - GRID_W]], axis=0)
            down = jnp.concatenate([u[GRID_W:tm], u_next], axis=0)
            if tm > span:
                up = jnp.where(row < GRID_W, 0.0, up)
                down = jnp.where(row >= span - GRID_W, 0.0, down)
            sh = 0.25 * (up + down + left + right)
        else:
            left = jnp.where(row == 0, u_prev[HALO_ROWS - 1:HALO_ROWS, :], left)
            right = jnp.where(row == span - 1, u_next[0:1, :], right)
            sh = 0.5 * (left + right)
        o_ref[:, n0 - a:n1 - a] = (u + mu_ref[:, n0 - a:n1 - a] * (sh - u)).astype(o_ref.dtype)

    is_f = (_iota2((1, LANES), 1) % 8) >= 4
    a, b = SEG_OFFS[SEG_GATES], SEG_OFFS[SEG_GATES + 1]
    pre = jnp.dot(h_s[off:off + tm, :], w_ref[:, a:b], preferred_element_type=F32) + gb_ref[...]
    out_refs[SEG_GATES][...] = jnp.where(is_f, _log_sigmoid(pre), pre)

    for seg in (SEG_QKV, SEG_OZ, SEG_RZ, SEG_LXZ):
        o_ref, a, b = out_refs[seg], SEG_OFFS[seg], SEG_OFFS[seg + 1]
        for n0 in range(a, b, N_TILE):
            n1 = min(n0 + N_TILE, b)
            o_ref[:, n0 - a:n1 - a] = jnp.dot(h_s[off:off + tm, :], w_ref[:, n0:n1],
                                              preferred_element_type=F32).astype(o_ref.dtype)


def _inproj(x, mod, g_pre, w_in_p, r_mu, gate_bias, per_seq_mod, grid_shift):
    bsz0, t0, d = x.shape
    group = 1
    if not per_seq_mod and t0 < IN_TILE and IN_TILE % t0 == 0 and bsz0 % (IN_TILE // t0) == 0:
        group = IN_TILE // t0
    x = x.reshape(bsz0 // group, group * t0, d)
    bsz, t, _ = x.shape
    tm = min(t, IN_TILE)
    hb = tm // HALO_ROWS
    n_halo = t // HALO_ROWS
    mod_idx = (lambda b, i: (b, 0, 0)) if per_seq_mod else (lambda b, i: (0, 0, 0))
    outs = pl.pallas_call(
        functools.partial(_inproj_kernel, grid_shift=grid_shift, seq_len=t0),
        grid=(bsz, t // tm),
        in_specs=[pl.BlockSpec((None, tm, d), lambda b, i: (b, i, 0)),
                  pl.BlockSpec((None, HALO_ROWS, d), lambda b, i: (b, jnp.maximum(i * hb - 1, 0), 0)),
                  pl.BlockSpec((None, HALO_ROWS, d),
                               lambda b, i: (b, jnp.minimum((i + 1) * hb, n_halo - 1), 0)),
                  pl.BlockSpec((None, 1, 3 * d), mod_idx),
                  pl.BlockSpec((1, d), lambda b, i: (0, 0)),
                  pl.BlockSpec((d, IN_COLS_PAD), lambda b, i: (0, 0), pipeline_mode=pl.Buffered(1)),
                  pl.BlockSpec((1, R_SHIFT), lambda b, i: (0, 0)),
                  pl.BlockSpec((1, LANES), lambda b, i: (0, 0))],
        out_specs=[pl.BlockSpec((None, tm, w), lambda b, i: (b, i, 0)) for w in SEG_WIDTHS],
        out_shape=[jax.ShapeDtypeStruct((bsz, t, w), dt) for w, dt in zip(SEG_WIDTHS, SEG_DTYPES)],
        scratch_shapes=[pltpu.VMEM((tm if t == tm else tm + 2 * HALO_ROWS, d), BF16)],
        compiler_params=pltpu.CompilerParams(dimension_semantics=("arbitrary", "arbitrary"),
                                             vmem_limit_bytes=VMEM_LIMIT),
        name="inproj",
    )(x, x, x, mod, g_pre, w_in_p, r_mu, gate_bias)
    return [o.reshape(bsz0, t0, o.shape[-1]) for o in outs]


def _outproj_kernel(ym_ref, yr_ref, yl_ref, x_ref, mod_ref, g_ref, w_ref, o_ref):
    d = x_ref.shape[-1]
    o = jnp.dot(ym_ref[...], w_ref[0:HW, :], preferred_element_type=F32)
    o = o + jnp.dot(yr_ref[...], w_ref[HW:2 * HW, :], preferred_element_type=F32)
    o = o + jnp.dot(yl_ref[...], w_ref[2 * HW:2 * HW + L_W, :], preferred_element_type=F32)
    on = o * lax.rsqrt(jnp.mean(o * o, axis=-1, keepdims=True) + EPS) * g_ref[...]
    o_ref[...] = x_ref[...] + mod_ref[:, 2 * d:3 * d] * on


def _outproj(ym, yr, yl, x, mod, g_post, w_out_b, per_seq_mod):
    bsz0, t0, d = x.shape
    group = 1
    if not per_seq_mod and t0 < OUT_TILE and OUT_TILE % t0 == 0 and bsz0 % (OUT_TILE // t0) == 0:
        group = OUT_TILE // t0
    ym, yr, yl, x = (a.reshape(bsz0 // group, group * t0, a.shape[-1]) for a in (ym, yr, yl, x))
    bsz, t, _ = x.shape
    tm = min(t, OUT_TILE)
    mod_idx = (lambda b, i: (b, 0, 0)) if per_seq_mod else (lambda b, i: (0, 0, 0))
    tok = lambda w: pl.BlockSpec((None, tm, w), lambda b, i: (b, i, 0))
    out = pl.pallas_call(
        _outproj_kernel,
        grid=(bsz, t // tm),
        in_specs=[tok(HW), tok(HW), tok(L_W), tok(d),
                  pl.BlockSpec((None, 1, 3 * d), mod_idx),
                  pl.BlockSpec((1, d), lambda b, i: (0, 0)),
                  pl.BlockSpec((2 * HW + L_W, d), lambda b, i: (0, 0))],
        out_specs=tok(d),
        out_shape=jax.ShapeDtypeStruct((bsz, t, d), F32),
        compiler_params=pltpu.CompilerParams(dimension_semantics=("arbitrary", "arbitrary"),
                                             vmem_limit_bytes=VMEM_LIMIT),
        name="outproj",
    )(ym, yr, yl, x, mod, g_post, w_out_b)
    return out.reshape(bsz0, t0, d)


def _mlstm_kernel(*refs, has_state_in, has_state_out):
    qkv_ref, oz_ref, g_ref, ng_ref = refs[0:4]
    refs = refs[4:]
    if has_state_in:
        c0_ref, n0_ref, m0_ref = refs[0:3]
        refs = refs[3:]
    y_ref = refs[0]
    refs = refs[1:]
    if has_state_out:
        c_out_ref, n_out_ref, m_out_ref = refs[0:3]
        refs = refs[3:]
    hacc, cn_s, m_s = refs
    ns, t = qkv_ref.shape[0], qkv_ref.shape[1]
    nc = t // CHUNK
    lane = _iota2((CHUNK, LANES), 1)
    row = _iota2((CHUNK, LANES), 0)
    lo = lane < HD
    s_idx = lane % HD
    causal = (s_idx <= row, s_idx >= row)
    r2 = _iota2((2 * CHUNK, 2 * CHUNK), 0)
    c2 = _iota2((2 * CHUNK, 2 * CHUNK), 1)
    same = (r2 // CHUNK) == (c2 // CHUNK)
    cum = ((same & (c2 <= r2)).astype(BF16), (same & (c2 >= r2)).astype(BF16))
    cum_rep = tuple(jnp.concatenate([m, m, m], axis=1) for m in cum)
    bd_ones = same.astype(F32)
    ones_rep = jnp.concatenate([same.astype(BF16), same.astype(BF16)], axis=0)
    bd_mask2 = jnp.concatenate([same, same], axis=1)
    colid = _iota2((1, LANES), 1)
    is_f = (colid % 8) >= 4

    hacc[...] = jnp.zeros_like(hacc)
    cn_s[...] = jnp.zeros_like(cn_s)
    m_s[...] = jnp.zeros_like(m_s)
    if has_state_in:
        m_s[...] = m0_ref[...]
        for s in range(ns):
            for d in range(2):
                for p in range(N_PAIRS):
                    for h in range(2):
                        cn_s[s, d, p, h * HD:(h + 1) * HD, h * HD:(h + 1) * HD] = c0_ref[s, d, 2 * p + h]
                    n_rows = jnp.where(same, jnp.broadcast_to(n0_ref[s, d, p], (2 * CHUNK, LANES)), 0.0)
                    cn_s[s, d, p, :, LANES:2 * LANES] = n_rows.T

    def gate_table(s, t0, d):
        gates = g_ref[s, pl.ds(t0, CHUNK), :]
        x = jnp.concatenate([gates, gates], axis=0)
        q = jnp.where(is_f, _cum_dot(cum_rep[d], x), x)
        return q, q.T

    chains = [(s, d, p) for s in range(ns) for d in range(2) for p in range(N_PAIRS)]

    def body(j, carry):
        t0s = (pl.multiple_of(j * CHUNK, CHUNK), pl.multiple_of((nc - 1 - j) * CHUNK, CHUNK))
        tabs = {(s, d): gate_table(s, t0s[d], d) for s in range(ns) for d in range(2)}
        st = []
        for s, d, p in chains:
            t0 = t0s[d]
            q_tab, q_tab_t = tabs[(s, d)]
            ji, jf = _gate_col(d, 0, p), _gate_col(d, 1, p)
            c = dict(t0=t0, sl=slice(p * LANES, (p + 1) * LANES))
            c['bcol'] = _pair_col(q_tab, jf, lo)
            c['icol'] = _pair_col(q_tab, ji, lo)
            brow = _pair_row(q_tab_t, jf, lo[0:1, :])
            irow = _pair_row(q_tab_t, ji, lo[0:1, :])
            c['q'] = qkv_ref[s, pl.ds(t0, CHUNK), p * LANES:(p + 1) * LANES].astype(F32)
            c['k'] = qkv_ref[s, pl.ds(t0, CHUNK), HW + p * LANES:HW + (p + 1) * LANES].astype(F32) * (HD ** -0.5)
            c['v'] = qkv_ref[s, pl.ds(t0, CHUNK), 2 * HW + p * LANES:2 * HW + (p + 1) * LANES].astype(F32)
            c['mprev'] = m_s[s, d, p]
            c['cn'] = cn_s[s, d, p]
            dmat = jnp.where(causal[d], c['bcol'] - brow + irow, -jnp.inf)
            inter = c['bcol'] + c['mprev']
            mx = jnp.where(lo,
                           jnp.max(jnp.where(lo, dmat, -jnp.inf), axis=1, keepdims=True),
                           jnp.max(jnp.where(lo, -jnp.inf, dmat), axis=1, keepdims=True))
            c['m_t'] = jnp.maximum(inter, mx)
            c['pexp'] = jnp.exp(dmat - c['m_t'])
            c['sc'] = jnp.exp(inter - c['m_t'])
            last = CHUNK - 1 if d == 0 else 0
            b_last = c['bcol'][last:last + 1, :]
            g = b_last - c['bcol'] + c['icol']
            c['m_new'] = jnp.maximum(b_last + c['mprev'], jnp.max(g, axis=0, keepdims=True))
            c['dec'] = jnp.exp(b_last + c['mprev'] - c['m_new'])
            c['wk'] = jnp.exp(g - c['m_new'])
            st.append(c)
        for c in st:
            c['s'] = _bdot_nt(c['q'], _bd_stack(c['k'], lo)) * c['pexp']
        for c in st:
            c['a1'] = _bdot(c['q'], c['cn'])
        for c in st:
            c['upd'] = _bdot_tn(c['wk'] * c['k'], jnp.concatenate([c['v'], jnp.ones_like(c['v'])], axis=1))
        for c in st:
            c['a2'] = _bdot(c['s'], jnp.concatenate([_bd_stack(c['v'], lo), bd_ones], axis=1))
        for (s, d, p), c in zip(chains, st):
            num = c['sc'] * c['a1'][:, 0:LANES] + c['a2'][:, 0:LANES]
            den = c['sc'] * c['a1'][:, LANES:] + c['a2'][:, LANES:]
            h = num / jnp.maximum(jnp.abs(den), jnp.exp(-c['m_t']))
            hacc[s, pl.ds(c['t0'], CHUNK), c['sl']] = hacc[s, pl.ds(c['t0'], CHUNK), c['sl']] + h
            cn_s[s, d, p] = (jnp.concatenate([c['dec'], c['dec']], axis=1) * c['cn']
                             + jnp.where(bd_mask2, c['upd'], 0.0))
            m_s[s, d, p] = c['m_new']
        return carry

    lax.fori_loop(0, nc, body, 0)
    if has_state_out:
        m_out_ref[...] = m_s[...]
        lo2 = _iota2((1, LANES), 1) < HD
        for s in range(ns):
            for d in range(2):
                for p in range(N_PAIRS):
                    for h in range(2):
                        c_out_ref[s, d, 2 * p + h] = cn_s[s, d, p, h * HD:(h + 1) * HD, h * HD:(h + 1) * HD]
                    n_t = cn_s[s, d, p, :, LANES:2 * LANES].T
                    n_out_ref[s, d, p] = jnp.where(lo2, n_t[0:1, :], n_t[HD:HD + 1, :])

    ng = ng_ref[...]
    rb = min(t, 256)

    def epilogue(i, carry):
        r0 = pl.multiple_of(i * rb, rb)
        for s in range(ns):
            h = hacc[s, pl.ds(r0, rb), :]
            hn = h * lax.rsqrt(_head_sum(h * h, ones_rep) * (1.0 / HD) + EPS) * ng
            o = oz_ref[s, pl.ds(r0, rb), 0:HW].astype(F32)
            z = oz_ref[s, pl.ds(r0, rb), HW:2 * HW].astype(F32)
            y_ref[s, pl.ds(r0, rb), :] = (hn * _sigmoid(o) * _silu(z)).astype(y_ref.dtype)
        return carry

    lax.fori_loop(0, t // rb, epilogue, 0)


def _seq_spec(ns, t, w, single_buffer):
    if single_buffer:
        return pl.BlockSpec((ns, t, w), lambda b: (b, 0, 0), pipeline_mode=pl.Buffered(1))
    return pl.BlockSpec((ns, t, w), lambda b: (b, 0, 0))


def _full_spec(shape):
    n = len(shape)
    return pl.BlockSpec(shape, lambda b: (0,) * n)


def _state_spec(ns, shape, layer=None):
    if layer is None:
        return pl.BlockSpec((ns,) + shape, lambda b: (b,) + (0,) * len(shape))
    return pl.BlockSpec((ns, None) + shape, lambda b: (b, layer) + (0,) * len(shape))


def _mixer_plan(bsz, in_bytes_per_seq, other_bytes_per_seq):
    for ns, bufs in ((2, 2), (2, 1), (1, 2), (1, 1)):
        if bsz % ns == 0 and ns * (bufs * in_bytes_per_seq + other_bytes_per_seq) <= MIXER_VMEM_BUDGET:
            return ns, bufs == 1
    return 1, True


def _pair_rows(x):
    return jnp.repeat(x.reshape(x.shape[:-1] + (N_PAIRS, 1, 2)), HD, axis=-1)


def _mlstm(qkv, oz, gates, norm_g, state_in, want_state, layer=None):
    bsz, t, _ = qkv.shape
    ns, big = _mixer_plan(bsz, t * (5 * HW * qkv.dtype.itemsize + LANES * 4), t * (HW * 2 * 2 + HW * 4))
    c_shape, row_shape = (2, N_HEADS, HD, HD), (2, N_PAIRS, 1, LANES)
    in_specs = [_seq_spec(ns, t, 3 * HW, big), _seq_spec(ns, t, 2 * HW, big),
                _seq_spec(ns, t, LANES, big), _full_spec((1, HW))]
    args = [qkv, oz, gates, norm_g]
    if state_in is not None:
        c0, n0, m0 = state_in
        in_specs += [_state_spec(ns, c_shape, layer), _state_spec(ns, row_shape, layer),
                     _state_spec(ns, row_shape, layer)]
        args += [c0, n0.reshape(n0.shape[:-3] + row_shape), _pair_rows(m0)]
    out_specs = [_seq_spec(ns, t, HW, False)]
    out_shape = [jax.ShapeDtypeStruct((bsz, t, HW), BF16)]
    if want_state:
        out_specs += [_state_spec(ns, c_shape), _state_spec(ns, row_shape), _state_spec(ns, row_shape)]
        out_shape += [jax.ShapeDtypeStruct((bsz,) + c_shape, F32),
                      jax.ShapeDtypeStruct((bsz,) + row_shape, F32),
                      jax.ShapeDtypeStruct((bsz,) + row_shape, F32)]
    outs = pl.pallas_call(
        functools.partial(_mlstm_kernel, has_state_in=state_in is not None, has_state_out=want_state),
        grid=(bsz // ns,),
        in_specs=in_specs,
        out_specs=out_specs,
        out_shape=out_shape,
        scratch_shapes=[pltpu.VMEM((ns, t, HW), F32),
                        pltpu.VMEM((ns, 2, N_PAIRS, 2 * CHUNK, 2 * LANES), F32),
                        pltpu.VMEM((ns,) + row_shape, F32)],
        compiler_params=pltpu.CompilerParams(dimension_semantics=("arbitrary",),
                                             vmem_limit_bytes=VMEM_LIMIT),
        name="mlstm",
    )(*args)
    if not want_state:
        return outs[0], None
    y, c1, n1, m1 = outs
    return y, (c1, n1.reshape(bsz, 2, N_HEADS, HD), m1[:, :, :, 0, ::HD].reshape(bsz, 2, N_HEADS))


def _rwkv_kernel(*refs, has_state_in, has_state_out):
    (rs_ref, rz_ref, w0_ref, a0_ref, w2_ref, a2_ref, kk_ref, ka_ref, rk_ref, ng_ref) = refs[0:10]
    refs = refs[10:]
    if has_state_in:
        s0_ref = refs[0]
        refs = refs[1:]
    y_ref = refs[0]
    refs = refs[1:]
    if has_state_out:
        s_out_ref = refs[0]
        refs = refs[1:]
    yacc, bonus, s_s = refs
    ns, t = rs_ref.shape[0], rs_ref.shape[1]
    nc = t // CHUNK
    lane = _iota2((CHUNK, LANES), 1)
    row = _iota2((CHUNK, LANES), 0)
    lo = lane < HD
    s_idx = lane % HD
    strict = (s_idx < row, s_idx > row)
    incl = (s_idx <= row, s_idx >= row)
    eye_pk = (s_idx == row).astype(F32)
    r64 = _iota2((CHUNK, CHUNK), 0)
    c64 = _iota2((CHUNK, CHUNK), 1)
    cum = ((c64 <= r64).astype(BF16), (c64 >= r64).astype(BF16))
    cum_rep = tuple(jnp.concatenate([m, m, m], axis=1) for m in cum)
    r2 = _iota2((2 * CHUNK, 2 * CHUNK), 0)
    c2 = _iota2((2 * CHUNK, 2 * CHUNK), 1)
    bd_mask = (r2 // CHUNK) == (c2 // CHUNK)
    ones_rep = jnp.concatenate([bd_mask.astype(BF16), bd_mask.astype(BF16)], axis=0)
    kk = kk_ref[...]
    ka = ka_ref[...]
    rk = rk_ref[...]

    yacc[...] = jnp.zeros_like(yacc)
    s_s[...] = jnp.zeros_like(s_s)
    if has_state_in:
        for s in range(ns):
            for d in range(2):
                for p in range(N_PAIRS):
                    for h in range(2):
                        s_s[s, d, p, h * HD:(h + 1) * HD, h * HD:(h + 1) * HD] = s0_ref[s, d, 2 * p + h]

    def prep(s, c, d):
        t0 = pl.multiple_of(c * CHUNK, CHUNK)
        blk = rs_ref[s, pl.ds(t0, CHUNK), :].astype(F32)
        r = blk[:, 0:HW]
        k = blk[:, HW:2 * HW]
        v = blk[:, 2 * HW:3 * HW]
        wl = blk[:, 3 * HW:3 * HW + 2 * LORA]
        al = blk[:, 3 * HW + 2 * LORA:3 * HW + 4 * LORA]
        dsl = slice(d * HW, (d + 1) * HW)
        logw = -RWKV_DECAY_SCALE * _sigmoid(w0_ref[:, dsl] + _bdot(jnp.tanh(wl), w2_ref[:, dsl]))
        a = _sigmoid(a0_ref[:, dsl] + _bdot(al, a2_ref[:, dsl]))
        kappa = k * kk
        khat = kappa * lax.rsqrt(jnp.maximum(_head_sum(kappa * kappa, ones_rep), 1e-24))
        kt = k * (1.0 + (a - 1.0) * ka)
        b = khat * a
        if d == 0:
            bonus[s, pl.ds(t0, CHUNK), :] = _head_sum(r * k * rk, ones_rep) * v
        last = CHUNK - 1 if d == 0 else 0
        logp = _cum_dot(cum_rep[d], logw)
        logp_last = logp[last:last + 1, :]
        e_neg = jnp.exp(-logp)
        e_last = jnp.exp(logp_last - logp)
        return dict(t0=t0, v=v, alpha=jnp.exp(logp - logw) * khat, beta=b * e_neg, kap=kt * e_neg,
                    rho=r * jnp.exp(logp), beta_l=b * e_last, kap_l=kt * e_last,
                    p_last=jnp.exp(logp_last))

    chains = [(s, d, p) for s in range(ns) for d in range(2) for p in range(N_PAIRS)]

    def body(j, carry):
        pre = {(s, d): prep(s, j if d == 0 else nc - 1 - j, d) for s in range(ns) for d in range(2)}
        st = []
        for s, d, p in chains:
            sl = slice(p * LANES, (p + 1) * LANES)
            c = {key: (val if key == 't0' else val[:, sl]) for key, val in pre[(s, d)].items()}
            c['sl'] = sl
            st.append(c)
        for c in st:
            c['g'] = _bdot_nt(jnp.concatenate([c['alpha'], c['rho']], axis=0),
                              jnp.concatenate([_bd_stack(c['beta'], lo), _bd_stack(c['kap'], lo)], axis=0))
        for (s, d, p), c in zip(chains, st):
            g = c.pop('g')
            c['n'] = jnp.where(strict[d], g[0:CHUNK, 0:LANES], 0.0)
            c['mk'] = jnp.where(strict[d], g[0:CHUNK, LANES:], 0.0)
            c['mrb'] = jnp.where(incl[d], g[CHUNK:, 0:LANES], 0.0)
            c['mrk'] = jnp.where(incl[d], g[CHUNK:, LANES:], 0.0)
            c['inv'] = eye_pk - c['n']
        for c in st:
            c['pw'] = _bdot(c['n'], _bd_stack(c['n'], lo))
        for c in st:
            c['mkv'] = _bdot(c['mk'], _bd_stack(c['v'], lo))
        for _ in range(4):
            for c in st:
                rp = _bdot(jnp.concatenate([c['inv'], c['pw']], axis=0), _bd_stack(c['pw'], lo))
                c['inv'] = c['inv'] + rp[0:CHUNK]
                c['pw'] = rp[CHUNK:]
        for c in st:
            c['inv'] = c['inv'] + _bdot(c['inv'], _bd_stack(c['pw'], lo))
        for c in st:
            c['wu'] = _bdot(c['inv'], jnp.concatenate([_bd_stack(c['alpha'], lo), _bd_stack(c['mkv'], lo)], axis=1))
        for (s, d, p), c in zip(chains, st):
            c['s_prev'] = s_s[s, d, p]
            ws = _bdot_nt(jnp.concatenate([c['wu'][:, 0:LANES], c['rho']], axis=0), c['s_prev'])
            c['u'] = ws[0:CHUNK] + c['wu'][:, LANES:]
            c['ys'] = ws[CHUNK:]
        for c in st:
            c['upd'] = _bdot_tn(jnp.concatenate([c['v'], -c['u']], axis=0),
                                jnp.concatenate([c['kap_l'], c['beta_l']], axis=0))
        for c in st:
            c['y'] = c['ys'] + _bdot(jnp.concatenate([c['mrk'], -c['mrb']], axis=1),
                                     jnp.concatenate([_bd_stack(c['v'], lo), _bd_stack(c['u'], lo)], axis=0))
        for (s, d, p), c in zip(chains, st):
            s_s[s, d, p] = c['s_prev'] * c['p_last'] + jnp.where(bd_mask, c['upd'], 0.0)
            yacc[s, pl.ds(c['t0'], CHUNK), c['sl']] = yacc[s, pl.ds(c['t0'], CHUNK), c['sl']] + c['y']
        return carry

    lax.fori_loop(0, nc, body, 0)
    if has_state_out:
        for s in range(ns):
            for d in range(2):
                for p in range(N_PAIRS):
                    for h in range(2):
                        s_out_ref[s, d, 2 * p + h] = s_s[s, d, p, h * HD:(h + 1) * HD, h * HD:(h + 1) * HD]

    ng = ng_ref[...]
    rb = min(t, 256)

    def epilogue(i, carry):
        r0 = pl.multiple_of(i * rb, rb)
        for s in range(ns):
            y = yacc[s, pl.ds(r0, rb), :]
            yn = (y * lax.rsqrt(_head_sum(y * y, ones_rep) * (1.0 / HD) + EPS) * ng
                  + bonus[s, pl.ds(r0, rb), :])
            y_ref[s, pl.ds(r0, rb), :] = (yn * _silu(rz_ref[s, pl.ds(r0, rb), :].astype(F32))).astype(y_ref.dtype)
        return carry

    lax.fori_loop(0, t // rb, epilogue, 0)


def _rwkv(rs, rz, w0, a0, w2bd, a2bd, kk, ka, rk, norm_g, state_in, want_state, layer=None):
    bsz, t, _ = rs.shape
    ns, big = _mixer_plan(bsz, t * (R_SHIFT + HW) * rs.dtype.itemsize, t * (HW * 2 * 2 + 2 * HW * 4))
    s_shape = (2, N_HEADS, HD, HD)
    in_specs = [_seq_spec(ns, t, R_SHIFT, big), _seq_spec(ns, t, HW, big),
                _full_spec((1, 2 * HW)), _full_spec((1, 2 * HW)),
                _full_spec((2 * LORA, 2 * HW)), _full_spec((2 * LORA, 2 * HW)),
                _full_spec((1, HW)), _full_spec((1, HW)), _full_spec((1, HW)), _full_spec((1, HW))]
    args = [rs, rz, w0, a0, w2bd, a2bd, kk, ka, rk, norm_g]
    if state_in is not None:
        in_specs.append(_state_spec(ns, s_shape, layer))
        args.append(state_in)
    out_specs = [_seq_spec(ns, t, HW, False)]
    out_shape = [jax.ShapeDtypeStruct((bsz, t, HW), BF16)]
    if want_state:
        out_specs.append(_state_spec(ns, s_shape))
        out_shape.append(jax.ShapeDtypeStruct((bsz,) + s_shape, F32))
    outs = pl.pallas_call(
        functools.partial(_rwkv_kernel, has_state_in=state_in is not None, has_state_out=want_state),
        grid=(bsz // ns,),
        in_specs=in_specs,
        out_specs=out_specs,
        out_shape=out_shape,
        scratch_shapes=[pltpu.VMEM((ns, t, HW), F32), pltpu.VMEM((ns, t, HW), F32),
                        pltpu.VMEM((ns, 2, N_PAIRS, 2 * CHUNK, LANES), F32)],
        compiler_params=pltpu.CompilerParams(dimension_semantics=("arbitrary",),
                                             vmem_limit_bytes=VMEM_LIMIT),
        name="rwkv",
    )(*args)
    return (outs[0], outs[1]) if want_state else (outs[0], None)


def _lru_kernel(xz_ref, conv_ref, cb_ref, w_ref, b_ref, lam_ref, h0_ref, y_ref, hfin_ref, a_s, b_s, acc):
    t = xz_ref.shape[0]
    nb = t // LRU_BLOCK
    n_groups = LRU_BLOCK // SUBLANES
    row = _iota2((LRU_BLOCK, L_W), 0)
    sub = _iota2((n_groups, SUBLANES, L_W), 1)
    conv = conv_ref[...]
    cb = cb_ref[...]
    sp = _softplus(-lam_ref[...])

    def coefficients(j, carry):
        t0 = pl.multiple_of(j * LRU_BLOCK, LRU_BLOCK)
        x = xz_ref[pl.ds(t0, LRU_BLOCK), 0:L_W].astype(F32)
        prev_rows = xz_ref[pl.ds(pl.multiple_of(jnp.maximum(t0 - HALO, 0), HALO), HALO), 0:L_W]
        next_rows = xz_ref[pl.ds(pl.multiple_of(jnp.minimum(t0 + LRU_BLOCK, t - HALO), HALO), HALO), 0:L_W]
        prev_rows = jnp.where(j > 0, prev_rows.astype(F32), 0.0)
        next_rows = jnp.where(j < nb - 1, next_rows.astype(F32), 0.0)
        p1, p2 = prev_rows[HALO - 1:HALO, :], prev_rows[HALO - 2:HALO - 1, :]
        xm1 = jnp.where(row == 0, p1, pltpu.roll(x, 1, 0))
        xm2 = jnp.where(row == 0, p2, jnp.where(row == 1, p1, pltpu.roll(x, 2, 0)))
        xp1 = jnp.where(row == LRU_BLOCK - 1, next_rows[0:1, :], pltpu.roll(x, LRU_BLOCK - 1, 0))
        xc = conv[0:1, :] * xm2 + conv[1:2, :] * xm1 + conv[2:3, :] * x + conv[3:4, :] * xp1 + cb
        pre = _bdot(xc, w_ref[...]) + b_ref[...]
        for d in range(2):
            rg = _sigmoid(pre[:, 2 * d * L_W:(2 * d + 1) * L_W])
            ig = _sigmoid(pre[:, (2 * d + 1) * L_W:(2 * d + 2) * L_W])
            log_a = -LRU_C * rg * sp[:, d * L_W:(d + 1) * L_W]
            a_step = jnp.exp(log_a)
            b_step = jnp.sqrt(jnp.tanh(-log_a) * (1.0 + a_step * a_step)) * ig * xc
            a_cum = a_step.reshape(n_groups, SUBLANES, L_W)
            b_cum = b_step.reshape(n_groups, SUBLANES, L_W)
            k = 1
            while k < SUBLANES:
                keep = (sub >= k) if d == 0 else (sub < SUBLANES - k)
                shift = k if d == 0 else SUBLANES - k
                a_sh = jnp.where(keep, pltpu.roll(a_cum, shift, 1), 1.0)
                b_sh = jnp.where(keep, pltpu.roll(b_cum, shift, 1), 0.0)
                b_cum = a_cum * b_sh + b_cum
                a_cum = a_cum * a_sh
                k *= 2
            a_s[d, pl.ds(t0, LRU_BLOCK), :] = a_cum.reshape(LRU_BLOCK, L_W)
            b_s[d, pl.ds(t0, LRU_BLOCK), :] = b_cum.reshape(LRU_BLOCK, L_W)
        return carry

    lax.fori_loop(0, nb, coefficients, 0)

    def scan_block(d, j, carry):
        t0 = pl.multiple_of(j * LRU_BLOCK, LRU_BLOCK)
        a_cum = a_s[d, pl.ds(t0, LRU_BLOCK), :]
        b_cum = b_s[d, pl.ds(t0, LRU_BLOCK), :]
        pieces = [None] * n_groups
        for gi in (range(n_groups) if d == 0 else reversed(range(n_groups))):
            rows = slice(gi * SUBLANES, (gi + 1) * SUBLANES)
            hg = b_cum[rows, :] + a_cum[rows, :] * carry
            carry = hg[SUBLANES - 1:SUBLANES, :] if d == 0 else hg[0:1, :]
            pieces[gi] = hg
        return t0, jnp.concatenate(pieces, axis=0), carry

    def forward(j, carry):
        t0, h, carry = scan_block(0, j, carry)
        acc[pl.ds(t0, LRU_BLOCK), :] = h
        return carry

    def backward(jj, carry):
        t0, h, carry = scan_block(1, nb - 1 - jj, carry)
        z = xz_ref[pl.ds(t0, LRU_BLOCK), L_W:2 * L_W].astype(F32)
        y_ref[pl.ds(t0, LRU_BLOCK), :] = ((acc[pl.ds(t0, LRU_BLOCK), :] + h) * _silu(z)).astype(y_ref.dtype)
        return carry

    hfin_ref[0] = lax.fori_loop(0, nb, forward, h0_ref[0])
    hfin_ref[1] = lax.fori_loop(0, nb, backward, h0_ref[1])


def _lru(xz, conv, conv_b, wbd, bias, lam, h0):
    bsz, t, _ = xz.shape
    st = lambda shape: pl.BlockSpec((None,) + shape, lambda b: (b,) + (0,) * len(shape))
    return pl.pallas_call(
        _lru_kernel,
        grid=(bsz,),
        in_specs=[_seq_spec(None, t, 2 * L_W, False), _full_spec((CONV_W, L_W)), _full_spec((1, L_W)),
                  _full_spec((L_W, 4 * L_W)), _full_spec((1, 4 * L_W)), _full_spec((1, 2 * L_W)),
                  st((2, 1, L_W))],
        out_specs=[pl.BlockSpec((None, t, L_W), lambda b: (b, 0, 0)), st((2, 1, L_W))],
        out_shape=[jax.ShapeDtypeStruct((bsz, t, L_W), BF16),
                   jax.ShapeDtypeStruct((bsz, 2, 1, L_W), F32)],
        scratch_shapes=[pltpu.VMEM((2, t, L_W), F32), pltpu.VMEM((2, t, L_W), F32),
                        pltpu.VMEM((t, L_W), F32)],
        compiler_params=pltpu.CompilerParams(dimension_semantics=("arbitrary",),
                                             vmem_limit_bytes=VMEM_LIMIT),
        name="lru",
    )(xz, conv, conv_b, wbd, bias, lam, h0)


def _permute_w_in(w_in_l):
    valid = jnp.asarray(_GATE_SRC >= 0)
    gates = jnp.where(valid, w_in_l[:, 5 * HW:M_COLS][:, np.maximum(_GATE_SRC, 0)], 0.0)
    return jnp.concatenate([w_in_l[:, 0:5 * HW], gates, w_in_l[:, M_COLS:]], axis=1).astype(BF16)


def _gate_source_index():
    src = np.full((LANES,), -1, np.int32)
    for dd in range(2):
        for is_f in range(2):
            for p in range(N_PAIRS):
                for parity in range(2):
                    src[parity * HD + _gate_col(dd, is_f, p)] = (is_f * 2 * N_HEADS + dd * N_HEADS
                                                                 + 2 * p + parity)
    return src


_GATE_SRC = _gate_source_index()


def _gate_bias(m_bi_l, m_bf_l):
    flat = jnp.concatenate([m_bi_l.reshape(-1), m_bf_l.reshape(-1)])
    return jnp.where(jnp.asarray(_GATE_SRC >= 0), flat[np.maximum(_GATE_SRC, 0)], 0.0).reshape(1, LANES)


def _block_diag(blocks):
    n, a, b = blocks.shape
    eye = jnp.eye(n, dtype=blocks.dtype)
    return (eye[:, None, :, None] * blocks[:, :, None, :]).reshape(n * a, n * b)


def _layer_params(l, g_pre, g_post, w_in, w_out, m_bi, m_bf, m_norm, r_mu, r_w0, r_w2, r_a0, r_a2,
                  r_kk, r_ka, r_rk, r_norm, l_conv, l_conv_b, l_wa, l_ba, l_wx, l_bx, l_lambda):
    row = lambda v: v.reshape(1, -1)
    return dict(
        g_pre=row(g_pre[l]), g_post=row(g_post[l]),
        w_in=_permute_w_in(w_in[l]), w_out=w_out[l].astype(BF16),
        gate_bias=_gate_bias(m_bi[l], m_bf[l]), m_norm=row(m_norm[l]),
        r_mu=row(r_mu[l]), r_w0=row(r_w0[l]), r_a0=row(r_a0[l]),
        r_w2=_block_diag(r_w2[l]).astype(BF16), r_a2=_block_diag(r_a2[l]).astype(BF16),
        r_kk=row(r_kk[l]), r_ka=row(r_ka[l]), r_rk=row(r_rk[l]), r_norm=row(r_norm[l]),
        l_conv=l_conv[l], l_conv_b=row(l_conv_b[l]),
        l_w=jnp.concatenate([_block_diag(l_wa[l][0]), _block_diag(l_wx[l][0]),
                             _block_diag(l_wa[l][1]), _block_diag(l_wx[l][1])], axis=1).astype(BF16),
        l_b=jnp.concatenate([l_ba[l][0], l_bx[l][0], l_ba[l][1], l_bx[l][1]]).reshape(1, -1),
        l_lambda=row(l_lambda[l]),
    )


def _trunk_layer(x, mod, lp, states, want_state, per_seq_mod, grid_shift, layer=None):
    bsz = x.shape[0]
    qkv, oz, gates, rs, rz, lxz = _inproj(x, mod, lp['g_pre'], lp['w_in'], lp['r_mu'], lp['gate_bias'],
                                          per_seq_mod, grid_shift)
    m_in = None if states is None else states[0:3]
    r_in = None if states is None else states[3]
    if states is None:
        l_in = jnp.zeros((bsz, 2, 1, L_W), F32)
    else:
        l_in = (states[4] if layer is None else states[4][:, layer])[:, :, None, :]
    y_m, m_out = _mlstm(qkv, oz, gates, lp['m_norm'], m_in, want_state, layer)
    y_r, r_out = _rwkv(rs, rz, lp['r_w0'], lp['r_a0'], lp['r_w2'], lp['r_a2'], lp['r_kk'],
                       lp['r_ka'], lp['r_rk'], lp['r_norm'], r_in, want_state, layer)
    y_l, l_out = _lru(lxz, lp['l_conv'], lp['l_conv_b'], lp['l_w'], lp['l_b'], lp['l_lambda'], l_in)
    y = _outproj(y_m, y_r, y_l, x, mod, lp['g_post'], lp['w_out'], per_seq_mod)
    new_states = m_out + (r_out, l_out[:, :, 0, :]) if want_state else None
    return y, new_states


def kernel(x_prompt, x_sample, c, state_mlstm_C, state_mlstm_n, state_mlstm_m, state_rwkv, state_rglru, c_ctx, g_pre, g_post, w_mod, b_mod, w_in, w_out, m_bi, m_bf, m_norm, r_mu, r_w0, r_w2, r_a0, r_a2, r_kk, r_ka, r_rk, r_norm, l_conv, l_conv_b, l_wa, l_ba, l_wx, l_bx, l_lambda):
    depth = w_in.shape[0]
    bs = x_sample.shape[0]
    d = x_prompt.shape[-1]
    rows = -(-(1 + bs) // SUBLANES) * SUBLANES
    cc = jnp.zeros((rows, d), F32).at[0].set(c_ctx).at[1:1 + bs].set(c)
    mod = _modulation(cc, w_mod, b_mod)
    xp, xs = x_prompt, x_sample
    cache = tuple(a.astype(F32) for a in
                  (state_mlstm_C, state_mlstm_n, state_mlstm_m, state_rwkv, state_rglru))
    new_states = []
    for l in range(depth):
        lp = _layer_params(l, g_pre, g_post, w_in, w_out, m_bi, m_bf, m_norm, r_mu, r_w0, r_w2, r_a0,
                           r_a2, r_kk, r_ka, r_rk, r_norm, l_conv, l_conv_b, l_wa, l_ba, l_wx, l_bx,
                           l_lambda)
        xp, st = _trunk_layer(xp, mod[l, 0:1][:, None, :], lp, None, True, False, False)
        new_states.append(st)
        xs, _ = _trunk_layer(xs, mod[l, 1:1 + bs][:, None, :], lp, cache, False, True, True, layer=l)
    stacked = tuple(jnp.stack([st[i] for st in new_states], axis=1) for i in range(5))
    return (xp, xs) + stacked
```

```python
import functools

import numpy as np
import jax
import jax.numpy as jnp
from jax import lax
from jax.experimental import pallas as pl
from jax.experimental.pallas import tpu as pltpu

F32 = jnp.float32
BF16 = jnp.bfloat16
HIGHEST = lax.Precision.HIGHEST

D_MODEL = 1024
EPS = 1e-6
HD = 64
N_HEADS = 6
N_PAIRS = N_HEADS // 2
HW = N_HEADS * HD
CHUNK = 64
GRID_W = 64
LORA = 64
R_SHIFT = 3 * HW + 4 * LORA
L_W = 256
L_BLOCKS = 4
CONV_W = 4
LRU_C = 8.0
RWKV_DECAY_SCALE = 0.6065306597126334
M_COLS = 5 * HW + 4 * N_HEADS
IN_COLS = M_COLS + R_SHIFT + HW + 2 * L_W

LANES = 128
SUBLANES = 8
HALO = 16
HALO_ROWS = GRID_W
LRU_BLOCK = 128
MIB = 1024 * 1024
V7X_VMEM_BYTES = 64 * MIB
VMEM_LIMIT = V7X_VMEM_BYTES - 4 * MIB
MIXER_SPILL_BYTES = 10 * MIB
MIXER_VMEM_BUDGET = VMEM_LIMIT - MIXER_SPILL_BYTES

SEG_WIDTHS = (3 * HW, 2 * HW, LANES, R_SHIFT, HW, 2 * L_W)
SEG_DTYPES = (BF16, BF16, F32, BF16, BF16, BF16)
SEG_OFFS = tuple(int(v) for v in np.cumsum((0,) + SEG_WIDTHS))
IN_COLS_PAD = SEG_OFFS[-1]
N_TILE = 512
IN_TILE = 512
OUT_TILE = 1024


def _gate_col(d, is_f, p):
    return 8 * d + 4 * is_f + p


def _bdot(a, b):
    return jnp.dot(a.astype(BF16), b.astype(BF16), preferred_element_type=F32)


def _bdot_nt(a, b):
    return lax.dot_general(a.astype(BF16), b.astype(BF16), (((1,), (1,)), ((), ())),
                           preferred_element_type=F32)


def _bdot_tn(a, b):
    return lax.dot_general(a.astype(BF16), b.astype(BF16), (((0,), (0,)), ((), ())),
                           preferred_element_type=F32)


def _fdot(a, b):
    return jnp.dot(a, b, precision=HIGHEST, preferred_element_type=F32)


def _split_bf16(x, terms):
    out = []
    for _ in range(terms - 1):
        hi = x.astype(BF16)
        out.append(hi)
        x = x - hi.astype(F32)
    out.append(x.astype(BF16))
    return out


def _cum_dot(cum_rep, x, terms=3):
    return jnp.dot(cum_rep, jnp.concatenate(_split_bf16(x, terms), axis=0), preferred_element_type=F32)


def _head_sum(x, ones_rep, terms=2):
    outs = []
    for p in range(x.shape[1] // LANES):
        parts = _split_bf16(x[:, p * LANES:(p + 1) * LANES], terms)
        outs.append(jnp.dot(jnp.concatenate(parts, axis=1), ones_rep, preferred_element_type=F32))
    return outs[0] if len(outs) == 1 else jnp.concatenate(outs, axis=1)


def _sigmoid(x):
    return 0.5 * jnp.tanh(0.5 * x) + 0.5


def _silu(x):
    return x * _sigmoid(x)


def _softplus(x):
    return jnp.maximum(x, 0.0) + jnp.log1p(jnp.exp(-jnp.abs(x)))


def _log_sigmoid(x):
    return -_softplus(-x)


def _iota2(shape, dim):
    return lax.broadcasted_iota(jnp.int32, shape, dim)


def _bd_stack(x, lo):
    return jnp.concatenate([jnp.where(lo, x, 0.0), jnp.where(lo, 0.0, x)], axis=0)


def _pair_col(q, j, lo):
    return jnp.where(lo, q[0:CHUNK, j:j + 1], q[0:CHUNK, HD + j:HD + j + 1])


def _pair_row(qt, j, lo_row):
    return jnp.where(lo_row, qt[j:j + 1, :], qt[HD + j:HD + j + 1, :])


def _mod_kernel(c_ref, w_ref, b_ref, o_ref):
    o_ref[...] = _fdot(_silu(c_ref[...]), w_ref[...]) + b_ref[...]


def _modulation(cc, w_mod, b_mod):
    depth, d, n = w_mod.shape
    rows = cc.shape[0]
    return pl.pallas_call(
        _mod_kernel,
        grid=(depth, n // N_TILE),
        in_specs=[pl.BlockSpec((rows, d), lambda l, j: (0, 0)),
                  pl.BlockSpec((None, d, N_TILE), lambda l, j: (l, 0, j)),
                  pl.BlockSpec((None, 1, N_TILE), lambda l, j: (l, 0, j))],
        out_specs=pl.BlockSpec((None, rows, N_TILE), lambda l, j: (l, 0, j)),
        out_shape=jax.ShapeDtypeStruct((depth, rows, n), F32),
        compiler_params=pltpu.CompilerParams(dimension_semantics=("arbitrary", "arbitrary")),
        name="modulation",
    )(cc, w_mod, b_mod.reshape(depth, 1, n))


SEG_QKV, SEG_OZ, SEG_GATES, SEG_RS, SEG_RZ, SEG_LXZ = range(6)


def _inproj_kernel(x_ref, xprev_ref, xnext_ref, mod_ref, g_ref, w_ref, mu_ref, gb_ref, *refs,
                   grid_shift, seq_len):
    tm, d = x_ref.shape
    single_tile = tm >= seq_len
    span = seq_len if single_tile else tm
    i = pl.program_id(1)
    n_tiles = pl.num_programs(1)
    mod = mod_ref[...]
    g = g_ref[...]

    def norm_mod(x):
        h = x * lax.rsqrt(jnp.mean(x * x, axis=-1, keepdims=True) + EPS) * g
        return (h * (1.0 + mod[:, d:2 * d]) + mod[:, 0:d]).astype(BF16)

    out_refs, h_s = refs[:-1], refs[-1]
    off = 0 if single_tile else HALO_ROWS
    h_s[off:off + tm, :] = norm_mod(x_ref[...])
    if not single_tile:
        h_s[0:off, :] = norm_mod(xprev_ref[...])
        h_s[off + tm:, :] = norm_mod(xnext_ref[...])
    has_prev = i > 0
    has_next = i < n_tiles - 1
    row = _iota2((tm, 1), 0) % span
    a, b = SEG_OFFS[SEG_RS], SEG_OFFS[SEG_RS + 1]
    o_ref = out_refs[SEG_RS]
    for n0 in range(a, b, N_TILE):
        n1 = min(n0 + N_TILE, b)
        u_all = jnp.dot(h_s[...], w_ref[:, n0:n1], preferred_element_type=F32)
        if single_tile:
            u = u_all
            u_prev = u_next = jnp.zeros((HALO_ROWS, n1 - n0), F32)
        else:
            u = u_all[HALO_ROWS:HALO_ROWS + tm]
            u_prev = jnp.where(has_prev, u_all[0:HALO_ROWS], 0.0)
            u_next = jnp.where(has_next, u_all[HALO_ROWS + tm:], 0.0)
        left = pltpu.roll(u, 1, 0)
        right = pltpu.roll(u, tm - 1, 0)
        if grid_shift:
            left = jnp.where(row % GRID_W == 0, 0.0, left)
            right = jnp.where(row % GRID_W == GRID_W - 1, 0.0, right)
            up = jnp.concatenate([u_prev, u[0:tm - GRID_W]], axis=0)
            down = jnp.concatenate([u[GRID_W:tm], u_next], axis=0)
            if tm > span:
                up = jnp.where(row < GRID_W, 0.0, up)
                down = jnp.where(row >= span - GRID_W, 0.0, down)
            sh = 0.25 * (up + down + left + right)
        else:
            left = jnp.where(row == 0, u_prev[HALO_ROWS - 1:HALO_ROWS, :], left)
            right = jnp.where(row == span - 1, u_next[0:1, :], right)
            sh = 0.5 * (left + right)
        o_ref[:, n0 - a:n1 - a] = (u + mu_ref[:, n0 - a:n1 - a] * (sh - u)).astype(o_ref.dtype)

    is_f = (_iota2((1, LANES), 1) % 8) >= 4
    a, b = SEG_OFFS[SEG_GATES], SEG_OFFS[SEG_GATES + 1]
    pre = jnp.dot(h_s[off:off + tm, :], w_ref[:, a:b], preferred_element_type=F32) + gb_ref[...]
    out_refs[SEG_GATES][...] = jnp.where(is_f, _log_sigmoid(pre), pre)

    for seg in (SEG_QKV, SEG_OZ, SEG_RZ, SEG_LXZ):
        o_ref, a, b = out_refs[seg], SEG_OFFS[seg], SEG_OFFS[seg + 1]
        for n0 in range(a, b, N_TILE):
            n1 = min(n0 + N_TILE, b)
            o_ref[:, n0 - a:n1 - a] = jnp.dot(h_s[off:off + tm, :], w_ref[:, n0:n1],
                                              preferred_element_type=F32).astype(o_ref.dtype)


def _inproj(x, mod, g_pre, w_in_p, r_mu, gate_bias, per_seq_mod, grid_shift):
    bsz0, t0, d = x.shape
    group = 1
    if not per_seq_mod and t0 < IN_TILE and IN_TILE % t0 == 0 and bsz0 % (IN_TILE // t0) == 0:
        group = IN_TILE // t0
    x = x.reshape(bsz0 // group, group * t0, d)
    bsz, t, _ = x.shape
    tm = min(t, IN_TILE)
    hb = tm // HALO_ROWS
    n_halo = t // HALO_ROWS
    mod_idx = (lambda b, i: (b, 0, 0)) if per_seq_mod else (lambda b, i: (0, 0, 0))
    outs = pl.pallas_call(
        functools.partial(_inproj_kernel, grid_shift=grid_shift, seq_len=t0),
        grid=(bsz, t // tm),
        in_specs=[pl.BlockSpec((None, tm, d), lambda b, i: (b, i, 0)),
                  pl.BlockSpec((None, HALO_ROWS, d), lambda b, i: (b, jnp.maximum(i * hb - 1, 0), 0)),
                  pl.BlockSpec((None, HALO_ROWS, d),
                               lambda b, i: (b, jnp.minimum((i + 1) * hb, n_halo - 1), 0)),
                  pl.BlockSpec((None, 1, 3 * d), mod_idx),
                  pl.BlockSpec((1, d), lambda b, i: (0, 0)),
                  pl.BlockSpec((d, IN_COLS_PAD), lambda b, i: (0, 0), pipeline_mode=pl.Buffered(1)),
                  pl.BlockSpec((1, R_SHIFT), lambda b, i: (0, 0)),
                  pl.BlockSpec((1, LANES), lambda b, i: (0, 0))],
        out_specs=[pl.BlockSpec((None, tm, w), lambda b, i: (b, i, 0)) for w in SEG_WIDTHS],
        out_shape=[jax.ShapeDtypeStruct((bsz, t, w), dt) for w, dt in zip(SEG_WIDTHS, SEG_DTYPES)],
        scratch_shapes=[pltpu.VMEM((tm if t == tm else tm + 2 * HALO_ROWS, d), BF16)],
        compiler_params=pltpu.CompilerParams(dimension_semantics=("arbitrary", "arbitrary"),
                                             vmem_limit_bytes=VMEM_LIMIT),
        name="inproj",
    )(x, x, x, mod, g_pre, w_in_p, r_mu, gate_bias)
    return [o.reshape(bsz0, t0, o.shape[-1]) for o in outs]


def _outproj_kernel(ym_ref, yr_ref, yl_ref, x_ref, mod_ref, g_ref, w_ref, o_ref):
    d = x_ref.shape[-1]
    o = jnp.dot(ym_ref[...], w_ref[0:HW, :], preferred_element_type=F32)
    o = o + jnp.dot(yr_ref[...], w_ref[HW:2 * HW, :], preferred_element_type=F32)
    o = o + jnp.dot(yl_ref[...], w_ref[2 * HW:2 * HW + L_W, :], preferred_element_type=F32)
    on = o * lax.rsqrt(jnp.mean(o * o, axis=-1, keepdims=True) + EPS) * g_ref[...]
    o_ref[...] = x_ref[...] + mod_ref[:, 2 * d:3 * d] * on


def _outproj(ym, yr, yl, x, mod, g_post, w_out_b, per_seq_mod):
    bsz0, t0, d = x.shape
    group = 1
    if not per_seq_mod and t0 < OUT_TILE and OUT_TILE % t0 == 0 and bsz0 % (OUT_TILE // t0) == 0:
        group = OUT_TILE // t0
    ym, yr, yl, x = (a.reshape(bsz0 // group, group * t0, a.shape[-1]) for a in (ym, yr, yl, x))
    bsz, t, _ = x.shape
    tm = min(t, OUT_TILE)
    mod_idx = (lambda b, i: (b, 0, 0)) if per_seq_mod else (lambda b, i: (0, 0, 0))
    tok = lambda w: pl.BlockSpec((None, tm, w), lambda b, i: (b, i, 0))
    out = pl.pallas_call(
        _outproj_kernel,
        grid=(bsz, t // tm),
        in_specs=[tok(HW), tok(HW), tok(L_W), tok(d),
                  pl.BlockSpec((None, 1, 3 * d), mod_idx),
                  pl.BlockSpec((1, d), lambda b, i: (0, 0)),
                  pl.BlockSpec((2 * HW + L_W, d), lambda b, i: (0, 0))],
        out_specs=tok(d),
        out_shape=jax.ShapeDtypeStruct((bsz, t, d), F32),
        compiler_params=pltpu.CompilerParams(dimension_semantics=("arbitrary", "arbitrary"),
                                             vmem_limit_bytes=VMEM_LIMIT),
        name="outproj",
    )(ym, yr, yl, x, mod, g_post, w_out_b)
    return out.reshape(bsz0, t0, d)


def _mlstm_kernel(*refs, has_state_in, has_state_out):
    qkv_ref, oz_ref, g_ref, ng_ref = refs[0:4]
    refs = refs[4:]
    if has_state_in:
        c0_ref, n0_ref, m0_ref = refs[0:3]
        refs = refs[3:]
    y_ref = refs[0]
    refs = refs[1:]
    if has_state_out:
        c_out_ref, n_out_ref, m_out_ref = refs[0:3]
        refs = refs[3:]
    hacc, cn_s, m_s = refs
    ns, t = qkv_ref.shape[0], qkv_ref.shape[1]
    nc = t // CHUNK
    lane = _iota2((CHUNK, LANES), 1)
    row = _iota2((CHUNK, LANES), 0)
    lo = lane < HD
    s_idx = lane % HD
    causal = (s_idx <= row, s_idx >= row)
    r2 = _iota2((2 * CHUNK, 2 * CHUNK), 0)
    c2 = _iota2((2 * CHUNK, 2 * CHUNK), 1)
    same = (r2 // CHUNK) == (c2 // CHUNK)
    cum = ((same & (c2 <= r2)).astype(BF16), (same & (c2 >= r2)).astype(BF16))
    cum_rep = tuple(jnp.concatenate([m, m, m], axis=1) for m in cum)
    bd_ones = same.astype(F32)
    ones_rep = jnp.concatenate([same.astype(BF16), same.astype(BF16)], axis=0)
    bd_mask2 = jnp.concatenate([same, same], axis=1)
    colid = _iota2((1, LANES), 1)
    is_f = (colid % 8) >= 4

    hacc[...] = jnp.zeros_like(hacc)
    cn_s[...] = jnp.zeros_like(cn_s)
    m_s[...] = jnp.zeros_like(m_s)
    if has_state_in:
        m_s[...] = m0_ref[...]
        for s in range(ns):
            for d in range(2):
                for p in range(N_PAIRS):
                    for h in range(2):
                        cn_s[s, d, p, h * HD:(h + 1) * HD, h * HD:(h + 1) * HD] = c0_ref[s, d, 2 * p + h]
                    n_rows = jnp.where(same, jnp.broadcast_to(n0_ref[s, d, p], (2 * CHUNK, LANES)), 0.0)
                    cn_s[s, d, p, :, LANES:2 * LANES] = n_rows.T

    def gate_table(s, t0, d):
        gates = g_ref[s, pl.ds(t0, CHUNK), :]
        x = jnp.concatenate([gates, gates], axis=0)
        q = jnp.where(is_f, _cum_dot(cum_rep[d], x), x)
        return q, q.T

    chains = [(s, d, p) for s in range(ns) for d in range(2) for p in range(N_PAIRS)]

    def body(j, carry):
        t0s = (pl.multiple_of(j * CHUNK, CHUNK), pl.multiple_of((nc - 1 - j) * CHUNK, CHUNK))
        tabs = {(s, d): gate_table(s, t0s[d], d) for s in range(ns) for d in range(2)}
        st = []
        for s, d, p in chains:
            t0 = t0s[d]
            q_tab, q_tab_t = tabs[(s, d)]
            ji, jf = _gate_col(d, 0, p), _gate_col(d, 1, p)
            c = dict(t0=t0, sl=slice(p * LANES, (p + 1) * LANES))
            c['bcol'] = _pair_col(q_tab, jf, lo)
            c['icol'] = _pair_col(q_tab, ji, lo)
            brow = _pair_row(q_tab_t, jf, lo[0:1, :])
            irow = _pair_row(q_tab_t, ji, lo[0:1, :])
            c['q'] = qkv_ref[s, pl.ds(t0, CHUNK), p * LANES:(p + 1) * LANES].astype(F32)
            c['k'] = qkv_ref[s, pl.ds(t0, CHUNK), HW + p * LANES:HW + (p + 1) * LANES].astype(F32) * (HD ** -0.5)
            c['v'] = qkv_ref[s, pl.ds(t0, CHUNK), 2 * HW + p * LANES:2 * HW + (p + 1) * LANES].astype(F32)
            c['mprev'] = m_s[s, d, p]
            c['cn'] = cn_s[s, d, p]
            dmat = jnp.where(causal[d], c['bcol'] - brow + irow, -jnp.inf)
            inter = c['bcol'] + c['mprev']
            mx = jnp.where(lo,
                           jnp.max(jnp.where(lo, dmat, -jnp.inf), axis=1, keepdims=True),
                           jnp.max(jnp.where(lo, -jnp.inf, dmat), axis=1, keepdims=True))
            c['m_t'] = jnp.maximum(inter, mx)
            c['pexp'] = jnp.exp(dmat - c['m_t'])
            c['sc'] = jnp.exp(inter - c['m_t'])
            last = CHUNK - 1 if d == 0 else 0
            b_last = c['bcol'][last:last + 1, :]
            g = b_last - c['bcol'] + c['icol']
            c['m_new'] = jnp.maximum(b_last + c['mprev'], jnp.max(g, axis=0, keepdims=True))
            c['dec'] = jnp.exp(b_last + c['mprev'] - c['m_new'])
            c['wk'] = jnp.exp(g - c['m_new'])
            st.append(c)
        for c in st:
            c['s'] = _bdot_nt(c['q'], _bd_stack(c['k'], lo)) * c['pexp']
        for c in st:
            c['a1'] = _bdot(c['q'], c['cn'])
        for c in st:
            c['upd'] = _bdot_tn(c['wk'] * c['k'], jnp.concatenate([c['v'], jnp.ones_like(c['v'])], axis=1))
        for c in st:
            c['a2'] = _bdot(c['s'], jnp.concatenate([_bd_stack(c['v'], lo), bd_ones], axis=1))
        for (s, d, p), c in zip(chains, st):
            num = c['sc'] * c['a1'][:, 0:LANES] + c['a2'][:, 0:LANES]
            den = c['sc'] * c['a1'][:, LANES:] + c['a2'][:, LANES:]
            h = num / jnp.maximum(jnp.abs(den), jnp.exp(-c['m_t']))
            hacc[s, pl.ds(c['t0'], CHUNK), c['sl']] = hacc[s, pl.ds(c['t0'], CHUNK), c['sl']] + h
            cn_s[s, d, p] = (jnp.concatenate([c['dec'], c['dec']], axis=1) * c['cn']
                             + jnp.where(bd_mask2, c['upd'], 0.0))
            m_s[s, d, p] = c['m_new']
        return carry

    lax.fori_loop(0, nc, body, 0)
    if has_state_out:
        m_out_ref[...] = m_s[...]
        lo2 = _iota2((1, LANES), 1) < HD
        for s in range(ns):
            for d in range(2):
                for p in range(N_PAIRS):
                    for h in range(2):
                        c_out_ref[s, d, 2 * p + h] = cn_s[s, d, p, h * HD:(h + 1) * HD, h * HD:(h + 1) * HD]
                    n_t = cn_s[s, d, p, :, LANES:2 * LANES].T
                    n_out_ref[s, d, p] = jnp.where(lo2, n_t[0:1, :], n_t[HD:HD + 1, :])

    ng = ng_ref[...]
    rb = min(t, 256)

    def epilogue(i, carry):
        r0 = pl.multiple_of(i * rb, rb)
        for s in range(ns):
            h = hacc[s, pl.ds(r0, rb), :]
            hn = h * lax.rsqrt(_head_sum(h * h, ones_rep) * (1.0 / HD) + EPS) * ng
            o = oz_ref[s, pl.ds(r0, rb), 0:HW].astype(F32)
            z = oz_ref[s, pl.ds(r0, rb), HW:2 * HW].astype(F32)
            y_ref[s, pl.ds(r0, rb), :] = (hn * _sigmoid(o) * _silu(z)).astype(y_ref.dtype)
        return carry

    lax.fori_loop(0, t // rb, epilogue, 0)


def _seq_spec(ns, t, w, single_buffer):
    if single_buffer:
        return pl.BlockSpec((ns, t, w), lambda b: (b, 0, 0), pipeline_mode=pl.Buffered(1))
    return pl.BlockSpec((ns, t, w), lambda b: (b, 0, 0))


def _full_spec(shape):
    n = len(shape)
    return pl.BlockSpec(shape, lambda b: (0,) * n)


def _state_spec(ns, shape, layer=None):
    if layer is None:
        return pl.BlockSpec((ns,) + shape, lambda b: (b,) + (0,) * len(shape))
    return pl.BlockSpec((ns, None) + shape, lambda b: (b, layer) + (0,) * len(shape))


def _mixer_plan(bsz, in_bytes_per_seq, other_bytes_per_seq):
    for ns, bufs in ((2, 2), (2, 1), (1, 2), (1, 1)):
        if bsz % ns == 0 and ns * (bufs * in_bytes_per_seq + other_bytes_per_seq) <= MIXER_VMEM_BUDGET:
            return ns, bufs == 1
    return 1, True


def _pair_rows(x):
    return jnp.repeat(x.reshape(x.shape[:-1] + (N_PAIRS, 1, 2)), HD, axis=-1)


def _mlstm(qkv, oz, gates, norm_g, state_in, want_state, layer=None):
    bsz, t, _ = qkv.shape
    ns, big = _mixer_plan(bsz, t * (5 * HW * qkv.dtype.itemsize + LANES * 4), t * (HW * 2 * 2 + HW * 4))
    c_shape, row_shape = (2, N_HEADS, HD, HD), (2, N_PAIRS, 1, LANES)
    in_specs = [_seq_spec(ns, t, 3 * HW, big), _seq_spec(ns, t, 2 * HW, big),
                _seq_spec(ns, t, LANES, big), _full_spec((1, HW))]
    args = [qkv, oz, gates, norm_g]
    if state_in is not None:
        c0, n0, m0 = state_in
        in_specs += [_state_spec(ns, c_shape, layer), _state_spec(ns, row_shape, layer),
                     _state_spec(ns, row_shape, layer)]
        args += [c0, n0.reshape(n0.shape[:-3] + row_shape), _pair_rows(m0)]
    out_specs = [_seq_spec(ns, t, HW, False)]
    out_shape = [jax.ShapeDtypeStruct((bsz, t, HW), BF16)]
    if want_state:
        out_specs += [_state_spec(ns, c_shape), _state_spec(ns, row_shape), _state_spec(ns, row_shape)]
        out_shape += [jax.ShapeDtypeStruct((bsz,) + c_shape, F32),
                      jax.ShapeDtypeStruct((bsz,) + row_shape, F32),
                      jax.ShapeDtypeStruct((bsz,) + row_shape, F32)]
    outs = pl.pallas_call(
        functools.partial(_mlstm_kernel, has_state_in=state_in is not None, has_state_out=want_state),
        grid=(bsz // ns,),
        in_specs=in_specs,
        out_specs=out_specs,
        out_shape=out_shape,
        scratch_shapes=[pltpu.VMEM((ns, t, HW), F32),
                        pltpu.VMEM((ns, 2, N_PAIRS, 2 * CHUNK, 2 * LANES), F32),
                        pltpu.VMEM((ns,) + row_shape, F32)],
        compiler_params=pltpu.CompilerParams(dimension_semantics=("arbitrary",),
                                             vmem_limit_bytes=VMEM_LIMIT),
        name="mlstm",
    )(*args)
    if not want_state:
        return outs[0], None
    y, c1, n1, m1 = outs
    return y, (c1, n1.reshape(bsz, 2, N_HEADS, HD), m1[:, :, :, 0, ::HD].reshape(bsz, 2, N_HEADS))


def _rwkv_kernel(*refs, has_state_in, has_state_out):
    (rs_ref, rz_ref, w0_ref, a0_ref, w2_ref, a2_ref, kk_ref, ka_ref, rk_ref, ng_ref) = refs[0:10]
    refs = refs[10:]
    if has_state_in:
        s0_ref = refs[0]
        refs = refs[1:]
    y_ref = refs[0]
    refs = refs[1:]
    if has_state_out:
        s_out_ref = refs[0]
        refs = refs[1:]
    yacc, bonus, s_s = refs
    ns, t = rs_ref.shape[0], rs_ref.shape[1]
    nc = t // CHUNK
    lane = _iota2((CHUNK, LANES), 1)
    row = _iota2((CHUNK, LANES), 0)
    lo = lane < HD
    s_idx = lane % HD
    strict = (s_idx < row, s_idx > row)
    incl = (s_idx <= row, s_idx >= row)
    eye_pk = (s_idx == row).astype(F32)
    r64 = _iota2((CHUNK, CHUNK), 0)
    c64 = _iota2((CHUNK, CHUNK), 1)
    cum = ((c64 <= r64).astype(BF16), (c64 >= r64).astype(BF16))
    cum_rep = tuple(jnp.concatenate([m, m, m], axis=1) for m in cum)
    r2 = _iota2((2 * CHUNK, 2 * CHUNK), 0)
    c2 = _iota2((2 * CHUNK, 2 * CHUNK), 1)
    bd_mask = (r2 // CHUNK) == (c2 // CHUNK)
    ones_rep = jnp.concatenate([bd_mask.astype(BF16), bd_mask.astype(BF16)], axis=0)
    kk = kk_ref[...]
    ka = ka_ref[...]
    rk = rk_ref[...]

    yacc[...] = jnp.zeros_like(yacc)
    s_s[...] = jnp.zeros_like(s_s)
    if has_state_in:
        for s in range(ns):
            for d in range(2):
                for p in range(N_PAIRS):
                    for h in range(2):
                        s_s[s, d, p, h * HD:(h + 1) * HD, h * HD:(h + 1) * HD] = s0_ref[s, d, 2 * p + h]

    def prep(s, c, d):
        t0 = pl.multiple_of(c * CHUNK, CHUNK)
        blk = rs_ref[s, pl.ds(t0, CHUNK), :].astype(F32)
        r = blk[:, 0:HW]
        k = blk[:, HW:2 * HW]
        v = blk[:, 2 * HW:3 * HW]
        wl = blk[:, 3 * HW:3 * HW + 2 * LORA]
        al = blk[:, 3 * HW + 2 * LORA:3 * HW + 4 * LORA]
        dsl = slice(d * HW, (d + 1) * HW)
        logw = -RWKV_DECAY_SCALE * _sigmoid(w0_ref[:, dsl] + _bdot(jnp.tanh(wl), w2_ref[:, dsl]))
        a = _sigmoid(a0_ref[:, dsl] + _bdot(al, a2_ref[:, dsl]))
        kappa = k * kk
        khat = kappa * lax.rsqrt(jnp.maximum(_head_sum(kappa * kappa, ones_rep), 1e-24))
        kt = k * (1.0 + (a - 1.0) * ka)
        b = khat * a
        if d == 0:
            bonus[s, pl.ds(t0, CHUNK), :] = _head_sum(r * k * rk, ones_rep) * v
        last = CHUNK - 1 if d == 0 else 0
        logp = _cum_dot(cum_rep[d], logw)
        logp_last = logp[last:last + 1, :]
        e_neg = jnp.exp(-logp)
        e_last = jnp.exp(logp_last - logp)
        return dict(t0=t0, v=v, alpha=jnp.exp(logp - logw) * khat, beta=b * e_neg, kap=kt * e_neg,
                    rho=r * jnp.exp(logp), beta_l=b * e_last, kap_l=kt * e_last,
                    p_last=jnp.exp(logp_last))

    chains = [(s, d, p) for s in range(ns) for d in range(2) for p in range(N_PAIRS)]

    def body(j, carry):
        pre = {(s, d): prep(s, j if d == 0 else nc - 1 - j, d) for s in range(ns) for d in range(2)}
        st = []
        for s, d, p in chains:
            sl = slice(p * LANES, (p + 1) * LANES)
            c = {key: (val if key == 't0' else val[:, sl]) for key, val in pre[(s, d)].items()}
            c['sl'] = sl
            st.append(c)
        for c in st:
            c['g'] = _bdot_nt(jnp.concatenate([c['alpha'], c['rho']], axis=0),
                              jnp.concatenate([_bd_stack(c['beta'], lo), _bd_stack(c['kap'], lo)], axis=0))
        for (s, d, p), c in zip(chains, st):
            g = c.pop('g')
            c['n'] = jnp.where(strict[d], g[0:CHUNK, 0:LANES], 0.0)
            c['mk'] = jnp.where(strict[d], g[0:CHUNK, LANES:], 0.0)
            c['mrb'] = jnp.where(incl[d], g[CHUNK:, 0:LANES], 0.0)
            c['mrk'] = jnp.where(incl[d], g[CHUNK:, LANES:], 0.0)
            c['inv'] = eye_pk - c['n']
        for c in st:
            c['pw'] = _bdot(c['n'], _bd_stack(c['n'], lo))
        for c in st:
            c['mkv'] = _bdot(c['mk'], _bd_stack(c['v'], lo))
        for _ in range(4):
            for c in st:
                rp = _bdot(jnp.concatenate([c['inv'], c['pw']], axis=0), _bd_stack(c['pw'], lo))
                c['inv'] = c['inv'] + rp[0:CHUNK]
                c['pw'] = rp[CHUNK:]
        for c in st:
            c['inv'] = c['inv'] + _bdot(c['inv'], _bd_stack(c['pw'], lo))
        for c in st:
            c['wu'] = _bdot(c['inv'], jnp.concatenate([_bd_stack(c['alpha'], lo), _bd_stack(c['mkv'], lo)], axis=1))
        for (s, d, p), c in zip(chains, st):
            c['s_prev'] = s_s[s, d, p]
            ws = _bdot_nt(jnp.concatenate([c['wu'][:, 0:LANES], c['rho']], axis=0), c['s_prev'])
            c['u'] = ws[0:CHUNK] + c['wu'][:, LANES:]
            c['ys'] = ws[CHUNK:]
        for c in st:
            c['upd'] = _bdot_tn(jnp.concatenate([c['v'], -c['u']], axis=0),
                                jnp.concatenate([c['kap_l'], c['beta_l']], axis=0))
        for c in st:
            c['y'] = c['ys'] + _bdot(jnp.concatenate([c['mrk'], -c['mrb']], axis=1),
                                     jnp.concatenate([_bd_stack(c['v'], lo), _bd_stack(c['u'], lo)], axis=0))
        for (s, d, p), c in zip(chains, st):
            s_s[s, d, p] = c['s_prev'] * c['p_last'] + jnp.where(bd_mask, c['upd'], 0.0)
            yacc[s, pl.ds(c['t0'], CHUNK), c['sl']] = yacc[s, pl.ds(c['t0'], CHUNK), c['sl']] + c['y']
        return carry

    lax.fori_loop(0, nc, body, 0)
    if has_state_out:
        for s in range(ns):
            for d in range(2):
                for p in range(N_PAIRS):
                    for h in range(2):
                        s_out_ref[s, d, 2 * p + h] = s_s[s, d, p, h * HD:(h + 1) * HD, h * HD:(h + 1) * HD]

    ng = ng_ref[...]
    rb = min(t, 256)

    def epilogue(i, carry):
        r0 = pl.multiple_of(i * rb, rb)
        for s in range(ns):
            y = yacc[s, pl.ds(r0, rb), :]
            yn = (y * lax.rsqrt(_head_sum(y * y, ones_rep) * (1.0 / HD) + EPS) * ng
                  + bonus[s, pl.ds(r0, rb), :])
            y_ref[s, pl.ds(r0, rb), :] = (yn * _silu(rz_ref[s, pl.ds(r0, rb), :].astype(F32))).astype(y_ref.dtype)
        return carry

    lax.fori_loop(0, t // rb, epilogue, 0)


def _rwkv(rs, rz, w0, a0, w2bd, a2bd, kk, ka, rk, norm_g, state_in, want_state, layer=None):
    bsz, t, _ = rs.shape
    ns, big = _mixer_plan(bsz, t * (R_SHIFT + HW) * rs.dtype.itemsize, t * (HW * 2 * 2 + 2 * HW * 4))
    s_shape = (2, N_HEADS, HD, HD)
    in_specs = [_seq_spec(ns, t, R_SHIFT, big), _seq_spec(ns, t, HW, big),
                _full_spec((1, 2 * HW)), _full_spec((1, 2 * HW)),
                _full_spec((2 * LORA, 2 * HW)), _full_spec((2 * LORA, 2 * HW)),
                _full_spec((1, HW)), _full_spec((1, HW)), _full_spec((1, HW)), _full_spec((1, HW))]
    args = [rs, rz, w0, a0, w2bd, a2bd, kk, ka, rk, norm_g]
    if state_in is not None:
        in_specs.append(_state_spec(ns, s_shape, layer))
        args.append(state_in)
    out_specs = [_seq_spec(ns, t, HW, False)]
    out_shape = [jax.ShapeDtypeStruct((bsz, t, HW), BF16)]
    if want_state:
        out_specs.append(_state_spec(ns, s_shape))
        out_shape.append(jax.ShapeDtypeStruct((bsz,) + s_shape, F32))
    outs = pl.pallas_call(
        functools.partial(_rwkv_kernel, has_state_in=state_in is not None, has_state_out=want_state),
        grid=(bsz // ns,),
        in_specs=in_specs,
        out_specs=out_specs,
        out_shape=out_shape,
        scratch_shapes=[pltpu.VMEM((ns, t, HW), F32), pltpu.VMEM((ns, t, HW), F32),
                        pltpu.VMEM((ns, 2, N_PAIRS, 2 * CHUNK, LANES), F32)],
        compiler_params=pltpu.CompilerParams(dimension_semantics=("arbitrary",),
                                             vmem_limit_bytes=VMEM_LIMIT),
        name="rwkv",
    )(*args)
    return (outs[0], outs[1]) if want_state else (outs[0], None)


def _lru_kernel(xz_ref, conv_ref, cb_ref, w_ref, b_ref, lam_ref, h0_ref, y_ref, hfin_ref, a_s, b_s, acc):
    t = xz_ref.shape[0]
    nb = t // LRU_BLOCK
    n_groups = LRU_BLOCK // SUBLANES
    row = _iota2((LRU_BLOCK, L_W), 0)
    sub = _iota2((n_groups, SUBLANES, L_W), 1)
    conv = conv_ref[...]
    cb = cb_ref[...]
    neg_c_sp = -LRU_C * _softplus(-lam_ref[...])

    def coefficients(j, carry):
        t0 = pl.multiple_of(j * LRU_BLOCK, LRU_BLOCK)
        x = xz_ref[pl.ds(t0, LRU_BLOCK), 0:L_W].astype(F32)
        prev_rows = xz_ref[pl.ds(pl.multiple_of(jnp.maximum(t0 - HALO, 0), HALO), HALO), 0:L_W]
        next_rows = xz_ref[pl.ds(pl.multiple_of(jnp.minimum(t0 + LRU_BLOCK, t - HALO), HALO), HALO), 0:L_W]
        prev_rows = jnp.where(j > 0, prev_rows.astype(F32), 0.0)
        next_rows = jnp.where(j < nb - 1, next_rows.astype(F32), 0.0)
        p1, p2 = prev_rows[HALO - 1:HALO, :], prev_rows[HALO - 2:HALO - 1, :]
        xm1 = jnp.where(row == 0, p1, pltpu.roll(x, 1, 0))
        xm2 = jnp.where(row == 0, p2, jnp.where(row == 1, p1, pltpu.roll(x, 2, 0)))
        xp1 = jnp.where(row == LRU_BLOCK - 1, next_rows[0:1, :], pltpu.roll(x, LRU_BLOCK - 1, 0))
        xc = conv[0:1, :] * xm2 + conv[1:2, :] * xm1 + conv[2:3, :] * x + conv[3:4, :] * xp1 + cb
        pre = _bdot(xc, w_ref[...]) + b_ref[...]
        for d in range(2):
            rg = _sigmoid(pre[:, 2 * d * L_W:(2 * d + 1) * L_W])
            ig = _sigmoid(pre[:, (2 * d + 1) * L_W:(2 * d + 2) * L_W])
            log_a = rg * neg_c_sp[:, d * L_W:(d + 1) * L_W]
            a_step = jnp.exp(log_a)
            b_step = jnp.sqrt(jnp.tanh(-log_a) * (1.0 + a_step * a_step)) * ig * xc
            a_cum = a_step.reshape(n_groups, SUBLANES, L_W)
            b_cum = b_step.reshape(n_groups, SUBLANES, L_W)
            k = 1
            while k < SUBLANES:
                keep = (sub >= k) if d == 0 else (sub < SUBLANES - k)
                shift = k if d == 0 else SUBLANES - k
                a_sh = jnp.where(keep, pltpu.roll(a_cum, shift, 1), 1.0)
                b_sh = jnp.where(keep, pltpu.roll(b_cum, shift, 1), 0.0)
                b_cum = a_cum * b_sh + b_cum
                a_cum = a_cum * a_sh
                k *= 2
            a_s[d, pl.ds(t0, LRU_BLOCK), :] = a_cum.reshape(LRU_BLOCK, L_W)
            b_s[d, pl.ds(t0, LRU_BLOCK), :] = b_cum.reshape(LRU_BLOCK, L_W)
        return carry

    lax.fori_loop(0, nb, coefficients, 0)

    def scan_block(d, j, carry):
        t0 = pl.multiple_of(j * LRU_BLOCK, LRU_BLOCK)
        a_cum = a_s[d, pl.ds(t0, LRU_BLOCK), :]
        b_cum = b_s[d, pl.ds(t0, LRU_BLOCK), :]
        pieces = [None] * n_groups
        for gi in (range(n_groups) if d == 0 else reversed(range(n_groups))):
            rows = slice(gi * SUBLANES, (gi + 1) * SUBLANES)
            hg = b_cum[rows, :] + a_cum[rows, :] * carry
            carry = hg[SUBLANES - 1:SUBLANES, :] if d == 0 else hg[0:1, :]
            pieces[gi] = hg
        return t0, jnp.concatenate(pieces, axis=0), carry

    def forward(j, carry):
        t0, h, carry = scan_block(0, j, carry)
        acc[pl.ds(t0, LRU_BLOCK), :] = h
        return carry

    def backward(jj, carry):
        t0, h, carry = scan_block(1, nb - 1 - jj, carry)
        z = xz_ref[pl.ds(t0, LRU_BLOCK), L_W:2 * L_W].astype(F32)
        y_ref[pl.ds(t0, LRU_BLOCK), :] = ((acc[pl.ds(t0, LRU_BLOCK), :] + h) * _silu(z)).astype(y_ref.dtype)
        return carry

    hfin_ref[0] = lax.fori_loop(0, nb, forward, h0_ref[0])
    hfin_ref[1] = lax.fori_loop(0, nb, backward, h0_ref[1])


def _lru(xz, conv, conv_b, wbd, bias, lam, h0):
    bsz, t, _ = xz.shape
    st = lambda shape: pl.BlockSpec((None,) + shape, lambda b: (b,) + (0,) * len(shape))
    return pl.pallas_call(
        _lru_kernel,
        grid=(bsz,),
        in_specs=[_seq_spec(None, t, 2 * L_W, False), _full_spec((CONV_W, L_W)), _full_spec((1, L_W)),
                  _full_spec((L_W, 4 * L_W)), _full_spec((1, 4 * L_W)), _full_spec((1, 2 * L_W)),
                  st((2, 1, L_W))],
        out_specs=[pl.BlockSpec((None, t, L_W), lambda b: (b, 0, 0)), st((2, 1, L_W))],
        out_shape=[jax.ShapeDtypeStruct((bsz, t, L_W), BF16),
                   jax.ShapeDtypeStruct((bsz, 2, 1, L_W), F32)],
        scratch_shapes=[pltpu.VMEM((2, t, L_W), F32), pltpu.VMEM((2, t, L_W), F32),
                        pltpu.VMEM((t, L_W), F32)],
        compiler_params=pltpu.CompilerParams(dimension_semantics=("arbitrary",),
                                             vmem_limit_bytes=VMEM_LIMIT),
        name="lru",
    )(xz, conv, conv_b, wbd, bias, lam, h0)


def _permute_w_in(w_in_l):
    valid = jnp.asarray(_GATE_SRC >= 0)
    gates = jnp.where(valid, w_in_l[:, 5 * HW:M_COLS][:, np.maximum(_GATE_SRC, 0)], 0.0)
    return jnp.concatenate([w_in_l[:, 0:5 * HW], gates, w_in_l[:, M_COLS:]], axis=1).astype(BF16)


def _gate_source_index():
    src = np.full((LANES,), -1, np.int32)
    for dd in range(2):
        for is_f in range(2):
            for p in range(N_PAIRS):
                for parity in range(2):
                    src[parity * HD + _gate_col(dd, is_f, p)] = (is_f * 2 * N_HEADS + dd * N_HEADS
                                                                 + 2 * p + parity)
    return src


_GATE_SRC = _gate_source_index()


def _gate_bias(m_bi_l, m_bf_l):
    flat = jnp.concatenate([m_bi_l.reshape(-1), m_bf_l.reshape(-1)])
    return jnp.where(jnp.asarray(_GATE_SRC >= 0), flat[np.maximum(_GATE_SRC, 0)], 0.0).reshape(1, LANES)


def _block_diag(blocks):
    n, a, b = blocks.shape
    eye = jnp.eye(n, dtype=blocks.dtype)
    return (eye[:, None, :, None] * blocks[:, :, None, :]).reshape(n * a, n * b)


def _layer_params(l, g_pre, g_post, w_in, w_out, m_bi, m_bf, m_norm, r_mu, r_w0, r_w2, r_a0, r_a2,
                  r_kk, r_ka, r_rk, r_norm, l_conv, l_conv_b, l_wa, l_ba, l_wx, l_bx, l_lambda):
    row = lambda v: v.reshape(1, -1)
    return dict(
        g_pre=row(g_pre[l]), g_post=row(g_post[l]),
        w_in=_permute_w_in(w_in[l]), w_out=w_out[l].astype(BF16),
        gate_bias=_gate_bias(m_bi[l], m_bf[l]), m_norm=row(m_norm[l]),
        r_mu=row(r_mu[l]), r_w0=row(r_w0[l]), r_a0=row(r_a0[l]),
        r_w2=_block_diag(r_w2[l]).astype(BF16), r_a2=_block_diag(r_a2[l]).astype(BF16),
        r_kk=row(r_kk[l]), r_ka=row(r_ka[l]), r_rk=row(r_rk[l]), r_norm=row(r_norm[l]),
        l_conv=l_conv[l], l_conv_b=row(l_conv_b[l]),
        l_w=jnp.concatenate([_block_diag(l_wa[l][0]), _block_diag(l_wx[l][0]),
                             _block_diag(l_wa[l][1]), _block_diag(l_wx[l][1])], axis=1).astype(BF16),
        l_b=jnp.concatenate([l_ba[l][0], l_bx[l][0], l_ba[l][1], l_bx[l][1]]).reshape(1, -1),
        l_lambda=row(l_lambda[l]),
    )


def _trunk_layer(x, mod, lp, states, want_state, per_seq_mod, grid_shift, layer=None):
    bsz = x.shape[0]
    qkv, oz, gates, rs, rz, lxz = _inproj(x, mod, lp['g_pre'], lp['w_in'], lp['r_mu'], lp['gate_bias'],
                                          per_seq_mod, grid_shift)
    m_in = None if states is None else states[0:3]
    r_in = None if states is None else states[3]
    if states is None:
        l_in = jnp.zeros((bsz, 2, 1, L_W), F32)
    else:
        l_in = (states[4] if layer is None else states[4][:, layer])[:, :, None, :]
    y_m, m_out = _mlstm(qkv, oz, gates, lp['m_norm'], m_in, want_state, layer)
    y_r, r_out = _rwkv(rs, rz, lp['r_w0'], lp['r_a0'], lp['r_w2'], lp['r_a2'], lp['r_kk'],
                       lp['r_ka'], lp['r_rk'], lp['r_norm'], r_in, want_state, layer)
    y_l, l_out = _lru(lxz, lp['l_conv'], lp['l_conv_b'], lp['l_w'], lp['l_b'], lp['l_lambda'], l_in)
    y = _outproj(y_m, y_r, y_l, x, mod, lp['g_post'], lp['w_out'], per_seq_mod)
    new_states = m_out + (r_out, l_out[:, :, 0, :]) if want_state else None
    return y, new_states


def kernel(x_prompt, x_sample, c, state_mlstm_C, state_mlstm_n, state_mlstm_m, state_rwkv, state_rglru, c_ctx, g_pre, g_post, w_mod, b_mod, w_in, w_out, m_bi, m_bf, m_norm, r_mu, r_w0, r_w2, r_a0, r_a2, r_kk, r_ka, r_rk, r_norm, l_conv, l_conv_b, l_wa, l_ba, l_wx, l_bx, l_lambda):
    depth = w_in.shape[0]
    bs = x_sample.shape[0]
    d = x_prompt.shape[-1]
    rows = -(-(1 + bs) // SUBLANES) * SUBLANES
    cc = jnp.zeros((rows, d), F32).at[0].set(c_ctx).at[1:1 + bs].set(c)
    mod = _modulation(cc, w_mod, b_mod)
    xp, xs = x_prompt, x_sample
    cache = tuple(a.astype(F32) for a in
                  (state_mlstm_C, state_mlstm_n, state_mlstm_m, state_rwkv, state_rglru))
    new_states = []
    for l in range(depth):
        lp = _layer_params(l, g_pre, g_post, w_in, w_out, m_bi, m_bf, m_norm, r_mu, r_w0, r_w2, r_a0,
                           r_a2, r_kk, r_ka, r_rk, r_norm, l_conv, l_conv_b, l_wa, l_ba, l_wx, l_bx,
                           l_lambda)
        xp, st = _trunk_layer(xp, mod[l, 0:1][:, None, :], lp, None, True, False, False)
        new_states.append(st)
        xs, _ = _trunk_layer(xs, mod[l, 1:1 + bs][:, None, :], lp, cache, False, True, True, layer=l)
    stacked = tuple(jnp.stack([st[i] for st in new_states], axis=1) for i in range(5))
    return (xp, xs) + stacked
```

```python
import functools

import numpy as np
import jax
import jax.numpy as jnp
from jax import lax
from jax.experimental import pallas as pl
from jax.experimental.pallas import tpu as pltpu

F32 = jnp.float32
BF16 = jnp.bfloat16
HIGHEST = lax.Precision.HIGHEST

D_MODEL = 1024
EPS = 1e-6
HD = 64
N_HEADS = 6
N_PAIRS = N_HEADS // 2
HW = N_HEADS * HD
CHUNK = 64
GRID_W = 64
LORA = 64
R_SHIFT = 3 * HW + 4 * LORA
L_W = 256
L_BLOCKS = 4
CONV_W = 4
LRU_C = 8.0
RWKV_DECAY_SCALE = 0.6065306597126334
M_COLS = 5 * HW + 4 * N_HEADS
IN_COLS = M_COLS + R_SHIFT + HW + 2 * L_W

LANES = 128
SUBLANES = 8
HALO = 16
HALO_ROWS = GRID_W
LRU_BLOCK = 128
RWKV_MAX_SEQS = 4
MIB = 1024 * 1024
V7X_VMEM_BYTES = 64 * MIB
VMEM_LIMIT = V7X_VMEM_BYTES - 4 * MIB
MIXER_SPILL_BYTES = 10 * MIB
MIXER_VMEM_BUDGET = VMEM_LIMIT - MIXER_SPILL_BYTES

SEG_WIDTHS = (3 * HW, 2 * HW, LANES, R_SHIFT, HW, 2 * L_W)
SEG_DTYPES = (BF16, BF16, F32, BF16, BF16, BF16)
SEG_OFFS = tuple(int(v) for v in np.cumsum((0,) + SEG_WIDTHS))
IN_COLS_PAD = SEG_OFFS[-1]
N_TILE = 512
IN_TILE = 512
OUT_TILE = 1024


def _gate_col(d, is_f, p):
    return 8 * d + 4 * is_f + p


def _bdot(a, b):
    return jnp.dot(a.astype(BF16), b.astype(BF16), preferred_element_type=F32)


def _bdot_nt(a, b):
    return lax.dot_general(a.astype(BF16), b.astype(BF16), (((1,), (1,)), ((), ())),
                           preferred_element_type=F32)


def _bdot_tn(a, b):
    return lax.dot_general(a.astype(BF16), b.astype(BF16), (((0,), (0,)), ((), ())),
                           preferred_element_type=F32)


def _fdot(a, b):
    return jnp.dot(a, b, precision=HIGHEST, preferred_element_type=F32)


def _split_bf16(x, terms):
    out = []
    for _ in range(terms - 1):
        hi = x.astype(BF16)
        out.append(hi)
        x = x - hi.astype(F32)
    out.append(x.astype(BF16))
    return out


def _cum_dot(cum_rep, x, terms=3):
    return jnp.dot(cum_rep, jnp.concatenate(_split_bf16(x, terms), axis=0), preferred_element_type=F32)


def _head_sum(x, ones_rep, terms=2):
    outs = []
    for p in range(x.shape[1] // LANES):
        parts = _split_bf16(x[:, p * LANES:(p + 1) * LANES], terms)
        outs.append(jnp.dot(jnp.concatenate(parts, axis=1), ones_rep, preferred_element_type=F32))
    return outs[0] if len(outs) == 1 else jnp.concatenate(outs, axis=1)


def _sigmoid(x):
    return 0.5 * jnp.tanh(0.5 * x) + 0.5


def _silu(x):
    return x * _sigmoid(x)


def _softplus(x):
    return jnp.maximum(x, 0.0) + jnp.log1p(jnp.exp(-jnp.abs(x)))


def _log_sigmoid(x):
    return -_softplus(-x)


def _iota2(shape, dim):
    return lax.broadcasted_iota(jnp.int32, shape, dim)


def _bd_stack(x, lo):
    return jnp.concatenate([jnp.where(lo, x, 0.0), jnp.where(lo, 0.0, x)], axis=0)


def _pair_col(q, j, lo):
    return jnp.where(lo, q[0:CHUNK, j:j + 1], q[0:CHUNK, HD + j:HD + j + 1])


def _pair_row(qt, j, lo_row):
    return jnp.where(lo_row, qt[j:j + 1, :], qt[HD + j:HD + j + 1, :])


def _mod_kernel(c_ref, w_ref, b_ref, o_ref):
    o_ref[...] = _fdot(_silu(c_ref[...]), w_ref[...]) + b_ref[...]


def _modulation(cc, w_mod, b_mod):
    depth, d, n = w_mod.shape
    rows = cc.shape[0]
    return pl.pallas_call(
        _mod_kernel,
        grid=(depth, n // N_TILE),
        in_specs=[pl.BlockSpec((rows, d), lambda l, j: (0, 0)),
                  pl.BlockSpec((None, d, N_TILE), lambda l, j: (l, 0, j)),
                  pl.BlockSpec((None, 1, N_TILE), lambda l, j: (l, 0, j))],
        out_specs=pl.BlockSpec((None, rows, N_TILE), lambda l, j: (l, 0, j)),
        out_shape=jax.ShapeDtypeStruct((depth, rows, n), F32),
        compiler_params=pltpu.CompilerParams(dimension_semantics=("arbitrary", "arbitrary")),
        name="modulation",
    )(cc, w_mod, b_mod.reshape(depth, 1, n))


SEG_QKV, SEG_OZ, SEG_GATES, SEG_RS, SEG_RZ, SEG_LXZ = range(6)


def _inproj_kernel(x_ref, xprev_ref, xnext_ref, mod_ref, g_ref, w_ref, mu_ref, gb_ref, *refs,
                   grid_shift, seq_len):
    tm, d = x_ref.shape
    single_tile = tm >= seq_len
    span = seq_len if single_tile else tm
    i = pl.program_id(1)
    n_tiles = pl.num_programs(1)
    mod = mod_ref[...]
    g = g_ref[...]

    def norm_mod(x):
        h = x * lax.rsqrt(jnp.mean(x * x, axis=-1, keepdims=True) + EPS) * g
        return (h * (1.0 + mod[:, d:2 * d]) + mod[:, 0:d]).astype(BF16)

    out_refs, h_s = refs[:-1], refs[-1]
    off = 0 if single_tile else HALO_ROWS
    h_s[off:off + tm, :] = norm_mod(x_ref[...])
    if not single_tile:
        h_s[0:off, :] = norm_mod(xprev_ref[...])
        h_s[off + tm:, :] = norm_mod(xnext_ref[...])
    has_prev = i > 0
    has_next = i < n_tiles - 1
    row = _iota2((tm, 1), 0) % span
    a, b = SEG_OFFS[SEG_RS], SEG_OFFS[SEG_RS + 1]
    o_ref = out_refs[SEG_RS]
    for n0 in range(a, b, N_TILE):
        n1 = min(n0 + N_TILE, b)
        u_all = jnp.dot(h_s[...], w_ref[:, n0:n1], preferred_element_type=F32)
        if single_tile:
            u = u_all
            u_prev = u_next = jnp.zeros((HALO_ROWS, n1 - n0), F32)
        else:
            u = u_all[HALO_ROWS:HALO_ROWS + tm]
            u_prev = jnp.where(has_prev, u_all[0:HALO_ROWS], 0.0)
            u_next = jnp.where(has_next, u_all[HALO_ROWS + tm:], 0.0)
        left = pltpu.roll(u, 1, 0)
        right = pltpu.roll(u, tm - 1, 0)
        if grid_shift:
            left = jnp.where(row % GRID_W == 0, 0.0, left)
            right = jnp.where(row % GRID_W == GRID_W - 1, 0.0, right)
            up = jnp.concatenate([u_prev, u[0:tm - GRID_W]], axis=0)
            down = jnp.concatenate([u[GRID_W:tm], u_next], axis=0)
            if tm > span:
                up = jnp.where(row < GRID_W, 0.0, up)
                down = jnp.where(row >= span - GRID_W, 0.0, down)
            sh = 0.25 * (up + down + left + right)
        else:
            left = jnp.where(row == 0, u_prev[HALO_ROWS - 1:HALO_ROWS, :], left)
            right = jnp.where(row == span - 1, u_next[0:1, :], right)
            sh = 0.5 * (left + right)
        o_ref[:, n0 - a:n1 - a] = (u + mu_ref[:, n0 - a:n1 - a] * (sh - u)).astype(o_ref.dtype)

    is_f = (_iota2((1, LANES), 1) % 8) >= 4
    a, b = SEG_OFFS[SEG_GATES], SEG_OFFS[SEG_GATES + 1]
    pre = jnp.dot(h_s[off:off + tm, :], w_ref[:, a:b], preferred_element_type=F32) + gb_ref[...]
    out_refs[SEG_GATES][...] = jnp.where(is_f, _log_sigmoid(pre), pre)

    for seg in (SEG_QKV, SEG_OZ, SEG_RZ, SEG_LXZ):
        o_ref, a, b = out_refs[seg], SEG_OFFS[seg], SEG_OFFS[seg + 1]
        for n0 in range(a, b, N_TILE):
            n1 = min(n0 + N_TILE, b)
            o_ref[:, n0 - a:n1 - a] = jnp.dot(h_s[off:off + tm, :], w_ref[:, n0:n1],
                                              preferred_element_type=F32).astype(o_ref.dtype)


def _inproj(x, mod, g_pre, w_in_p, r_mu, gate_bias, per_seq_mod, grid_shift):
    bsz0, t0, d = x.shape
    group = 1
    if not per_seq_mod and t0 < IN_TILE and IN_TILE % t0 == 0 and bsz0 % (IN_TILE // t0) == 0:
        group = IN_TILE // t0
    x = x.reshape(bsz0 // group, group * t0, d)
    bsz, t, _ = x.shape
    tm = min(t, IN_TILE)
    hb = tm // HALO_ROWS
    n_halo = t // HALO_ROWS
    mod_idx = (lambda b, i: (b, 0, 0)) if per_seq_mod else (lambda b, i: (0, 0, 0))
    outs = pl.pallas_call(
        functools.partial(_inproj_kernel, grid_shift=grid_shift, seq_len=t0),
        grid=(bsz, t // tm),
        in_specs=[pl.BlockSpec((None, tm, d), lambda b, i: (b, i, 0)),
                  pl.BlockSpec((None, HALO_ROWS, d), lambda b, i: (b, jnp.maximum(i * hb - 1, 0), 0)),
                  pl.BlockSpec((None, HALO_ROWS, d),
                               lambda b, i: (b, jnp.minimum((i + 1) * hb, n_halo - 1), 0)),
                  pl.BlockSpec((None, 1, 3 * d), mod_idx),
                  pl.BlockSpec((1, d), lambda b, i: (0, 0)),
                  pl.BlockSpec((d, IN_COLS_PAD), lambda b, i: (0, 0), pipeline_mode=pl.Buffered(1)),
                  pl.BlockSpec((1, R_SHIFT), lambda b, i: (0, 0)),
                  pl.BlockSpec((1, LANES), lambda b, i: (0, 0))],
        out_specs=[pl.BlockSpec((None, tm, w), lambda b, i: (b, i, 0)) for w in SEG_WIDTHS],
        out_shape=[jax.ShapeDtypeStruct((bsz, t, w), dt) for w, dt in zip(SEG_WIDTHS, SEG_DTYPES)],
        scratch_shapes=[pltpu.VMEM((tm if t == tm else tm + 2 * HALO_ROWS, d), BF16)],
        compiler_params=pltpu.CompilerParams(dimension_semantics=("arbitrary", "arbitrary"),
                                             vmem_limit_bytes=VMEM_LIMIT),
        name="inproj",
    )(x, x, x, mod, g_pre, w_in_p, r_mu, gate_bias)
    return [o.reshape(bsz0, t0, o.shape[-1]) for o in outs]


def _outproj_kernel(ym_ref, yr_ref, yl_ref, x_ref, mod_ref, g_ref, w_ref, o_ref):
    d = x_ref.shape[-1]
    o = jnp.dot(ym_ref[...], w_ref[0:HW, :], preferred_element_type=F32)
    o = o + jnp.dot(yr_ref[...], w_ref[HW:2 * HW, :], preferred_element_type=F32)
    o = o + jnp.dot(yl_ref[...], w_ref[2 * HW:2 * HW + L_W, :], preferred_element_type=F32)
    on = o * lax.rsqrt(jnp.mean(o * o, axis=-1, keepdims=True) + EPS) * g_ref[...]
    o_ref[...] = x_ref[...] + mod_ref[:, 2 * d:3 * d] * on


def _outproj(ym, yr, yl, x, mod, g_post, w_out_b, per_seq_mod):
    bsz0, t0, d = x.shape
    group = 1
    if not per_seq_mod and t0 < OUT_TILE and OUT_TILE % t0 == 0 and bsz0 % (OUT_TILE // t0) == 0:
        group = OUT_TILE // t0
    ym, yr, yl, x = (a.reshape(bsz0 // group, group * t0, a.shape[-1]) for a in (ym, yr, yl, x))
    bsz, t, _ = x.shape
    tm = min(t, OUT_TILE)
    mod_idx = (lambda b, i: (b, 0, 0)) if per_seq_mod else (lambda b, i: (0, 0, 0))
    tok = lambda w: pl.BlockSpec((None, tm, w), lambda b, i: (b, i, 0))
    out = pl.pallas_call(
        _outproj_kernel,
        grid=(bsz, t // tm),
        in_specs=[tok(HW), tok(HW), tok(L_W), tok(d),
                  pl.BlockSpec((None, 1, 3 * d), mod_idx),
                  pl.BlockSpec((1, d), lambda b, i: (0, 0)),
                  pl.BlockSpec((2 * HW + L_W, d), lambda b, i: (0, 0))],
        out_specs=tok(d),
        out_shape=jax.ShapeDtypeStruct((bsz, t, d), F32),
        compiler_params=pltpu.CompilerParams(dimension_semantics=("arbitrary", "arbitrary"),
                                             vmem_limit_bytes=VMEM_LIMIT),
        name="outproj",
    )(ym, yr, yl, x, mod, g_post, w_out_b)
    return out.reshape(bsz0, t0, d)


def _mlstm_kernel(*refs, has_state_in, has_state_out):
    qkv_ref, oz_ref, g_ref, ng_ref = refs[0:4]
    refs = refs[4:]
    if has_state_in:
        c0_ref, n0_ref, m0_ref = refs[0:3]
        refs = refs[3:]
    y_ref = refs[0]
    refs = refs[1:]
    if has_state_out:
        c_out_ref, n_out_ref, m_out_ref = refs[0:3]
        refs = refs[3:]
    hacc, cn_s, m_s = refs
    ns, t = qkv_ref.shape[0], qkv_ref.shape[1]
    nc = t // CHUNK
    lane = _iota2((CHUNK, LANES), 1)
    row = _iota2((CHUNK, LANES), 0)
    lo = lane < HD
    s_idx = lane % HD
    causal = (s_idx <= row, s_idx >= row)
    r2 = _iota2((2 * CHUNK, 2 * CHUNK), 0)
    c2 = _iota2((2 * CHUNK, 2 * CHUNK), 1)
    same = (r2 // CHUNK) == (c2 // CHUNK)
    cum = ((same & (c2 <= r2)).astype(BF16), (same & (c2 >= r2)).astype(BF16))
    cum_rep = tuple(jnp.concatenate([m, m, m], axis=1) for m in cum)
    bd_ones = same.astype(F32)
    ones_rep = jnp.concatenate([same.astype(BF16), same.astype(BF16)], axis=0)
    bd_mask2 = jnp.concatenate([same, same], axis=1)
    colid = _iota2((1, LANES), 1)
    is_f = (colid % 8) >= 4

    hacc[...] = jnp.zeros_like(hacc)
    cn_s[...] = jnp.zeros_like(cn_s)
    m_s[...] = jnp.zeros_like(m_s)
    if has_state_in:
        m_s[...] = m0_ref[...]
        for s in range(ns):
            for d in range(2):
                for p in range(N_PAIRS):
                    for h in range(2):
                        cn_s[s, d, p, h * HD:(h + 1) * HD, h * HD:(h + 1) * HD] = c0_ref[s, d, 2 * p + h]
                    n_rows = jnp.where(same, jnp.broadcast_to(n0_ref[s, d, p], (2 * CHUNK, LANES)), 0.0)
                    cn_s[s, d, p, :, LANES:2 * LANES] = n_rows.T

    def gate_table(s, t0, d):
        gates = g_ref[s, pl.ds(t0, CHUNK), :]
        x = jnp.concatenate([gates, gates], axis=0)
        q = jnp.where(is_f, _cum_dot(cum_rep[d], x), x)
        return q, q.T

    chains = [(s, d, p) for s in range(ns) for d in range(2) for p in range(N_PAIRS)]

    def body(j, carry):
        t0s = (pl.multiple_of(j * CHUNK, CHUNK), pl.multiple_of((nc - 1 - j) * CHUNK, CHUNK))
        tabs = {(s, d): gate_table(s, t0s[d], d) for s in range(ns) for d in range(2)}
        st = []
        for s, d, p in chains:
            t0 = t0s[d]
            q_tab, q_tab_t = tabs[(s, d)]
            ji, jf = _gate_col(d, 0, p), _gate_col(d, 1, p)
            c = dict(t0=t0, sl=slice(p * LANES, (p + 1) * LANES))
            c['bcol'] = _pair_col(q_tab, jf, lo)
            c['icol'] = _pair_col(q_tab, ji, lo)
            brow = _pair_row(q_tab_t, jf, lo[0:1, :])
            irow = _pair_row(q_tab_t, ji, lo[0:1, :])
            c['q'] = qkv_ref[s, pl.ds(t0, CHUNK), p * LANES:(p + 1) * LANES].astype(F32)
            c['k'] = qkv_ref[s, pl.ds(t0, CHUNK), HW + p * LANES:HW + (p + 1) * LANES].astype(F32) * (HD ** -0.5)
            c['v'] = qkv_ref[s, pl.ds(t0, CHUNK), 2 * HW + p * LANES:2 * HW + (p + 1) * LANES].astype(F32)
            c['mprev'] = m_s[s, d, p]
            c['cn'] = cn_s[s, d, p]
            dmat = jnp.where(causal[d], c['bcol'] - brow + irow, -jnp.inf)
            inter = c['bcol'] + c['mprev']
            mx = jnp.where(lo,
                           jnp.max(jnp.where(lo, dmat, -jnp.inf), axis=1, keepdims=True),
                           jnp.max(jnp.where(lo, -jnp.inf, dmat), axis=1, keepdims=True))
            c['m_t'] = jnp.maximum(inter, mx)
            c['pexp'] = jnp.exp(dmat - c['m_t'])
            c['sc'] = jnp.exp(inter - c['m_t'])
            last = CHUNK - 1 if d == 0 else 0
            b_last = c['bcol'][last:last + 1, :]
            g = b_last - c['bcol'] + c['icol']
            c['m_new'] = jnp.maximum(b_last + c['mprev'], jnp.max(g, axis=0, keepdims=True))
            c['dec'] = jnp.exp(b_last + c['mprev'] - c['m_new'])
            c['wk'] = jnp.exp(g - c['m_new'])
            st.append(c)
        for c in st:
            c['s'] = _bdot_nt(c['q'], _bd_stack(c['k'], lo)) * c['pexp']
        for c in st:
            c['a1'] = _bdot(c['q'], c['cn'])
        for c in st:
            c['upd'] = _bdot_tn(c['wk'] * c['k'], jnp.concatenate([c['v'], jnp.ones_like(c['v'])], axis=1))
        for c in st:
            c['a2'] = _bdot(c['s'], jnp.concatenate([_bd_stack(c['v'], lo), bd_ones], axis=1))
        for (s, d, p), c in zip(chains, st):
            num = c['sc'] * c['a1'][:, 0:LANES] + c['a2'][:, 0:LANES]
            den = c['sc'] * c['a1'][:, LANES:] + c['a2'][:, LANES:]
            h = num / jnp.maximum(jnp.abs(den), jnp.exp(-c['m_t']))
            hacc[s, pl.ds(c['t0'], CHUNK), c['sl']] = hacc[s, pl.ds(c['t0'], CHUNK), c['sl']] + h
            cn_s[s, d, p] = (jnp.concatenate([c['dec'], c['dec']], axis=1) * c['cn']
                             + jnp.where(bd_mask2, c['upd'], 0.0))
            m_s[s, d, p] = c['m_new']
        return carry

    lax.fori_loop(0, nc, body, 0)
    if has_state_out:
        m_out_ref[...] = m_s[...]
        lo2 = _iota2((1, LANES), 1) < HD
        for s in range(ns):
            for d in range(2):
                for p in range(N_PAIRS):
                    for h in range(2):
                        c_out_ref[s, d, 2 * p + h] = cn_s[s, d, p, h * HD:(h + 1) * HD, h * HD:(h + 1) * HD]
                    n_t = cn_s[s, d, p, :, LANES:2 * LANES].T
                    n_out_ref[s, d, p] = jnp.where(lo2, n_t[0:1, :], n_t[HD:HD + 1, :])

    ng = ng_ref[...]
    rb = min(t, 256)

    def epilogue(i, carry):
        r0 = pl.multiple_of(i * rb, rb)
        for s in range(ns):
            h = hacc[s, pl.ds(r0, rb), :]
            hn = h * lax.rsqrt(_head_sum(h * h, ones_rep) * (1.0 / HD) + EPS) * ng
            o = oz_ref[s, pl.ds(r0, rb), 0:HW].astype(F32)
            z = oz_ref[s, pl.ds(r0, rb), HW:2 * HW].astype(F32)
            y_ref[s, pl.ds(r0, rb), :] = (hn * _sigmoid(o) * _silu(z)).astype(y_ref.dtype)
        return carry

    lax.fori_loop(0, t // rb, epilogue, 0)


def _seq_spec(ns, t, w, single_buffer):
    if single_buffer:
        return pl.BlockSpec((ns, t, w), lambda b: (b, 0, 0), pipeline_mode=pl.Buffered(1))
    return pl.BlockSpec((ns, t, w), lambda b: (b, 0, 0))


def _full_spec(shape):
    n = len(shape)
    return pl.BlockSpec(shape, lambda b: (0,) * n)


def _state_spec(ns, shape, layer=None):
    if layer is None:
        return pl.BlockSpec((ns,) + shape, lambda b: (b,) + (0,) * len(shape))
    return pl.BlockSpec((ns, None) + shape, lambda b: (b, layer) + (0,) * len(shape))


def _mixer_plan(bsz, in_bytes_per_seq, other_bytes_per_seq, max_seqs=2):
    plans = ((2, 2), (2, 1), (1, 2), (1, 1))
    if max_seqs >= 4:
        plans = ((4, 2),) + plans
    for ns, bufs in plans:
        if bsz % ns == 0 and ns * (bufs * in_bytes_per_seq + other_bytes_per_seq) <= MIXER_VMEM_BUDGET:
            return ns, bufs == 1
    return 1, True


def _pair_rows(x):
    return jnp.repeat(x.reshape(x.shape[:-1] + (N_PAIRS, 1, 2)), HD, axis=-1)


def _mlstm(qkv, oz, gates, norm_g, state_in, want_state, layer=None):
    bsz, t, _ = qkv.shape
    ns, big = _mixer_plan(bsz, t * (5 * HW * qkv.dtype.itemsize + LANES * 4), t * (HW * 2 * 2 + HW * 4))
    c_shape, row_shape = (2, N_HEADS, HD, HD), (2, N_PAIRS, 1, LANES)
    in_specs = [_seq_spec(ns, t, 3 * HW, big), _seq_spec(ns, t, 2 * HW, big),
                _seq_spec(ns, t, LANES, big), _full_spec((1, HW))]
    args = [qkv, oz, gates, norm_g]
    if state_in is not None:
        c0, n0, m0 = state_in
        in_specs += [_state_spec(ns, c_shape, layer), _state_spec(ns, row_shape, layer),
                     _state_spec(ns, row_shape, layer)]
        args += [c0, n0.reshape(n0.shape[:-3] + row_shape), _pair_rows(m0)]
    out_specs = [_seq_spec(ns, t, HW, False)]
    out_shape = [jax.ShapeDtypeStruct((bsz, t, HW), BF16)]
    if want_state:
        out_specs += [_state_spec(ns, c_shape), _state_spec(ns, row_shape), _state_spec(ns, row_shape)]
        out_shape += [jax.ShapeDtypeStruct((bsz,) + c_shape, F32),
                      jax.ShapeDtypeStruct((bsz,) + row_shape, F32),
                      jax.ShapeDtypeStruct((bsz,) + row_shape, F32)]
    outs = pl.pallas_call(
        functools.partial(_mlstm_kernel, has_state_in=state_in is not None, has_state_out=want_state),
        grid=(bsz // ns,),
        in_specs=in_specs,
        out_specs=out_specs,
        out_shape=out_shape,
        scratch_shapes=[pltpu.VMEM((ns, t, HW), F32),
                        pltpu.VMEM((ns, 2, N_PAIRS, 2 * CHUNK, 2 * LANES), F32),
                        pltpu.VMEM((ns,) + row_shape, F32)],
        compiler_params=pltpu.CompilerParams(dimension_semantics=("arbitrary",),
                                             vmem_limit_bytes=VMEM_LIMIT),
        name="mlstm",
    )(*args)
    if not want_state:
        return outs[0], None
    y, c1, n1, m1 = outs
    return y, (c1, n1.reshape(bsz, 2, N_HEADS, HD), m1[:, :, :, 0, ::HD].reshape(bsz, 2, N_HEADS))


def _rwkv_kernel(*refs, has_state_in, has_state_out):
    (rs_ref, rz_ref, w0_ref, a0_ref, w2_ref, a2_ref, kk_ref, ka_ref, rk_ref, ng_ref) = refs[0:10]
    refs = refs[10:]
    if has_state_in:
        s0_ref = refs[0]
        refs = refs[1:]
    y_ref = refs[0]
    refs = refs[1:]
    if has_state_out:
        s_out_ref = refs[0]
        refs = refs[1:]
    yacc, bonus, s_s = refs
    ns, t = rs_ref.shape[0], rs_ref.shape[1]
    nc = t // CHUNK
    lane = _iota2((CHUNK, LANES), 1)
    row = _iota2((CHUNK, LANES), 0)
    lo = lane < HD
    s_idx = lane % HD
    strict = (s_idx < row, s_idx > row)
    incl = (s_idx <= row, s_idx >= row)
    eye_pk = (s_idx == row).astype(F32)
    r64 = _iota2((CHUNK, CHUNK), 0)
    c64 = _iota2((CHUNK, CHUNK), 1)
    cum = ((c64 <= r64).astype(BF16), (c64 >= r64).astype(BF16))
    cum_rep = tuple(jnp.concatenate([m, m, m], axis=1) for m in cum)
    r2 = _iota2((2 * CHUNK, 2 * CHUNK), 0)
    c2 = _iota2((2 * CHUNK, 2 * CHUNK), 1)
    bd_mask = (r2 // CHUNK) == (c2 // CHUNK)
    ones_rep = jnp.concatenate([bd_mask.astype(BF16), bd_mask.astype(BF16)], axis=0)
    kk = kk_ref[...]
    ka = ka_ref[...]
    rk = rk_ref[...]

    yacc[...] = jnp.zeros_like(yacc)
    s_s[...] = jnp.zeros_like(s_s)
    if has_state_in:
        for s in range(ns):
            for d in range(2):
                for p in range(N_PAIRS):
                    for h in range(2):
                        s_s[s, d, p, h * HD:(h + 1) * HD, h * HD:(h + 1) * HD] = s0_ref[s, d, 2 * p + h]

    def prep(s, c, d):
        t0 = pl.multiple_of(c * CHUNK, CHUNK)
        blk = rs_ref[s, pl.ds(t0, CHUNK), :].astype(F32)
        r = blk[:, 0:HW]
        k = blk[:, HW:2 * HW]
        v = blk[:, 2 * HW:3 * HW]
        wl = blk[:, 3 * HW:3 * HW + 2 * LORA]
        al = blk[:, 3 * HW + 2 * LORA:3 * HW + 4 * LORA]
        dsl = slice(d * HW, (d + 1) * HW)
        logw = -RWKV_DECAY_SCALE * _sigmoid(w0_ref[:, dsl] + _bdot(jnp.tanh(wl), w2_ref[:, dsl]))
        a = _sigmoid(a0_ref[:, dsl] + _bdot(al, a2_ref[:, dsl]))
        kappa = k * kk
        khat = kappa * lax.rsqrt(jnp.maximum(_head_sum(kappa * kappa, ones_rep), 1e-24))
        kt = k * (1.0 + (a - 1.0) * ka)
        b = khat * a
        if d == 0:
            bonus[s, pl.ds(t0, CHUNK), :] = _head_sum(r * k * rk, ones_rep) * v
        last = CHUNK - 1 if d == 0 else 0
        logp = _cum_dot(cum_rep[d], logw)
        logp_last = logp[last:last + 1, :]
        e_neg = jnp.exp(-logp)
        e_last = jnp.exp(logp_last - logp)
        return dict(t0=t0, v=v, alpha=jnp.exp(logp - logw) * khat, beta=b * e_neg, kap=kt * e_neg,
                    rho=r * jnp.exp(logp), beta_l=b * e_last, kap_l=kt * e_last,
                    p_last=jnp.exp(logp_last))

    chains = [(s, d, p) for s in range(ns) for d in range(2) for p in range(N_PAIRS)]

    def body(j, carry):
        pre = {(s, d): prep(s, j if d == 0 else nc - 1 - j, d) for s in range(ns) for d in range(2)}
        st = []
        for s, d, p in chains:
            sl = slice(p * LANES, (p + 1) * LANES)
            c = {key: (val if key == 't0' else val[:, sl]) for key, val in pre[(s, d)].items()}
            c['sl'] = sl
            st.append(c)
        for c in st:
            c['g'] = _bdot_nt(jnp.concatenate([c['alpha'], c['rho']], axis=0),
                              jnp.concatenate([_bd_stack(c['beta'], lo), _bd_stack(c['kap'], lo)], axis=0))
        for (s, d, p), c in zip(chains, st):
            g = c.pop('g')
            c['n'] = jnp.where(strict[d], g[0:CHUNK, 0:LANES], 0.0)
            c['mk'] = jnp.where(strict[d], g[0:CHUNK, LANES:], 0.0)
            c['mrb'] = jnp.where(incl[d], g[CHUNK:, 0:LANES], 0.0)
            c['mrk'] = jnp.where(incl[d], g[CHUNK:, LANES:], 0.0)
            c['inv'] = eye_pk - c['n']
        for c in st:
            c['pw'] = _bdot(c['n'], _bd_stack(c['n'], lo))
        for c in st:
            c['mkv'] = _bdot(c['mk'], _bd_stack(c['v'], lo))
        for _ in range(4):
            for c in st:
                rp = _bdot(jnp.concatenate([c['inv'], c['pw']], axis=0), _bd_stack(c['pw'], lo))
                c['inv'] = c['inv'] + rp[0:CHUNK]
                c['pw'] = rp[CHUNK:]
        for c in st:
            c['inv'] = c['inv'] + _bdot(c['inv'], _bd_stack(c['pw'], lo))
        for c in st:
            c['wu'] = _bdot(c['inv'], jnp.concatenate([_bd_stack(c['alpha'], lo), _bd_stack(c['mkv'], lo)], axis=1))
        for (s, d, p), c in zip(chains, st):
            c['s_prev'] = s_s[s, d, p]
            ws = _bdot_nt(jnp.concatenate([c['wu'][:, 0:LANES], c['rho']], axis=0), c['s_prev'])
            c['u'] = ws[0:CHUNK] + c['wu'][:, LANES:]
            c['ys'] = ws[CHUNK:]
        for c in st:
            c['upd'] = _bdot_tn(jnp.concatenate([c['v'], -c['u']], axis=0),
                                jnp.concatenate([c['kap_l'], c['beta_l']], axis=0))
        for c in st:
            c['y'] = c['ys'] + _bdot(jnp.concatenate([c['mrk'], -c['mrb']], axis=1),
                                     jnp.concatenate([_bd_stack(c['v'], lo), _bd_stack(c['u'], lo)], axis=0))
        for (s, d, p), c in zip(chains, st):
            s_s[s, d, p] = c['s_prev'] * c['p_last'] + jnp.where(bd_mask, c['upd'], 0.0)
            yacc[s, pl.ds(c['t0'], CHUNK), c['sl']] = yacc[s, pl.ds(c['t0'], CHUNK), c['sl']] + c['y']
        return carry

    lax.fori_loop(0, nc, body, 0)
    if has_state_out:
        for s in range(ns):
            for d in range(2):
                for p in range(N_PAIRS):
                    for h in range(2):
                        s_out_ref[s, d, 2 * p + h] = s_s[s, d, p, h * HD:(h + 1) * HD, h * HD:(h + 1) * HD]

    ng = ng_ref[...]
    rb = min(t, 256)

    def epilogue(i, carry):
        r0 = pl.multiple_of(i * rb, rb)
        for s in range(ns):
            y = yacc[s, pl.ds(r0, rb), :]
            yn = (y * lax.rsqrt(_head_sum(y * y, ones_rep) * (1.0 / HD) + EPS) * ng
                  + bonus[s, pl.ds(r0, rb), :])
            y_ref[s, pl.ds(r0, rb), :] = (yn * _silu(rz_ref[s, pl.ds(r0, rb), :].astype(F32))).astype(y_ref.dtype)
        return carry

    lax.fori_loop(0, t // rb, epilogue, 0)


def _rwkv(rs, rz, w0, a0, w2bd, a2bd, kk, ka, rk, norm_g, state_in, want_state, layer=None):
    bsz, t, _ = rs.shape
    ns, big = _mixer_plan(bsz, t * (R_SHIFT + HW) * rs.dtype.itemsize, t * (HW * 2 * 2 + 2 * HW * 4),
                          max_seqs=RWKV_MAX_SEQS)
    s_shape = (2, N_HEADS, HD, HD)
    in_specs = [_seq_spec(ns, t, R_SHIFT, big), _seq_spec(ns, t, HW, big),
                _full_spec((1, 2 * HW)), _full_spec((1, 2 * HW)),
                _full_spec((2 * LORA, 2 * HW)), _full_spec((2 * LORA, 2 * HW)),
                _full_spec((1, HW)), _full_spec((1, HW)), _full_spec((1, HW)), _full_spec((1, HW))]
    args = [rs, rz, w0, a0, w2bd, a2bd, kk, ka, rk, norm_g]
    if state_in is not None:
        in_specs.append(_state_spec(ns, s_shape, layer))
        args.append(state_in)
    out_specs = [_seq_spec(ns, t, HW, False)]
    out_shape = [jax.ShapeDtypeStruct((bsz, t, HW), BF16)]
    if want_state:
        out_specs.append(_state_spec(ns, s_shape))
        out_shape.append(jax.ShapeDtypeStruct((bsz,) + s_shape, F32))
    outs = pl.pallas_call(
        functools.partial(_rwkv_kernel, has_state_in=state_in is not None, has_state_out=want_state),
        grid=(bsz // ns,),
        in_specs=in_specs,
        out_specs=out_specs,
        out_shape=out_shape,
        scratch_shapes=[pltpu.VMEM((ns, t, HW), F32), pltpu.VMEM((ns, t, HW), F32),
                        pltpu.VMEM((ns, 2, N_PAIRS, 2 * CHUNK, LANES), F32)],
        compiler_params=pltpu.CompilerParams(dimension_semantics=("arbitrary",),
                                             vmem_limit_bytes=VMEM_LIMIT),
        name="rwkv",
    )(*args)
    return (outs[0], outs[1]) if want_state else (outs[0], None)


def _lru_kernel(xz_ref, conv_ref, cb_ref, w_ref, b_ref, lam_ref, h0_ref, y_ref, hfin_ref, a_s, b_s, acc):
    t = xz_ref.shape[0]
    nb = t // LRU_BLOCK
    n_groups = LRU_BLOCK // SUBLANES
    row = _iota2((LRU_BLOCK, L_W), 0)
    sub = _iota2((n_groups, SUBLANES, L_W), 1)
    conv = conv_ref[...]
    cb = cb_ref[...]
    neg_c_sp = -LRU_C * _softplus(-lam_ref[...])

    def coefficients(j, carry):
        t0 = pl.multiple_of(j * LRU_BLOCK, LRU_BLOCK)
        x = xz_ref[pl.ds(t0, LRU_BLOCK), 0:L_W].astype(F32)
        prev_rows = xz_ref[pl.ds(pl.multiple_of(jnp.maximum(t0 - HALO, 0), HALO), HALO), 0:L_W]
        next_rows = xz_ref[pl.ds(pl.multiple_of(jnp.minimum(t0 + LRU_BLOCK, t - HALO), HALO), HALO), 0:L_W]
        prev_rows = jnp.where(j > 0, prev_rows.astype(F32), 0.0)
        next_rows = jnp.where(j < nb - 1, next_rows.astype(F32), 0.0)
        p1, p2 = prev_rows[HALO - 1:HALO, :], prev_rows[HALO - 2:HALO - 1, :]
        xm1 = jnp.where(row == 0, p1, pltpu.roll(x, 1, 0))
        xm2 = jnp.where(row == 0, p2, jnp.where(row == 1, p1, pltpu.roll(x, 2, 0)))
        xp1 = jnp.where(row == LRU_BLOCK - 1, next_rows[0:1, :], pltpu.roll(x, LRU_BLOCK - 1, 0))
        xc = conv[0:1, :] * xm2 + conv[1:2, :] * xm1 + conv[2:3, :] * x + conv[3:4, :] * xp1 + cb
        pre = _bdot(xc, w_ref[...]) + b_ref[...]
        for d in range(2):
            rg = _sigmoid(pre[:, 2 * d * L_W:(2 * d + 1) * L_W])
            ig = _sigmoid(pre[:, (2 * d + 1) * L_W:(2 * d + 2) * L_W])
            log_a = rg * neg_c_sp[:, d * L_W:(d + 1) * L_W]
            a_step = jnp.exp(log_a)
            b_step = jnp.sqrt(jnp.tanh(-log_a) * (1.0 + a_step * a_step)) * ig * xc
            a_cum = a_step.reshape(n_groups, SUBLANES, L_W)
            b_cum = b_step.reshape(n_groups, SUBLANES, L_W)
            k = 1
            while k < SUBLANES:
                keep = (sub >= k) if d == 0 else (sub < SUBLANES - k)
                shift = k if d == 0 else SUBLANES - k
                a_sh = jnp.where(keep, pltpu.roll(a_cum, shift, 1), 1.0)
                b_sh = jnp.where(keep, pltpu.roll(b_cum, shift, 1), 0.0)
                b_cum = a_cum * b_sh + b_cum
                a_cum = a_cum * a_sh
                k *= 2
            a_s[d, pl.ds(t0, LRU_BLOCK), :] = a_cum.reshape(LRU_BLOCK, L_W)
            b_s[d, pl.ds(t0, LRU_BLOCK), :] = b_cum.reshape(LRU_BLOCK, L_W)
        return carry

    lax.fori_loop(0, nb, coefficients, 0)

    def scan_block(d, j, carry):
        t0 = pl.multiple_of(j * LRU_BLOCK, LRU_BLOCK)
        a_cum = a_s[d, pl.ds(t0, LRU_BLOCK), :]
        b_cum = b_s[d, pl.ds(t0, LRU_BLOCK), :]
        pieces = [None] * n_groups
        for gi in (range(n_groups) if d == 0 else reversed(range(n_groups))):
            rows = slice(gi * SUBLANES, (gi + 1) * SUBLANES)
            hg = b_cum[rows, :] + a_cum[rows, :] * carry
            carry = hg[SUBLANES - 1:SUBLANES, :] if d == 0 else hg[0:1, :]
            pieces[gi] = hg
        return t0, jnp.concatenate(pieces, axis=0), carry

    def forward(j, carry):
        t0, h, carry = scan_block(0, j, carry)
        acc[pl.ds(t0, LRU_BLOCK), :] = h
        return carry

    def backward(jj, carry):
        t0, h, carry = scan_block(1, nb - 1 - jj, carry)
        z = xz_ref[pl.ds(t0, LRU_BLOCK), L_W:2 * L_W].astype(F32)
        y_ref[pl.ds(t0, LRU_BLOCK), :] = ((acc[pl.ds(t0, LRU_BLOCK), :] + h) * _silu(z)).astype(y_ref.dtype)
        return carry

    hfin_ref[0] = lax.fori_loop(0, nb, forward, h0_ref[0])
    hfin_ref[1] = lax.fori_loop(0, nb, backward, h0_ref[1])


def _lru(xz, conv, conv_b, wbd, bias, lam, h0):
    bsz, t, _ = xz.shape
    st = lambda shape: pl.BlockSpec((None,) + shape, lambda b: (b,) + (0,) * len(shape))
    return pl.pallas_call(
        _lru_kernel,
        grid=(bsz,),
        in_specs=[_seq_spec(None, t, 2 * L_W, False), _full_spec((CONV_W, L_W)), _full_spec((1, L_W)),
                  _full_spec((L_W, 4 * L_W)), _full_spec((1, 4 * L_W)), _full_spec((1, 2 * L_W)),
                  st((2, 1, L_W))],
        out_specs=[pl.BlockSpec((None, t, L_W), lambda b: (b, 0, 0)), st((2, 1, L_W))],
        out_shape=[jax.ShapeDtypeStruct((bsz, t, L_W), BF16),
                   jax.ShapeDtypeStruct((bsz, 2, 1, L_W), F32)],
        scratch_shapes=[pltpu.VMEM((2, t, L_W), F32), pltpu.VMEM((2, t, L_W), F32),
                        pltpu.VMEM((t, L_W), F32)],
        compiler_params=pltpu.CompilerParams(dimension_semantics=("arbitrary",),
                                             vmem_limit_bytes=VMEM_LIMIT),
        name="lru",
    )(xz, conv, conv_b, wbd, bias, lam, h0)


def _permute_w_in(w_in_l):
    valid = jnp.asarray(_GATE_SRC >= 0)
    gates = jnp.where(valid, w_in_l[:, 5 * HW:M_COLS][:, np.maximum(_GATE_SRC, 0)], 0.0)
    return jnp.concatenate([w_in_l[:, 0:5 * HW], gates, w_in_l[:, M_COLS:]], axis=1).astype(BF16)


def _gate_source_index():
    src = np.full((LANES,), -1, np.int32)
    for dd in range(2):
        for is_f in range(2):
            for p in range(N_PAIRS):
                for parity in range(2):
                    src[parity * HD + _gate_col(dd, is_f, p)] = (is_f * 2 * N_HEADS + dd * N_HEADS
                                                                 + 2 * p + parity)
    return src


_GATE_SRC = _gate_source_index()


def _gate_bias(m_bi_l, m_bf_l):
    flat = jnp.concatenate([m_bi_l.reshape(-1), m_bf_l.reshape(-1)])
    return jnp.where(jnp.asarray(_GATE_SRC >= 0), flat[np.maximum(_GATE_SRC, 0)], 0.0).reshape(1, LANES)


def _block_diag(blocks):
    n, a, b = blocks.shape
    eye = jnp.eye(n, dtype=blocks.dtype)
    return (eye[:, None, :, None] * blocks[:, :, None, :]).reshape(n * a, n * b)


def _layer_params(l, g_pre, g_post, w_in, w_out, m_bi, m_bf, m_norm, r_mu, r_w0, r_w2, r_a0, r_a2,
                  r_kk, r_ka, r_rk, r_norm, l_conv, l_conv_b, l_wa, l_ba, l_wx, l_bx, l_lambda):
    row = lambda v: v.reshape(1, -1)
    return dict(
        g_pre=row(g_pre[l]), g_post=row(g_post[l]),
        w_in=_permute_w_in(w_in[l]), w_out=w_out[l].astype(BF16),
        gate_bias=_gate_bias(m_bi[l], m_bf[l]), m_norm=row(m_norm[l]),
        r_mu=row(r_mu[l]), r_w0=row(r_w0[l]), r_a0=row(r_a0[l]),
        r_w2=_block_diag(r_w2[l]).astype(BF16), r_a2=_block_diag(r_a2[l]).astype(BF16),
        r_kk=row(r_kk[l]), r_ka=row(r_ka[l]), r_rk=row(r_rk[l]), r_norm=row(r_norm[l]),
        l_conv=l_conv[l], l_conv_b=row(l_conv_b[l]),
        l_w=jnp.concatenate([_block_diag(l_wa[l][0]), _block_diag(l_wx[l][0]),
                             _block_diag(l_wa[l][1]), _block_diag(l_wx[l][1])], axis=1).astype(BF16),
        l_b=jnp.concatenate([l_ba[l][0], l_bx[l][0], l_ba[l][1], l_bx[l][1]]).reshape(1, -1),
        l_lambda=row(l_lambda[l]),
    )


def _trunk_layer(x, mod, lp, states, want_state, per_seq_mod, grid_shift, layer=None):
    bsz = x.shape[0]
    qkv, oz, gates, rs, rz, lxz = _inproj(x, mod, lp['g_pre'], lp['w_in'], lp['r_mu'], lp['gate_bias'],
                                          per_seq_mod, grid_shift)
    m_in = None if states is None else states[0:3]
    r_in = None if states is None else states[3]
    if states is None:
        l_in = jnp.zeros((bsz, 2, 1, L_W), F32)
    else:
        l_in = (states[4] if layer is None else states[4][:, layer])[:, :, None, :]
    y_m, m_out = _mlstm(qkv, oz, gates, lp['m_norm'], m_in, want_state, layer)
    y_r, r_out = _rwkv(rs, rz, lp['r_w0'], lp['r_a0'], lp['r_w2'], lp['r_a2'], lp['r_kk'],
                       lp['r_ka'], lp['r_rk'], lp['r_norm'], r_in, want_state, layer)
    y_l, l_out = _lru(lxz, lp['l_conv'], lp['l_conv_b'], lp['l_w'], lp['l_b'], lp['l_lambda'], l_in)
    y = _outproj(y_m, y_r, y_l, x, mod, lp['g_post'], lp['w_out'], per_seq_mod)
    new_states = m_out + (r_out, l_out[:, :, 0, :]) if want_state else None
    return y, new_states


def kernel(x_prompt, x_sample, c, state_mlstm_C, state_mlstm_n, state_mlstm_m, state_rwkv, state_rglru, c_ctx, g_pre, g_post, w_mod, b_mod, w_in, w_out, m_bi, m_bf, m_norm, r_mu, r_w0, r_w2, r_a0, r_a2, r_kk, r_ka, r_rk, r_norm, l_conv, l_conv_b, l_wa, l_ba, l_wx, l_bx, l_lambda):
    depth = w_in.shape[0]
    bs = x_sample.shape[0]
    d = x_prompt.shape[-1]
    rows = -(-(1 + bs) // SUBLANES) * SUBLANES
    cc = jnp.zeros((rows, d), F32).at[0].set(c_ctx).at[1:1 + bs].set(c)
    mod = _modulation(cc, w_mod, b_mod)
    xp, xs = x_prompt, x_sample
    cache = tuple(a.astype(F32) for a in
                  (state_mlstm_C, state_mlstm_n, state_mlstm_m, state_rwkv, state_rglru))
    new_states = []
    for l in range(depth):
        lp = _layer_params(l, g_pre, g_post, w_in, w_out, m_bi, m_bf, m_norm, r_mu, r_w0, r_w2, r_a0,
                           r_a2, r_kk, r_ka, r_rk, r_norm, l_conv, l_conv_b, l_wa, l_ba, l_wx, l_bx,
                           l_lambda)
        xp, st = _trunk_layer(xp, mod[l, 0:1][:, None, :], lp, None, True, False, False)
        new_states.append(st)
        xs, _ = _trunk_layer(xs, mod[l, 1:1 + bs][:, None, :], lp, cache, False, True, True, layer=l)
    stacked = tuple(jnp.stack([st[i] for st in new_states], axis=1) for i in range(5))
    return (xp, xs) + stacked
```

```python
import functools

import numpy as np
import jax
import jax.numpy as jnp
from jax import lax
from jax.experimental import pallas as pl
from jax.experimental.pallas import tpu as pltpu

F32 = jnp.float32
BF16 = jnp.bfloat16
HIGHEST = lax.Precision.HIGHEST

D_MODEL = 1024
EPS = 1e-6
HD = 64
N_HEADS = 6
N_PAIRS = N_HEADS // 2
HW = N_HEADS * HD
CHUNK = 64
GRID_W = 64
LORA = 64
R_SHIFT = 3 * HW + 4 * LORA
L_W = 256
L_BLOCKS = 4
CONV_W = 4
LRU_C = 8.0
RWKV_DECAY_SCALE = 0.6065306597126334
M_COLS = 5 * HW + 4 * N_HEADS
IN_COLS = M_COLS + R_SHIFT + HW + 2 * L_W

LANES = 128
SUBLANES = 8
HALO = 16
HALO_ROWS = GRID_W
LRU_BLOCK = 128
RWKV_MAX_SEQS = 4
MLSTM_MAX_SEQS = 4
MIB = 1024 * 1024
V7X_VMEM_BYTES = 64 * MIB
VMEM_LIMIT = V7X_VMEM_BYTES - 4 * MIB
MIXER_SPILL_BYTES = 10 * MIB
MIXER_VMEM_BUDGET = VMEM_LIMIT - MIXER_SPILL_BYTES

SEG_WIDTHS = (3 * HW, 2 * HW, LANES, R_SHIFT, HW, 2 * L_W)
SEG_DTYPES = (BF16, BF16, F32, BF16, BF16, BF16)
SEG_OFFS = tuple(int(v) for v in np.cumsum((0,) + SEG_WIDTHS))
IN_COLS_PAD = SEG_OFFS[-1]
N_TILE = 512
IN_TILE = 512
OUT_TILE = 1024


def _gate_col(d, is_f, p):
    return 8 * d + 4 * is_f + p


def _bdot(a, b):
    return jnp.dot(a.astype(BF16), b.astype(BF16), preferred_element_type=F32)


def _bdot_nt(a, b):
    return lax.dot_general(a.astype(BF16), b.astype(BF16), (((1,), (1,)), ((), ())),
                           preferred_element_type=F32)


def _bdot_tn(a, b):
    return lax.dot_general(a.astype(BF16), b.astype(BF16), (((0,), (0,)), ((), ())),
                           preferred_element_type=F32)


def _fdot(a, b):
    return jnp.dot(a, b, precision=HIGHEST, preferred_element_type=F32)


def _split_bf16(x, terms):
    out = []
    for _ in range(terms - 1):
        hi = x.astype(BF16)
        out.append(hi)
        x = x - hi.astype(F32)
    out.append(x.astype(BF16))
    return out


def _cum_dot(cum_rep, x, terms=3):
    return jnp.dot(cum_rep, jnp.concatenate(_split_bf16(x, terms), axis=0), preferred_element_type=F32)


def _head_sum(x, ones_rep, terms=2):
    outs = []
    for p in range(x.shape[1] // LANES):
        parts = _split_bf16(x[:, p * LANES:(p + 1) * LANES], terms)
        outs.append(jnp.dot(jnp.concatenate(parts, axis=1), ones_rep, preferred_element_type=F32))
    return outs[0] if len(outs) == 1 else jnp.concatenate(outs, axis=1)


def _sigmoid(x):
    return 0.5 * jnp.tanh(0.5 * x) + 0.5


def _silu(x):
    return x * _sigmoid(x)


def _softplus(x):
    return jnp.maximum(x, 0.0) + jnp.log1p(jnp.exp(-jnp.abs(x)))


def _log_sigmoid(x):
    return -_softplus(-x)


def _iota2(shape, dim):
    return lax.broadcasted_iota(jnp.int32, shape, dim)


def _bd_stack(x, lo):
    return jnp.concatenate([jnp.where(lo, x, 0.0), jnp.where(lo, 0.0, x)], axis=0)


def _pair_col(q, j, lo):
    return jnp.where(lo, q[0:CHUNK, j:j + 1], q[0:CHUNK, HD + j:HD + j + 1])


def _pair_row(qt, j, lo_row):
    return jnp.where(lo_row, qt[j:j + 1, :], qt[HD + j:HD + j + 1, :])


def _mod_kernel(c_ref, w_ref, b_ref, o_ref):
    o_ref[...] = _bdot(_silu(c_ref[...]), w_ref[...]) + b_ref[...]


def _modulation(cc, w_mod, b_mod):
    depth, d, n = w_mod.shape
    rows = cc.shape[0]
    return pl.pallas_call(
        _mod_kernel,
        grid=(depth, n // N_TILE),
        in_specs=[pl.BlockSpec((rows, d), lambda l, j: (0, 0)),
                  pl.BlockSpec((None, d, N_TILE), lambda l, j: (l, 0, j)),
                  pl.BlockSpec((None, 1, N_TILE), lambda l, j: (l, 0, j))],
        out_specs=pl.BlockSpec((None, rows, N_TILE), lambda l, j: (l, 0, j)),
        out_shape=jax.ShapeDtypeStruct((depth, rows, n), F32),
        compiler_params=pltpu.CompilerParams(dimension_semantics=("arbitrary", "arbitrary")),
        name="modulation",
    )(cc, w_mod, b_mod.reshape(depth, 1, n))


SEG_QKV, SEG_OZ, SEG_GATES, SEG_RS, SEG_RZ, SEG_LXZ = range(6)


def _inproj_kernel(x_ref, xprev_ref, xnext_ref, mod_ref, g_ref, w_ref, mu_ref, gb_ref, *refs,
                   grid_shift, seq_len):
    tm, d = x_ref.shape
    single_tile = tm >= seq_len
    span = seq_len if single_tile else tm
    i = pl.program_id(1)
    n_tiles = pl.num_programs(1)
    mod = mod_ref[...]
    g = g_ref[...]

    def norm_mod(x):
        h = x * lax.rsqrt(jnp.mean(x * x, axis=-1, keepdims=True) + EPS) * g
        return (h * (1.0 + mod[:, d:2 * d]) + mod[:, 0:d]).astype(BF16)

    out_refs, h_s = refs[:-1], refs[-1]
    off = 0 if single_tile else HALO_ROWS
    h_s[off:off + tm, :] = norm_mod(x_ref[...])
    if not single_tile:
        h_s[0:off, :] = norm_mod(xprev_ref[...])
        h_s[off + tm:, :] = norm_mod(xnext_ref[...])
    has_prev = i > 0
    has_next = i < n_tiles - 1
    row = _iota2((tm, 1), 0) % span
    a, b = SEG_OFFS[SEG_RS], SEG_OFFS[SEG_RS + 1]
    o_ref = out_refs[SEG_RS]
    for n0 in range(a, b, N_TILE):
        n1 = min(n0 + N_TILE, b)
        u_all = jnp.dot(h_s[...], w_ref[:, n0:n1], preferred_element_type=F32)
        if single_tile:
            u = u_all
            u_prev = u_next = jnp.zeros((HALO_ROWS, n1 - n0), F32)
        else:
            u = u_all[HALO_ROWS:HALO_ROWS + tm]
            u_prev = jnp.where(has_prev, u_all[0:HALO_ROWS], 0.0)
            u_next = jnp.where(has_next, u_all[HALO_ROWS + tm:], 0.0)
        left = pltpu.roll(u, 1, 0)
        right = pltpu.roll(u, tm - 1, 0)
        if grid_shift:
            left = jnp.where(row % GRID_W == 0, 0.0, left)
            right = jnp.where(row % GRID_W == GRID_W - 1, 0.0, right)
            up = jnp.concatenate([u_prev, u[0:tm - GRID_W]], axis=0)
            down = jnp.concatenate([u[GRID_W:tm], u_next], axis=0)
            if tm > span:
                up = jnp.where(row < GRID_W, 0.0, up)
                down = jnp.where(row >= span - GRID_W, 0.0, down)
            sh = 0.25 * (up + down + left + right)
        else:
            left = jnp.where(row == 0, u_prev[HALO_ROWS - 1:HALO_ROWS, :], left)
            right = jnp.where(row == span - 1, u_next[0:1, :], right)
            sh = 0.5 * (left + right)
        o_ref[:, n0 - a:n1 - a] = (u + mu_ref[:, n0 - a:n1 - a] * (sh - u)).astype(o_ref.dtype)

    is_f = (_iota2((1, LANES), 1) % 8) >= 4
    a, b = SEG_OFFS[SEG_GATES], SEG_OFFS[SEG_GATES + 1]
    pre = jnp.dot(h_s[off:off + tm, :], w_ref[:, a:b], preferred_element_type=F32) + gb_ref[...]
    out_refs[SEG_GATES][...] = jnp.where(is_f, _log_sigmoid(pre), pre)

    for seg in (SEG_QKV, SEG_OZ, SEG_RZ, SEG_LXZ):
        o_ref, a, b = out_refs[seg], SEG_OFFS[seg], SEG_OFFS[seg + 1]
        for n0 in range(a, b, N_TILE):
            n1 = min(n0 + N_TILE, b)
            o_ref[:, n0 - a:n1 - a] = jnp.dot(h_s[off:off + tm, :], w_ref[:, n0:n1],
                                              preferred_element_type=F32).astype(o_ref.dtype)


def _inproj(x, mod, g_pre, w_in_p, r_mu, gate_bias, per_seq_mod, grid_shift):
    bsz0, t0, d = x.shape
    group = 1
    if not per_seq_mod and t0 < IN_TILE and IN_TILE % t0 == 0 and bsz0 % (IN_TILE // t0) == 0:
        group = IN_TILE // t0
    x = x.reshape(bsz0 // group, group * t0, d)
    bsz, t, _ = x.shape
    tm = min(t, IN_TILE)
    hb = tm // HALO_ROWS
    n_halo = t // HALO_ROWS
    mod_idx = (lambda b, i: (b, 0, 0)) if per_seq_mod else (lambda b, i: (0, 0, 0))
    outs = pl.pallas_call(
        functools.partial(_inproj_kernel, grid_shift=grid_shift, seq_len=t0),
        grid=(bsz, t // tm),
        in_specs=[pl.BlockSpec((None, tm, d), lambda b, i: (b, i, 0)),
                  pl.BlockSpec((None, HALO_ROWS, d), lambda b, i: (b, jnp.maximum(i * hb - 1, 0), 0)),
                  pl.BlockSpec((None, HALO_ROWS, d),
                               lambda b, i: (b, jnp.minimum((i + 1) * hb, n_halo - 1), 0)),
                  pl.BlockSpec((None, 1, 3 * d), mod_idx),
                  pl.BlockSpec((1, d), lambda b, i: (0, 0)),
                  pl.BlockSpec((d, IN_COLS_PAD), lambda b, i: (0, 0), pipeline_mode=pl.Buffered(1)),
                  pl.BlockSpec((1, R_SHIFT), lambda b, i: (0, 0)),
                  pl.BlockSpec((1, LANES), lambda b, i: (0, 0))],
        out_specs=[pl.BlockSpec((None, tm, w), lambda b, i: (b, i, 0)) for w in SEG_WIDTHS],
        out_shape=[jax.ShapeDtypeStruct((bsz, t, w), dt) for w, dt in zip(SEG_WIDTHS, SEG_DTYPES)],
        scratch_shapes=[pltpu.VMEM((tm if t == tm else tm + 2 * HALO_ROWS, d), BF16)],
        compiler_params=pltpu.CompilerParams(dimension_semantics=("arbitrary", "arbitrary"),
                                             vmem_limit_bytes=VMEM_LIMIT),
        name="inproj",
    )(x, x, x, mod, g_pre, w_in_p, r_mu, gate_bias)
    return [o.reshape(bsz0, t0, o.shape[-1]) for o in outs]


def _outproj_kernel(ym_ref, yr_ref, yl_ref, x_ref, mod_ref, g_ref, w_ref, o_ref):
    d = x_ref.shape[-1]
    o = jnp.dot(ym_ref[...], w_ref[0:HW, :], preferred_element_type=F32)
    o = o + jnp.dot(yr_ref[...], w_ref[HW:2 * HW, :], preferred_element_type=F32)
    o = o + jnp.dot(yl_ref[...], w_ref[2 * HW:2 * HW + L_W, :], preferred_element_type=F32)
    on = o * lax.rsqrt(jnp.mean(o * o, axis=-1, keepdims=True) + EPS) * g_ref[...]
    o_ref[...] = x_ref[...] + mod_ref[:, 2 * d:3 * d] * on


def _outproj(ym, yr, yl, x, mod, g_post, w_out_b, per_seq_mod):
    bsz0, t0, d = x.shape
    group = 1
    if not per_seq_mod and t0 < OUT_TILE and OUT_TILE % t0 == 0 and bsz0 % (OUT_TILE // t0) == 0:
        group = OUT_TILE // t0
    ym, yr, yl, x = (a.reshape(bsz0 // group, group * t0, a.shape[-1]) for a in (ym, yr, yl, x))
    bsz, t, _ = x.shape
    tm = min(t, OUT_TILE)
    mod_idx = (lambda b, i: (b, 0, 0)) if per_seq_mod else (lambda b, i: (0, 0, 0))
    tok = lambda w: pl.BlockSpec((None, tm, w), lambda b, i: (b, i, 0))
    out = pl.pallas_call(
        _outproj_kernel,
        grid=(bsz, t // tm),
        in_specs=[tok(HW), tok(HW), tok(L_W), tok(d),
                  pl.BlockSpec((None, 1, 3 * d), mod_idx),
                  pl.BlockSpec((1, d), lambda b, i: (0, 0)),
                  pl.BlockSpec((2 * HW + L_W, d), lambda b, i: (0, 0))],
        out_specs=tok(d),
        out_shape=jax.ShapeDtypeStruct((bsz, t, d), F32),
        compiler_params=pltpu.CompilerParams(dimension_semantics=("arbitrary", "arbitrary"),
                                             vmem_limit_bytes=VMEM_LIMIT),
        name="outproj",
    )(ym, yr, yl, x, mod, g_post, w_out_b)
    return out.reshape(bsz0, t0, d)


def _mlstm_kernel(*refs, has_state_in, has_state_out):
    qkv_ref, oz_ref, g_ref, ng_ref = refs[0:4]
    refs = refs[4:]
    if has_state_in:
        c0_ref, n0_ref, m0_ref = refs[0:3]
        refs = refs[3:]
    y_ref = refs[0]
    refs = refs[1:]
    if has_state_out:
        c_out_ref, n_out_ref, m_out_ref = refs[0:3]
        refs = refs[3:]
    hacc, cn_s, m_s = refs
    ns, t = qkv_ref.shape[0], qkv_ref.shape[1]
    nc = t // CHUNK
    lane = _iota2((CHUNK, LANES), 1)
    row = _iota2((CHUNK, LANES), 0)
    lo = lane < HD
    s_idx = lane % HD
    causal = (s_idx <= row, s_idx >= row)
    r2 = _iota2((2 * CHUNK, 2 * CHUNK), 0)
    c2 = _iota2((2 * CHUNK, 2 * CHUNK), 1)
    same = (r2 // CHUNK) == (c2 // CHUNK)
    cum = ((same & (c2 <= r2)).astype(BF16), (same & (c2 >= r2)).astype(BF16))
    cum_rep = tuple(jnp.concatenate([m, m, m], axis=1) for m in cum)
    bd_ones = same.astype(F32)
    ones_rep = jnp.concatenate([same.astype(BF16), same.astype(BF16)], axis=0)
    bd_mask2 = jnp.concatenate([same, same], axis=1)
    colid = _iota2((1, LANES), 1)
    is_f = (colid % 8) >= 4

    hacc[...] = jnp.zeros_like(hacc)
    cn_s[...] = jnp.zeros_like(cn_s)
    m_s[...] = jnp.zeros_like(m_s)
    if has_state_in:
        m_s[...] = m0_ref[...]
        for s in range(ns):
            for d in range(2):
                for p in range(N_PAIRS):
                    for h in range(2):
                        cn_s[s, d, p, h * HD:(h + 1) * HD, h * HD:(h + 1) * HD] = c0_ref[s, d, 2 * p + h]
                    n_rows = jnp.where(same, jnp.broadcast_to(n0_ref[s, d, p], (2 * CHUNK, LANES)), 0.0)
                    cn_s[s, d, p, :, LANES:2 * LANES] = n_rows.T

    def gate_table(s, t0, d):
        gates = g_ref[s, pl.ds(t0, CHUNK), :]
        x = jnp.concatenate([gates, gates], axis=0)
        q = jnp.where(is_f, _cum_dot(cum_rep[d], x), x)
        return q, q.T

    chains = [(s, d, p) for s in range(ns) for d in range(2) for p in range(N_PAIRS)]

    def body(j, carry):
        t0s = (pl.multiple_of(j * CHUNK, CHUNK), pl.multiple_of((nc - 1 - j) * CHUNK, CHUNK))
        tabs = {(s, d): gate_table(s, t0s[d], d) for s in range(ns) for d in range(2)}
        st = []
        for s, d, p in chains:
            t0 = t0s[d]
            q_tab, q_tab_t = tabs[(s, d)]
            ji, jf = _gate_col(d, 0, p), _gate_col(d, 1, p)
            c = dict(t0=t0, sl=slice(p * LANES, (p + 1) * LANES))
            c['bcol'] = _pair_col(q_tab, jf, lo)
            c['icol'] = _pair_col(q_tab, ji, lo)
            brow = _pair_row(q_tab_t, jf, lo[0:1, :])
            irow = _pair_row(q_tab_t, ji, lo[0:1, :])
            c['q'] = qkv_ref[s, pl.ds(t0, CHUNK), p * LANES:(p + 1) * LANES].astype(F32)
            c['k'] = qkv_ref[s, pl.ds(t0, CHUNK), HW + p * LANES:HW + (p + 1) * LANES].astype(F32) * (HD ** -0.5)
            c['v'] = qkv_ref[s, pl.ds(t0, CHUNK), 2 * HW + p * LANES:2 * HW + (p + 1) * LANES].astype(F32)
            c['mprev'] = m_s[s, d, p]
            c['cn'] = cn_s[s, d, p]
            dmat = jnp.where(causal[d], c['bcol'] - brow + irow, -jnp.inf)
            inter = c['bcol'] + c['mprev']
            mx = jnp.where(lo,
                           jnp.max(jnp.where(lo, dmat, -jnp.inf), axis=1, keepdims=True),
                           jnp.max(jnp.where(lo, -jnp.inf, dmat), axis=1, keepdims=True))
            c['m_t'] = jnp.maximum(inter, mx)
            c['pexp'] = jnp.exp(dmat - c['m_t'])
            c['sc'] = jnp.exp(inter - c['m_t'])
            last = CHUNK - 1 if d == 0 else 0
            b_last = c['bcol'][last:last + 1, :]
            g = b_last - c['bcol'] + c['icol']
            c['m_new'] = jnp.maximum(b_last + c['mprev'], jnp.max(g, axis=0, keepdims=True))
            c['dec'] = jnp.exp(b_last + c['mprev'] - c['m_new'])
            c['wk'] = jnp.exp(g - c['m_new'])
            st.append(c)
        for c in st:
            c['s'] = _bdot_nt(c['q'], _bd_stack(c['k'], lo)) * c['pexp']
        for c in st:
            c['a1'] = _bdot(c['q'], c['cn'])
        for c in st:
            c['upd'] = _bdot_tn(c['wk'] * c['k'], jnp.concatenate([c['v'], jnp.ones_like(c['v'])], axis=1))
        for c in st:
            c['a2'] = _bdot(c['s'], jnp.concatenate([_bd_stack(c['v'], lo), bd_ones], axis=1))
        for (s, d, p), c in zip(chains, st):
            num = c['sc'] * c['a1'][:, 0:LANES] + c['a2'][:, 0:LANES]
            den = c['sc'] * c['a1'][:, LANES:] + c['a2'][:, LANES:]
            h = num / jnp.maximum(jnp.abs(den), jnp.exp(-c['m_t']))
            hacc[s, pl.ds(c['t0'], CHUNK), c['sl']] = hacc[s, pl.ds(c['t0'], CHUNK), c['sl']] + h
            cn_s[s, d, p] = (jnp.concatenate([c['dec'], c['dec']], axis=1) * c['cn']
                             + jnp.where(bd_mask2, c['upd'], 0.0))
            m_s[s, d, p] = c['m_new']
        return carry

    lax.fori_loop(0, nc, body, 0)
    if has_state_out:
        m_out_ref[...] = m_s[...]
        lo2 = _iota2((1, LANES), 1) < HD
        for s in range(ns):
            for d in range(2):
                for p in range(N_PAIRS):
                    for h in range(2):
                        c_out_ref[s, d, 2 * p + h] = cn_s[s, d, p, h * HD:(h + 1) * HD, h * HD:(h + 1) * HD]
                    n_t = cn_s[s, d, p, :, LANES:2 * LANES].T
                    n_out_ref[s, d, p] = jnp.where(lo2, n_t[0:1, :], n_t[HD:HD + 1, :])

    ng = ng_ref[...]
    rb = min(t, 256)

    def epilogue(i, carry):
        r0 = pl.multiple_of(i * rb, rb)
        for s in range(ns):
            h = hacc[s, pl.ds(r0, rb), :]
            hn = h * lax.rsqrt(_head_sum(h * h, ones_rep) * (1.0 / HD) + EPS) * ng
            o = oz_ref[s, pl.ds(r0, rb), 0:HW].astype(F32)
            z = oz_ref[s, pl.ds(r0, rb), HW:2 * HW].astype(F32)
            y_ref[s, pl.ds(r0, rb), :] = (hn * _sigmoid(o) * _silu(z)).astype(y_ref.dtype)
        return carry

    lax.fori_loop(0, t // rb, epilogue, 0)


def _seq_spec(ns, t, w, single_buffer):
    if single_buffer:
        return pl.BlockSpec((ns, t, w), lambda b: (b, 0, 0), pipeline_mode=pl.Buffered(1))
    return pl.BlockSpec((ns, t, w), lambda b: (b, 0, 0))


def _full_spec(shape):
    n = len(shape)
    return pl.BlockSpec(shape, lambda b: (0,) * n)


def _state_spec(ns, shape, layer=None):
    if layer is None:
        return pl.BlockSpec((ns,) + shape, lambda b: (b,) + (0,) * len(shape))
    return pl.BlockSpec((ns, None) + shape, lambda b: (b, layer) + (0,) * len(shape))


def _mixer_plan(bsz, in_bytes_per_seq, other_bytes_per_seq, max_seqs=2):
    plans = ((2, 2), (2, 1), (1, 2), (1, 1))
    if max_seqs >= 4:
        plans = ((4, 2),) + plans
    for ns, bufs in plans:
        if bsz % ns == 0 and ns * (bufs * in_bytes_per_seq + other_bytes_per_seq) <= MIXER_VMEM_BUDGET:
            return ns, bufs == 1
    return 1, True


def _pair_rows(x):
    return jnp.repeat(x.reshape(x.shape[:-1] + (N_PAIRS, 1, 2)), HD, axis=-1)


def _mlstm(qkv, oz, gates, norm_g, state_in, want_state, layer=None):
    bsz, t, _ = qkv.shape
    ns, big = _mixer_plan(bsz, t * (5 * HW * qkv.dtype.itemsize + LANES * 4), t * (HW * 2 * 2 + HW * 4),
                          max_seqs=MLSTM_MAX_SEQS)
    c_shape, row_shape = (2, N_HEADS, HD, HD), (2, N_PAIRS, 1, LANES)
    in_specs = [_seq_spec(ns, t, 3 * HW, big), _seq_spec(ns, t, 2 * HW, big),
                _seq_spec(ns, t, LANES, big), _full_spec((1, HW))]
    args = [qkv, oz, gates, norm_g]
    if state_in is not None:
        c0, n0, m0 = state_in
        in_specs += [_state_spec(ns, c_shape, layer), _state_spec(ns, row_shape, layer),
                     _state_spec(ns, row_shape, layer)]
        args += [c0, n0.reshape(n0.shape[:-3] + row_shape), _pair_rows(m0)]
    out_specs = [_seq_spec(ns, t, HW, False)]
    out_shape = [jax.ShapeDtypeStruct((bsz, t, HW), BF16)]
    if want_state:
        out_specs += [_state_spec(ns, c_shape), _state_spec(ns, row_shape), _state_spec(ns, row_shape)]
        out_shape += [jax.ShapeDtypeStruct((bsz,) + c_shape, F32),
                      jax.ShapeDtypeStruct((bsz,) + row_shape, F32),
                      jax.ShapeDtypeStruct((bsz,) + row_shape, F32)]
    outs = pl.pallas_call(
        functools.partial(_mlstm_kernel, has_state_in=state_in is not None, has_state_out=want_state),
        grid=(bsz // ns,),
        in_specs=in_specs,
        out_specs=out_specs,
        out_shape=out_shape,
        scratch_shapes=[pltpu.VMEM((ns, t, HW), F32),
                        pltpu.VMEM((ns, 2, N_PAIRS, 2 * CHUNK, 2 * LANES), F32),
                        pltpu.VMEM((ns,) + row_shape, F32)],
        compiler_params=pltpu.CompilerParams(dimension_semantics=("arbitrary",),
                                             vmem_limit_bytes=VMEM_LIMIT),
        name="mlstm",
    )(*args)
    if not want_state:
        return outs[0], None
    y, c1, n1, m1 = outs
    return y, (c1, n1.reshape(bsz, 2, N_HEADS, HD), m1[:, :, :, 0, ::HD].reshape(bsz, 2, N_HEADS))


def _rwkv_kernel(*refs, has_state_in, has_state_out):
    (rs_ref, rz_ref, w0_ref, a0_ref, w2_ref, a2_ref, kk_ref, ka_ref, rk_ref, ng_ref) = refs[0:10]
    refs = refs[10:]
    if has_state_in:
        s0_ref = refs[0]
        refs = refs[1:]
    y_ref = refs[0]
    refs = refs[1:]
    if has_state_out:
        s_out_ref = refs[0]
        refs = refs[1:]
    yacc, bonus, s_s = refs
    ns, t = rs_ref.shape[0], rs_ref.shape[1]
    nc = t // CHUNK
    lane = _iota2((CHUNK, LANES), 1)
    row = _iota2((CHUNK, LANES), 0)
    lo = lane < HD
    s_idx = lane % HD
    strict = (s_idx < row, s_idx > row)
    incl = (s_idx <= row, s_idx >= row)
    eye_pk = (s_idx == row).astype(F32)
    r64 = _iota2((CHUNK, CHUNK), 0)
    c64 = _iota2((CHUNK, CHUNK), 1)
    cum = ((c64 <= r64).astype(BF16), (c64 >= r64).astype(BF16))
    cum_rep = tuple(jnp.concatenate([m, m, m], axis=1) for m in cum)
    r2 = _iota2((2 * CHUNK, 2 * CHUNK), 0)
    c2 = _iota2((2 * CHUNK, 2 * CHUNK), 1)
    bd_mask = (r2 // CHUNK) == (c2 // CHUNK)
    ones_rep = jnp.concatenate([bd_mask.astype(BF16), bd_mask.astype(BF16)], axis=0)
    kk = kk_ref[...]
    ka = ka_ref[...]
    rk = rk_ref[...]

    yacc[...] = jnp.zeros_like(yacc)
    s_s[...] = jnp.zeros_like(s_s)
    if has_state_in:
        for s in range(ns):
            for d in range(2):
                for p in range(N_PAIRS):
                    for h in range(2):
                        s_s[s, d, p, h * HD:(h + 1) * HD, h * HD:(h + 1) * HD] = s0_ref[s, d, 2 * p + h]

    def prep(s, c, d):
        t0 = pl.multiple_of(c * CHUNK, CHUNK)
        blk = rs_ref[s, pl.ds(t0, CHUNK), :].astype(F32)
        r = blk[:, 0:HW]
        k = blk[:, HW:2 * HW]
        v = blk[:, 2 * HW:3 * HW]
        wl = blk[:, 3 * HW:3 * HW + 2 * LORA]
        al = blk[:, 3 * HW + 2 * LORA:3 * HW + 4 * LORA]
        dsl = slice(d * HW, (d + 1) * HW)
        logw = -RWKV_DECAY_SCALE * _sigmoid(w0_ref[:, dsl] + _bdot(jnp.tanh(wl), w2_ref[:, dsl]))
        a = _sigmoid(a0_ref[:, dsl] + _bdot(al, a2_ref[:, dsl]))
        kappa = k * kk
        khat = kappa * lax.rsqrt(jnp.maximum(_head_sum(kappa * kappa, ones_rep), 1e-24))
        kt = k * (1.0 + (a - 1.0) * ka)
        b = khat * a
        if d == 0:
            bonus[s, pl.ds(t0, CHUNK), :] = _head_sum(r * k * rk, ones_rep) * v
        last = CHUNK - 1 if d == 0 else 0
        logp = _cum_dot(cum_rep[d], logw)
        logp_last = logp[last:last + 1, :]
        e_neg = jnp.exp(-logp)
        e_last = jnp.exp(logp_last - logp)
        return dict(t0=t0, v=v, alpha=jnp.exp(logp - logw) * khat, beta=b * e_neg, kap=kt * e_neg,
                    rho=r * jnp.exp(logp), beta_l=b * e_last, kap_l=kt * e_last,
                    p_last=jnp.exp(logp_last))

    chains = [(s, d, p) for s in range(ns) for d in range(2) for p in range(N_PAIRS)]

    def body(j, carry):
        pre = {(s, d): prep(s, j if d == 0 else nc - 1 - j, d) for s in range(ns) for d in range(2)}
        st = []
        for s, d, p in chains:
            sl = slice(p * LANES, (p + 1) * LANES)
            c = {key: (val if key == 't0' else val[:, sl]) for key, val in pre[(s, d)].items()}
            c['sl'] = sl
            st.append(c)
        for c in st:
            c['g'] = _bdot_nt(jnp.concatenate([c['alpha'], c['rho']], axis=0),
                              jnp.concatenate([_bd_stack(c['beta'], lo), _bd_stack(c['kap'], lo)], axis=0))
        for (s, d, p), c in zip(chains, st):
            g = c.pop('g')
            c['n'] = jnp.where(strict[d], g[0:CHUNK, 0:LANES], 0.0)
            c['mk'] = jnp.where(strict[d], g[0:CHUNK, LANES:], 0.0)
            c['mrb'] = jnp.where(incl[d], g[CHUNK:, 0:LANES], 0.0)
            c['mrk'] = jnp.where(incl[d], g[CHUNK:, LANES:], 0.0)
            c['inv'] = eye_pk - c['n']
        for c in st:
            c['pw'] = _bdot(c['n'], _bd_stack(c['n'], lo))
        for c in st:
            c['mkv'] = _bdot(c['mk'], _bd_stack(c['v'], lo))
        for _ in range(4):
            for c in st:
                rp = _bdot(jnp.concatenate([c['inv'], c['pw']], axis=0), _bd_stack(c['pw'], lo))
                c['inv'] = c['inv'] + rp[0:CHUNK]
                c['pw'] = rp[CHUNK:]
        for c in st:
            c['inv'] = c['inv'] + _bdot(c['inv'], _bd_stack(c['pw'], lo))
        for c in st:
            c['wu'] = _bdot(c['inv'], jnp.concatenate([_bd_stack(c['alpha'], lo), _bd_stack(c['mkv'], lo)], axis=1))
        for (s, d, p), c in zip(chains, st):
            c['s_prev'] = s_s[s, d, p]
            ws = _bdot_nt(jnp.concatenate([c['wu'][:, 0:LANES], c['rho']], axis=0), c['s_prev'])
            c['u'] = ws[0:CHUNK] + c['wu'][:, LANES:]
            c['ys'] = ws[CHUNK:]
        for c in st:
            c['upd'] = _bdot_tn(jnp.concatenate([c['v'], -c['u']], axis=0),
                                jnp.concatenate([c['kap_l'], c['beta_l']], axis=0))
        for c in st:
            c['y'] = c['ys'] + _bdot(jnp.concatenate([c['mrk'], -c['mrb']], axis=1),
                                     jnp.concatenate([_bd_stack(c['v'], lo), _bd_stack(c['u'], lo)], axis=0))
        for (s, d, p), c in zip(chains, st):
            s_s[s, d, p] = c['s_prev'] * c['p_last'] + jnp.where(bd_mask, c['upd'], 0.0)
            yacc[s, pl.ds(c['t0'], CHUNK), c['sl']] = yacc[s, pl.ds(c['t0'], CHUNK), c['sl']] + c['y']
        return carry

    lax.fori_loop(0, nc, body, 0)
    if has_state_out:
        for s in range(ns):
            for d in range(2):
                for p in range(N_PAIRS):
                    for h in range(2):
                        s_out_ref[s, d, 2 * p + h] = s_s[s, d, p, h * HD:(h + 1) * HD, h * HD:(h + 1) * HD]

    ng = ng_ref[...]
    rb = min(t, 256)

    def epilogue(i, carry):
        r0 = pl.multiple_of(i * rb, rb)
        for s in range(ns):
            y = yacc[s, pl.ds(r0, rb), :]
            yn = (y * lax.rsqrt(_head_sum(y * y, ones_rep) * (1.0 / HD) + EPS) * ng
                  + bonus[s, pl.ds(r0, rb), :])
            y_ref[s, pl.ds(r0, rb), :] = (yn * _silu(rz_ref[s, pl.ds(r0, rb), :].astype(F32))).astype(y_ref.dtype)
        return carry

    lax.fori_loop(0, t // rb, epilogue, 0)


def _rwkv(rs, rz, w0, a0, w2bd, a2bd, kk, ka, rk, norm_g, state_in, want_state, layer=None):
    bsz, t, _ = rs.shape
    ns, big = _mixer_plan(bsz, t * (R_SHIFT + HW) * rs.dtype.itemsize, t * (HW * 2 * 2 + 2 * HW * 4),
                          max_seqs=RWKV_MAX_SEQS)
    s_shape = (2, N_HEADS, HD, HD)
    in_specs = [_seq_spec(ns, t, R_SHIFT, big), _seq_spec(ns, t, HW, big),
                _full_spec((1, 2 * HW)), _full_spec((1, 2 * HW)),
                _full_spec((2 * LORA, 2 * HW)), _full_spec((2 * LORA, 2 * HW)),
                _full_spec((1, HW)), _full_spec((1, HW)), _full_spec((1, HW)), _full_spec((1, HW))]
    args = [rs, rz, w0, a0, w2bd, a2bd, kk, ka, rk, norm_g]
    if state_in is not None:
        in_specs.append(_state_spec(ns, s_shape, layer))
        args.append(state_in)
    out_specs = [_seq_spec(ns, t, HW, False)]
    out_shape = [jax.ShapeDtypeStruct((bsz, t, HW), BF16)]
    if want_state:
        out_specs.append(_state_spec(ns, s_shape))
        out_shape.append(jax.ShapeDtypeStruct((bsz,) + s_shape, F32))
    outs = pl.pallas_call(
        functools.partial(_rwkv_kernel, has_state_in=state_in is not None, has_state_out=want_state),
        grid=(bsz // ns,),
        in_specs=in_specs,
        out_specs=out_specs,
        out_shape=out_shape,
        scratch_shapes=[pltpu.VMEM((ns, t, HW), F32), pltpu.VMEM((ns, t, HW), F32),
                        pltpu.VMEM((ns, 2, N_PAIRS, 2 * CHUNK, LANES), F32)],
        compiler_params=pltpu.CompilerParams(dimension_semantics=("arbitrary",),
                                             vmem_limit_bytes=VMEM_LIMIT),
        name="rwkv",
    )(*args)
    return (outs[0], outs[1]) if want_state else (outs[0], None)


def _lru_kernel(xz_ref, conv_ref, cb_ref, w_ref, b_ref, lam_ref, h0_ref, y_ref, hfin_ref, a_s, b_s, acc):
    t = xz_ref.shape[0]
    nb = t // LRU_BLOCK
    n_groups = LRU_BLOCK // SUBLANES
    row = _iota2((LRU_BLOCK, L_W), 0)
    sub = _iota2((n_groups, SUBLANES, L_W), 1)
    conv = conv_ref[...]
    cb = cb_ref[...]
    neg_c_sp = -LRU_C * _softplus(-lam_ref[...])

    def coefficients(j, carry):
        t0 = pl.multiple_of(j * LRU_BLOCK, LRU_BLOCK)
        x = xz_ref[pl.ds(t0, LRU_BLOCK), 0:L_W].astype(F32)
        prev_rows = xz_ref[pl.ds(pl.multiple_of(jnp.maximum(t0 - HALO, 0), HALO), HALO), 0:L_W]
        next_rows = xz_ref[pl.ds(pl.multiple_of(jnp.minimum(t0 + LRU_BLOCK, t - HALO), HALO), HALO), 0:L_W]
        prev_rows = jnp.where(j > 0, prev_rows.astype(F32), 0.0)
        next_rows = jnp.where(j < nb - 1, next_rows.astype(F32), 0.0)
        p1, p2 = prev_rows[HALO - 1:HALO, :], prev_rows[HALO - 2:HALO - 1, :]
        xm1 = jnp.where(row == 0, p1, pltpu.roll(x, 1, 0))
        xm2 = jnp.where(row == 0, p2, jnp.where(row == 1, p1, pltpu.roll(x, 2, 0)))
        xp1 = jnp.where(row == LRU_BLOCK - 1, next_rows[0:1, :], pltpu.roll(x, LRU_BLOCK - 1, 0))
        xc = conv[0:1, :] * xm2 + conv[1:2, :] * xm1 + conv[2:3, :] * x + conv[3:4, :] * xp1 + cb
        pre = _bdot(xc, w_ref[...]) + b_ref[...]
        for d in range(2):
            rg = _sigmoid(pre[:, 2 * d * L_W:(2 * d + 1) * L_W])
            ig = _sigmoid(pre[:, (2 * d + 1) * L_W:(2 * d + 2) * L_W])
            log_a = rg * neg_c_sp[:, d * L_W:(d + 1) * L_W]
            a_step = jnp.exp(log_a)
            b_step = jnp.sqrt(jnp.tanh(-log_a) * (1.0 + a_step * a_step)) * ig * xc
            a_cum = a_step.reshape(n_groups, SUBLANES, L_W)
            b_cum = b_step.reshape(n_groups, SUBLANES, L_W)
            k = 1
            while k < SUBLANES:
                keep = (sub >= k) if d == 0 else (sub < SUBLANES - k)
                shift = k if d == 0 else SUBLANES - k
                a_sh = jnp.where(keep, pltpu.roll(a_cum, shift, 1), 1.0)
                b_sh = jnp.where(keep, pltpu.roll(b_cum, shift, 1), 0.0)
                b_cum = a_cum * b_sh + b_cum
                a_cum = a_cum * a_sh
                k *= 2
            a_s[d, pl.ds(t0, LRU_BLOCK), :] = a_cum.reshape(LRU_BLOCK, L_W)
            b_s[d, pl.ds(t0, LRU_BLOCK), :] = b_cum.reshape(LRU_BLOCK, L_W)
        return carry

    lax.fori_loop(0, nb, coefficients, 0)

    def scan_block(d, j, carry):
        t0 = pl.multiple_of(j * LRU_BLOCK, LRU_BLOCK)
        a_cum = a_s[d, pl.ds(t0, LRU_BLOCK), :]
        b_cum = b_s[d, pl.ds(t0, LRU_BLOCK), :]
        pieces = [None] * n_groups
        for gi in (range(n_groups) if d == 0 else reversed(range(n_groups))):
            rows = slice(gi * SUBLANES, (gi + 1) * SUBLANES)
            hg = b_cum[rows, :] + a_cum[rows, :] * carry
            carry = hg[SUBLANES - 1:SUBLANES, :] if d == 0 else hg[0:1, :]
            pieces[gi] = hg
        return t0, jnp.concatenate(pieces, axis=0), carry

    def forward(j, carry):
        t0, h, carry = scan_block(0, j, carry)
        acc[pl.ds(t0, LRU_BLOCK), :] = h
        return carry

    def backward(jj, carry):
        t0, h, carry = scan_block(1, nb - 1 - jj, carry)
        z = xz_ref[pl.ds(t0, LRU_BLOCK), L_W:2 * L_W].astype(F32)
        y_ref[pl.ds(t0, LRU_BLOCK), :] = ((acc[pl.ds(t0, LRU_BLOCK), :] + h) * _silu(z)).astype(y_ref.dtype)
        return carry

    hfin_ref[0] = lax.fori_loop(0, nb, forward, h0_ref[0])
    hfin_ref[1] = lax.fori_loop(0, nb, backward, h0_ref[1])


def _lru(xz, conv, conv_b, wbd, bias, lam, h0):
    bsz, t, _ = xz.shape
    st = lambda shape: pl.BlockSpec((None,) + shape, lambda b: (b,) + (0,) * len(shape))
    return pl.pallas_call(
        _lru_kernel,
        grid=(bsz,),
        in_specs=[_seq_spec(None, t, 2 * L_W, False), _full_spec((CONV_W, L_W)), _full_spec((1, L_W)),
                  _full_spec((L_W, 4 * L_W)), _full_spec((1, 4 * L_W)), _full_spec((1, 2 * L_W)),
                  st((2, 1, L_W))],
        out_specs=[pl.BlockSpec((None, t, L_W), lambda b: (b, 0, 0)), st((2, 1, L_W))],
        out_shape=[jax.ShapeDtypeStruct((bsz, t, L_W), BF16),
                   jax.ShapeDtypeStruct((bsz, 2, 1, L_W), F32)],
        scratch_shapes=[pltpu.VMEM((2, t, L_W), F32), pltpu.VMEM((2, t, L_W), F32),
                        pltpu.VMEM((t, L_W), F32)],
        compiler_params=pltpu.CompilerParams(dimension_semantics=("arbitrary",),
                                             vmem_limit_bytes=VMEM_LIMIT),
        name="lru",
    )(xz, conv, conv_b, wbd, bias, lam, h0)


def _permute_w_in(w_in_l):
    valid = jnp.asarray(_GATE_SRC >= 0)
    gates = jnp.where(valid, w_in_l[:, 5 * HW:M_COLS][:, np.maximum(_GATE_SRC, 0)], 0.0)
    return jnp.concatenate([w_in_l[:, 0:5 * HW], gates, w_in_l[:, M_COLS:]], axis=1).astype(BF16)


def _gate_source_index():
    src = np.full((LANES,), -1, np.int32)
    for dd in range(2):
        for is_f in range(2):
            for p in range(N_PAIRS):
                for parity in range(2):
                    src[parity * HD + _gate_col(dd, is_f, p)] = (is_f * 2 * N_HEADS + dd * N_HEADS
                                                                 + 2 * p + parity)
    return src


_GATE_SRC = _gate_source_index()


def _gate_bias(m_bi_l, m_bf_l):
    flat = jnp.concatenate([m_bi_l.reshape(-1), m_bf_l.reshape(-1)])
    return jnp.where(jnp.asarray(_GATE_SRC >= 0), flat[np.maximum(_GATE_SRC, 0)], 0.0).reshape(1, LANES)


def _block_diag(blocks):
    n, a, b = blocks.shape
    eye = jnp.eye(n, dtype=blocks.dtype)
    return (eye[:, None, :, None] * blocks[:, :, None, :]).reshape(n * a, n * b)


def _layer_params(l, g_pre, g_post, w_in, w_out, m_bi, m_bf, m_norm, r_mu, r_w0, r_w2, r_a0, r_a2,
                  r_kk, r_ka, r_rk, r_norm, l_conv, l_conv_b, l_wa, l_ba, l_wx, l_bx, l_lambda):
    row = lambda v: v.reshape(1, -1)
    return dict(
        g_pre=row(g_pre[l]), g_post=row(g_post[l]),
        w_in=_permute_w_in(w_in[l]), w_out=w_out[l].astype(BF16),
        gate_bias=_gate_bias(m_bi[l], m_bf[l]), m_norm=row(m_norm[l]),
        r_mu=row(r_mu[l]), r_w0=row(r_w0[l]), r_a0=row(r_a0[l]),
        r_w2=_block_diag(r_w2[l]).astype(BF16), r_a2=_block_diag(r_a2[l]).astype(BF16),
        r_kk=row(r_kk[l]), r_ka=row(r_ka[l]), r_rk=row(r_rk[l]), r_norm=row(r_norm[l]),
        l_conv=l_conv[l], l_conv_b=row(l_conv_b[l]),
        l_w=jnp.concatenate([_block_diag(l_wa[l][0]), _block_diag(l_wx[l][0]),
                             _block_diag(l_wa[l][1]), _block_diag(l_wx[l][1])], axis=1).astype(BF16),
        l_b=jnp.concatenate([l_ba[l][0], l_bx[l][0], l_ba[l][1], l_bx[l][1]]).reshape(1, -1),
        l_lambda=row(l_lambda[l]),
    )


def _trunk_layer(x, mod, lp, states, want_state, per_seq_mod, grid_shift, layer=None):
    bsz = x.shape[0]
    qkv, oz, gates, rs, rz, lxz = _inproj(x, mod, lp['g_pre'], lp['w_in'], lp['r_mu'], lp['gate_bias'],
                                          per_seq_mod, grid_shift)
    m_in = None if states is None else states[0:3]
    r_in = None if states is None else states[3]
    if states is None:
        l_in = jnp.zeros((bsz, 2, 1, L_W), F32)
    else:
        l_in = (states[4] if layer is None else states[4][:, layer])[:, :, None, :]
    y_m, m_out = _mlstm(qkv, oz, gates, lp['m_norm'], m_in, want_state, layer)
    y_r, r_out = _rwkv(rs, rz, lp['r_w0'], lp['r_a0'], lp['r_w2'], lp['r_a2'], lp['r_kk'],
                       lp['r_ka'], lp['r_rk'], lp['r_norm'], r_in, want_state, layer)
    y_l, l_out = _lru(lxz, lp['l_conv'], lp['l_conv_b'], lp['l_w'], lp['l_b'], lp['l_lambda'], l_in)
    y = _outproj(y_m, y_r, y_l, x, mod, lp['g_post'], lp['w_out'], per_seq_mod)
    new_states = m_out + (r_out, l_out[:, :, 0, :]) if want_state else None
    return y, new_states


def kernel(x_prompt, x_sample, c, state_mlstm_C, state_mlstm_n, state_mlstm_m, state_rwkv, state_rglru, c_ctx, g_pre, g_post, w_mod, b_mod, w_in, w_out, m_bi, m_bf, m_norm, r_mu, r_w0, r_w2, r_a0, r_a2, r_kk, r_ka, r_rk, r_norm, l_conv, l_conv_b, l_wa, l_ba, l_wx, l_bx, l_lambda):
    depth = w_in.shape[0]
    bs = x_sample.shape[0]
    d = x_prompt.shape[-1]
    rows = -(-(1 + bs) // SUBLANES) * SUBLANES
    cc = jnp.zeros((rows, d), F32).at[0].set(c_ctx).at[1:1 + bs].set(c)
    mod = _modulation(cc, w_mod, b_mod)
    xp, xs = x_prompt, x_sample
    cache = tuple(a.astype(F32) for a in
                  (state_mlstm_C, state_mlstm_n, state_mlstm_m, state_rwkv, state_rglru))
    new_states = []
    for l in range(depth):
        lp = _layer_params(l, g_pre, g_post, w_in, w_out, m_bi, m_bf, m_norm, r_mu, r_w0, r_w2, r_a0,
                           r_a2, r_kk, r_ka, r_rk, r_norm, l_conv, l_conv_b, l_wa, l_ba, l_wx, l_bx,
                           l_lambda)
        xp, st = _trunk_layer(xp, mod[l, 0:1][:, None, :], lp, None, True, False, False)
        new_states.append(st)
        xs, _ = _trunk_layer(xs, mod[l, 1:1 + bs][:, None, :], lp, cache, False, True, True, layer=l)
    stacked = tuple(jnp.stack([st[i] for st in new_states], axis=1) for i in range(5))
    return (xp, xs) + stacked
```
